```python
import jax, jax.numpy as jnp
from jax import lax
import numpy as np

D_MODEL = 2048
BATCH = 2
SEQ = 4096
DEPTH = 2

N_MIXERS = 2
N_GLA_LAYERS = (DEPTH + 1) // 2
N_FOX_LAYERS = DEPTH // 2

GLA_HEADS = 4
GLA_KEY_DIM = D_MODEL // 2
GLA_VAL_DIM = D_MODEL
GLA_HEAD_K = GLA_KEY_DIM // GLA_HEADS
GLA_HEAD_V = GLA_VAL_DIM // GLA_HEADS
GLA_GATE_RANK = 16
GLA_GATE_TEMP = 16.0
GLA_CHUNK = 64
GLA_IN = 2 * GLA_KEY_DIM + 2 * GLA_VAL_DIM + GLA_GATE_RANK

FOX_HEAD_DIM = 128
FOX_HEADS = D_MODEL // FOX_HEAD_DIM
FOX_BLOCK = 128
FOX_IN = 4 * D_MODEL + FOX_HEADS

D_FF = ((8 * D_MODEL + 2) // 3 + 255) // 256 * 256

RMS_EPS = 1e-6

kernel_name = "hybrid_gla_fox_interleaved"


def rms_norm(x, g):
    xf = x.astype(jnp.float32)
    y = xf * lax.rsqrt(jnp.mean(xf * xf, axis=-1, keepdims=True) + RMS_EPS)
    return (y * g.astype(jnp.float32)).astype(x.dtype)


def gla_mixer(h, w_in, w_g2, b_g2, o_gain, w_o):
    B, S, _ = h.shape
    H, HK, HV, C = GLA_HEADS, GLA_HEAD_K, GLA_HEAD_V, GLA_CHUNK
    proj = h @ w_in
    q, k, v, r, g_lr = jnp.split(
        proj, [GLA_KEY_DIM, 2 * GLA_KEY_DIM, 2 * GLA_KEY_DIM + GLA_VAL_DIM,
               2 * GLA_KEY_DIM + 2 * GLA_VAL_DIM], axis=-1)
    log_a = jax.nn.log_sigmoid((g_lr @ w_g2 + b_g2).astype(jnp.float32)) / GLA_GATE_TEMP

    def heads(t, dh):
        return t.reshape(B, S, H, dh).transpose(0, 2, 1, 3).astype(jnp.float32)

    q = heads(q, HK) * (HK ** -0.5)
    k = heads(k, HK)
    v = heads(v, HV)
    la = heads(log_a, HK)
    nc = S // C

    def chunks(t):
        return t.reshape(B, H, nc, C, t.shape[-1]).transpose(2, 0, 1, 3, 4)

    qc, kc, vc = chunks(q), chunks(k), chunks(v)
    bc = jnp.cumsum(chunks(la), axis=-2)
    causal = jnp.tril(jnp.ones((C, C), dtype=bool))

    def step(state, inp):
        q_i, k_i, v_i, b_i = inp
        diff = b_i[:, :, :, None, :] - b_i[:, :, None, :, :]
        decay = jnp.exp(jnp.where(causal[:, :, None], diff, -jnp.inf))
        attn = jnp.einsum('bhtk,bhsk,bhtsk->bhts', q_i, k_i, decay)
        o = attn @ v_i + jnp.einsum('bhtk,bhkv->bhtv', q_i * jnp.exp(b_i), state)
        b_last = b_i[:, :, -1:, :]
        state = (jnp.exp(b_last[:, :, 0, :])[..., None] * state
                 + jnp.einsum('bhsk,bhsv->bhkv', k_i * jnp.exp(b_last - b_i), v_i))
        return state, o

    state0 = jnp.zeros((B, H, HK, HV), jnp.float32)
    _, o = lax.scan(step, state0, (qc, kc, vc, bc))
    o = o.transpose(1, 0, 3, 2, 4).reshape(B, S, H, HV)
    o = rms_norm(o, o_gain).reshape(B, S, GLA_VAL_DIM).astype(h.dtype)
    return (o * jax.nn.silu(r)) @ w_o


def fox_mixer(h, w_in, b_f, q_gain, k_gain, w_o):
    B, S, _ = h.shape
    H, HD, BLK = FOX_HEADS, FOX_HEAD_DIM, FOX_BLOCK
    proj = h @ w_in
    q, k, v, f_lr, og = jnp.split(
        proj, [D_MODEL, 2 * D_MODEL, 3 * D_MODEL, 3 * D_MODEL + H], axis=-1)
    q = rms_norm(q.reshape(B, S, H, HD), q_gain).transpose(0, 2, 1, 3)
    k = rms_norm(k.reshape(B, S, H, HD), k_gain).transpose(0, 2, 1, 3)
    v = v.reshape(B, S, H, HD).transpose(0, 2, 1, 3)
    log_f = jax.nn.log_sigmoid((f_lr + b_f).astype(jnp.float32))
    c = jnp.cumsum(log_f.transpose(0, 2, 1), axis=-1)
    scale = HD ** -0.5
    nb = S // BLK
    q_blocks = q.reshape(B, H, nb, BLK, HD).transpose(2, 0, 1, 3, 4)
    cq_blocks = c.reshape(B, H, nb, BLK).transpose(2, 0, 1, 3)
    key_pos = jnp.arange(S)

    def block(args):
        qb, cqb, i = args
        q_pos = i * BLK + jnp.arange(BLK)
        logits = jnp.einsum('bhqd,bhkd->bhqk', qb, k).astype(jnp.float32) * scale
        logits = logits + (cqb[..., None] - c[:, :, None, :])
        logits = jnp.where(key_pos[None, :] <= q_pos[:, None], logits, -jnp.inf)
        p = jax.nn.softmax(logits, axis=-1)
        return jnp.einsum('bhqk,bhkd->bhqd', p.astype(v.dtype), v)

    o = lax.map(block, (q_blocks, cq_blocks, jnp.arange(nb)))
    o = o.transpose(1, 0, 3, 2, 4).reshape(B, S, D_MODEL)
    return (o * jax.nn.sigmoid(og)) @ w_o


def swiglu(h, w_gate, w_up, w_down):
    return (jax.nn.silu(h @ w_gate) * (h @ w_up)) @ w_down


def setup_inputs(seed: int = 0) -> dict:
    key = jax.random.key(seed)
    ks = jax.random.split(key, 17)
    f32 = jnp.float32

    def nrm(k, shape, scale):
        return jax.random.normal(k, shape, f32) * scale

    D = D_MODEL
    return {
        "x": nrm(ks[0], (BATCH, SEQ, D), 1.0),
        "norm_mix": 1.0 + nrm(ks[1], (DEPTH, D), 0.02),
        "norm_ffn": 1.0 + nrm(ks[2], (DEPTH, D), 0.02),
        "gla_w_in": nrm(ks[3], (N_GLA_LAYERS, D, GLA_IN), D ** -0.5),
        "gla_w_g2": nrm(ks[4], (N_GLA_LAYERS, GLA_GATE_RANK, GLA_KEY_DIM), GLA_GATE_RANK ** -0.5),
        "gla_b_g2": nrm(ks[5], (N_GLA_LAYERS, GLA_KEY_DIM), 0.02),
        "gla_o_gain": 1.0 + nrm(ks[6], (N_GLA_LAYERS, GLA_HEAD_V), 0.02),
        "gla_w_o": nrm(ks[7], (N_GLA_LAYERS, GLA_VAL_DIM, D), GLA_VAL_DIM ** -0.5),
        "fox_w_in": nrm(ks[8], (N_FOX_LAYERS, D, FOX_IN), D ** -0.5),
        "fox_b_f": 1.0 + nrm(ks[9], (N_FOX_LAYERS, FOX_HEADS), 0.1),
        "fox_q_gain": 1.0 + nrm(ks[10], (N_FOX_LAYERS, FOX_HEAD_DIM), 0.02),
        "fox_k_gain": 1.0 + nrm(ks[11], (N_FOX_LAYERS, FOX_HEAD_DIM), 0.02),
        "fox_w_o": nrm(ks[12], (N_FOX_LAYERS, D, D), D ** -0.5),
        "ffn_w_gate": nrm(ks[13], (DEPTH, D, D_FF), D ** -0.5),
        "ffn_w_up": nrm(ks[14], (DEPTH, D, D_FF), D ** -0.5),
        "ffn_w_down": nrm(ks[15], (DEPTH, D_FF, D), D_FF ** -0.5),
    }


def reference(x, norm_mix, norm_ffn, gla_w_in, gla_w_g2, gla_b_g2, gla_o_gain, gla_w_o,
              fox_w_in, fox_b_f, fox_q_gain, fox_k_gain, fox_w_o,
              ffn_w_gate, ffn_w_up, ffn_w_down):
    for i in range(DEPTH):
        h = rms_norm(x, norm_mix[i])
        j = i // N_MIXERS
        if i % N_MIXERS == 0:
            mix = gla_mixer(h, gla_w_in[j], gla_w_g2[j], gla_b_g2[j], gla_o_gain[j], gla_w_o[j])
        else:
            mix = fox_mixer(h, fox_w_in[j], fox_b_f[j], fox_q_gain[j], fox_k_gain[j], fox_w_o[j])
        x = x + mix
        h = rms_norm(x, norm_ffn[i])
        x = x + swiglu(h, ffn_w_gate[i], ffn_w_up[i], ffn_w_down[i])
    return x
```

```python
import functools

import jax
import jax.numpy as jnp
from jax import lax
from jax.experimental import pallas as pl
from jax.experimental.pallas import tpu as pltpu

F32 = jnp.float32
BF16 = jnp.bfloat16

D_MODEL = 2048
RMS_EPS = 1e-6

GLA_HEADS = 4
GLA_KEY_DIM = D_MODEL // 2
GLA_VAL_DIM = D_MODEL
GLA_HEAD_K = GLA_KEY_DIM // GLA_HEADS
GLA_HEAD_V = GLA_VAL_DIM // GLA_HEADS
GLA_GATE_RANK = 16
GLA_GATE_TEMP = 16.0
GLA_CHUNK = 64
GLA_MAIN = 2 * GLA_KEY_DIM + 2 * GLA_VAL_DIM
GLA_DIAG = 8

FOX_HEAD_DIM = 128
FOX_HEADS = D_MODEL // FOX_HEAD_DIM
FOX_MAIN = 4 * D_MODEL
FOX_SCALE = FOX_HEAD_DIM ** -0.5

D_FF = ((8 * D_MODEL + 2) // 3 + 255) // 256 * 256

LANES = 128
NORM_SLAB = 256
VMEM_LIMIT = 56 * 1024 * 1024

NT_DIMS = (((1,), (1,)), ((), ()))
TN_DIMS = (((0,), (0,)), ((), ()))


def _params(*sem):
    return pltpu.CompilerParams(dimension_semantics=sem, vmem_limit_bytes=VMEM_LIMIT)


def _rmsnorm_rows(x, g):
    ms = jnp.mean(x * x, axis=-1, keepdims=True)
    return x * lax.rsqrt(ms + RMS_EPS) * g


def _log_sigmoid(z):
    return jnp.minimum(z, 0.0) - jnp.log1p(jnp.exp(-jnp.abs(z)))


def _norm_to_scratch(x_ref, g_ref, h_ref):
    rows = x_ref.shape[0]
    slab = min(NORM_SLAB, rows)
    g = g_ref[...]
    for r0 in range(0, rows, slab):
        h_ref[r0:r0 + slab, :] = _rmsnorm_rows(x_ref[r0:r0 + slab, :], g).astype(BF16)


def _split3(a):
    hi = a.astype(BF16)
    r1 = a - hi.astype(F32)
    mid = r1.astype(BF16)
    lo = (r1 - mid.astype(F32)).astype(BF16)
    return hi, mid, lo


def _cumsum_rows(tril, a):
    hi, mid, lo = _split3(a)
    out = jnp.dot(tril, hi, preferred_element_type=F32)
    out += jnp.dot(tril, mid, preferred_element_type=F32)
    out += jnp.dot(tril, lo, preferred_element_type=F32)
    return out


def _gla_proj_kernel(x_ref, g_ref, w_ref, wg1_ref, wg2_ref, bg_ref, o_ref, la_ref, h_ref):
    @pl.when(pl.program_id(1) == 0)
    def _():
        _norm_to_scratch(x_ref, g_ref, h_ref)
        h = h_ref[...]
        g1 = jnp.dot(h, wg1_ref[...], preferred_element_type=F32)
        z = jnp.dot(g1.astype(BF16), wg2_ref[...], preferred_element_type=F32) + bg_ref[...]
        la_ref[...] = _log_sigmoid(z) * (1.0 / GLA_GATE_TEMP)

    o_ref[...] = jnp.dot(h_ref[...], w_ref[...], preferred_element_type=F32).astype(o_ref.dtype)


def _gla_proj(x2, g, w_main, wg1, wg2, bg, tm, tn):
    m = x2.shape[0]
    return pl.pallas_call(
        _gla_proj_kernel,
        grid=(m // tm, GLA_MAIN // tn),
        in_specs=[
            pl.BlockSpec((tm, D_MODEL), lambda i, j: (i, 0)),
            pl.BlockSpec((1, D_MODEL), lambda i, j: (0, 0)),
            pl.BlockSpec((D_MODEL, tn), lambda i, j: (0, j)),
            pl.BlockSpec((D_MODEL, LANES), lambda i, j: (0, 0)),
            pl.BlockSpec((LANES, GLA_KEY_DIM), lambda i, j: (0, 0)),
            pl.BlockSpec((1, GLA_KEY_DIM), lambda i, j: (0, 0)),
        ],
        out_specs=[
            pl.BlockSpec((tm, tn), lambda i, j: (i, j)),
            pl.BlockSpec((tm, GLA_KEY_DIM), lambda i, j: (i, 0)),
        ],
        out_shape=[
            jax.ShapeDtypeStruct((m, GLA_MAIN), BF16),
            jax.ShapeDtypeStruct((m, GLA_KEY_DIM), F32),
        ],
        scratch_shapes=[pltpu.VMEM((tm, D_MODEL), BF16)],
        compiler_params=_params("parallel", "arbitrary"),
        name="gla_proj",
    )(x2, g, w_main, wg1, wg2, bg)


def _gla_kernel(q_ref, k_ref, v_ref, r_ref, la_ref, gain_ref, o_ref, st_ref, *, n_chunks):
    c_len = GLA_CHUNK

    @pl.when(pl.program_id(2) == 0)
    def _():
        st_ref[...] = jnp.zeros_like(st_ref)

    row = lax.broadcasted_iota(jnp.int32, (c_len, c_len), 0)
    col = lax.broadcasted_iota(jnp.int32, (c_len, c_len), 1)
    tril = (col <= row).astype(BF16)
    delta = row - col
    band = jnp.where((delta >= 0) & (delta <= row % GLA_DIAG), delta, -1)
    level_masks = []
    block = 2 * GLA_DIAG
    while block <= c_len:
        half = block // 2
        mk = (row // block == col // block) & (row % block >= half) & (col % block < half)
        level_masks.append((block, mk.astype(F32)))
        block *= 2
    gain = gain_ref[...]
    scale = GLA_HEAD_K ** -0.5

    def chunk(c, carry):
        sl = pl.ds(pl.multiple_of(c * c_len, c_len), c_len)
        la = la_ref[0, sl, :]
        b = _cumsum_rows(tril, la)
        b_last = b[c_len - 1:c_len, :]
        qf = q_ref[0, sl, :].astype(F32) * scale
        kf = k_ref[0, sl, :].astype(F32)
        v = v_ref[0, sl, :]

        st = st_ref[...]
        qi = (qf * jnp.exp(b)).astype(BF16)
        o = lax.dot_general(qi, st.astype(BF16), NT_DIMS, preferred_element_type=F32)
        kd = (kf * jnp.exp(b_last - b)).astype(BF16)
        st_ref[...] = st * jnp.exp(b_last) + lax.dot_general(
            v, kd, TN_DIMS, preferred_element_type=F32)

        attn = jnp.zeros((c_len, c_len), F32)
        for blk, mk in level_masks:
            half = blk // 2
            ref = jnp.concatenate(
                [jnp.broadcast_to(b[s + half - 1:s + half, :], (blk, GLA_HEAD_K))
                 for s in range(0, c_len, blk)], axis=0)
            ql = (qf * jnp.exp(jnp.minimum(b - ref, 0.0))).astype(BF16)
            kl = (kf * jnp.exp(jnp.minimum(ref - b, 0.0))).astype(BF16)
            attn += mk * lax.dot_general(ql, kl, NT_DIMS, preferred_element_type=F32)
        for d in range(GLA_DIAG):
            k_d = pltpu.roll(kf, d, 0) if d else kf
            b_d = pltpu.roll(b, d, 0) if d else b
            e = jnp.exp(jnp.minimum(b - b_d, 0.0))
            diag = jnp.sum(qf * k_d * e, axis=-1, keepdims=True)
            attn += jnp.where(band == d, diag, 0.0)

        o += jnp.dot(attn.astype(BF16), v, preferred_element_type=F32)
        on = _rmsnorm_rows(o, gain)
        r = r_ref[0, sl, :].astype(F32)
        o_ref[0, sl, :] = (on * (r * jax.nn.sigmoid(r))).astype(o_ref.dtype)
        return carry

    lax.fori_loop(0, n_chunks, chunk, 0)


def _gla_mix(qkvr, la, gain, b_sz, s_len, t_len):
    n_chunks = t_len // GLA_CHUNK
    kq = GLA_KEY_DIM // GLA_HEAD_K
    kv = 2 * GLA_KEY_DIM // GLA_HEAD_V
    kr = kv + GLA_HEADS
    return pl.pallas_call(
        functools.partial(_gla_kernel, n_chunks=n_chunks),
        grid=(b_sz, GLA_HEADS, s_len // t_len),
        in_specs=[
            pl.BlockSpec((1, t_len, GLA_HEAD_K), lambda b, h, t: (b, t, h)),
            pl.BlockSpec((1, t_len, GLA_HEAD_K), lambda b, h, t: (b, t, kq + h)),
            pl.BlockSpec((1, t_len, GLA_HEAD_V), lambda b, h, t: (b, t, kv + h)),
            pl.BlockSpec((1, t_len, GLA_HEAD_V), lambda b, h, t: (b, t, kr + h)),
            pl.BlockSpec((1, t_len, GLA_HEAD_K), lambda b, h, t: (b, t, h)),
            pl.BlockSpec((1, GLA_HEAD_V), lambda b, h, t: (0, 0)),
        ],
        out_specs=pl.BlockSpec((1, t_len, GLA_HEAD_V), lambda b, h, t: (b, t, h)),
        out_shape=jax.ShapeDtypeStruct((b_sz, s_len, GLA_VAL_DIM), BF16),
        scratch_shapes=[pltpu.VMEM((GLA_HEAD_V, GLA_HEAD_K), F32)],
        compiler_params=_params("parallel", "parallel", "arbitrary"),
        name="gla_mix",
    )(qkvr, qkvr, qkvr, qkvr, la, gain)


def _residual_matmul_kernel(x_ref, a_ref, w_ref, o_ref):
    o_ref[...] = x_ref[...] + jnp.dot(a_ref[...], w_ref[...], preferred_element_type=F32)


def _residual_matmul(x2, a, w, tm, tn):
    m, k_dim = a.shape
    n = w.shape[1]
    return pl.pallas_call(
        _residual_matmul_kernel,
        grid=(m // tm, n // tn),
        in_specs=[
            pl.BlockSpec((tm, tn), lambda i, j: (i, j)),
            pl.BlockSpec((tm, k_dim), lambda i, j: (i, 0)),
            pl.BlockSpec((k_dim, tn), lambda i, j: (0, j)),
        ],
        out_specs=pl.BlockSpec((tm, tn), lambda i, j: (i, j)),
        out_shape=jax.ShapeDtypeStruct((m, n), F32),
        compiler_params=_params("parallel", "arbitrary"),
        name="residual_matmul",
    )(x2, a, w)


def _ffn_kernel(x_ref, g_ref, wg_ref, wu_ref, wd_ref, o_ref, h_ref):
    @pl.when(pl.program_id(1) == 0)
    def _():
        _norm_to_scratch(x_ref, g_ref, h_ref)
        o_ref[...] = x_ref[...]

    h = h_ref[...]
    gate = jnp.dot(h, wg_ref[...], preferred_element_type=F32)
    up = jnp.dot(h, wu_ref[...], preferred_element_type=F32)
    act = (gate * jax.nn.sigmoid(gate) * up).astype(BF16)
    o_ref[...] += jnp.dot(act, wd_ref[...], preferred_element_type=F32)


def _ffn(x2, g, wg, wu, wd, tm, tf):
    m = x2.shape[0]
    return pl.pallas_call(
        _ffn_kernel,
        grid=(m // tm, D_FF // tf),
        in_specs=[
            pl.BlockSpec((tm, D_MODEL), lambda i, f: (i, 0)),
            pl.BlockSpec((1, D_MODEL), lambda i, f: (0, 0)),
            pl.BlockSpec((D_MODEL, tf), lambda i, f: (0, f)),
            pl.BlockSpec((D_MODEL, tf), lambda i, f: (0, f)),
            pl.BlockSpec((tf, D_MODEL), lambda i, f: (f, 0)),
        ],
        out_specs=pl.BlockSpec((tm, D_MODEL), lambda i, f: (i, 0)),
        out_shape=jax.ShapeDtypeStruct((m, D_MODEL), F32),
        scratch_shapes=[pltpu.VMEM((tm, D_MODEL), BF16)],
        compiler_params=_params("parallel", "arbitrary"),
        name="ffn",
    )(x2, g, wg, wu, wd)


def _fox_proj_kernel(x_ref, g_ref, w_ref, wf_ref, bf_ref, qkg_ref, o_ref, lf_ref, h_ref,
                     *, q_tiles, qk_tiles):
    j = pl.program_id(1)

    @pl.when(j == 0)
    def _():
        _norm_to_scratch(x_ref, g_ref, h_ref)
        h = h_ref[...]
        z = jnp.dot(h, wf_ref[...], preferred_element_type=F32) + bf_ref[...]
        lf_ref[...] = _log_sigmoid(z)

    acc = jnp.dot(h_ref[...], w_ref[...], preferred_element_type=F32)

    @pl.when(j < qk_tiles)
    def _():
        mult = jnp.where(j < q_tiles, FOX_SCALE, 1.0).astype(F32)
        for c in range(acc.shape[1] // FOX_HEAD_DIM):
            cs = slice(c * FOX_HEAD_DIM, (c + 1) * FOX_HEAD_DIM)
            xs = acc[:, cs]
            ms = jnp.mean(xs * xs, axis=-1, keepdims=True)
            o_ref[:, cs] = (xs * (lax.rsqrt(ms + RMS_EPS) * mult) * qkg_ref[:, cs]).astype(o_ref.dtype)

    @pl.when(j >= qk_tiles)
    def _():
        o_ref[...] = acc.astype(o_ref.dtype)


def _fox_proj(x2, g, w_main, wf, bf, qk_gain, tm, tn):
    m = x2.shape[0]
    q_tiles = D_MODEL // tn
    qk_tiles = 2 * q_tiles
    return pl.pallas_call(
        functools.partial(_fox_proj_kernel, q_tiles=q_tiles, qk_tiles=qk_tiles),
        grid=(m // tm, FOX_MAIN // tn),
        in_specs=[
            pl.BlockSpec((tm, D_MODEL), lambda i, j: (i, 0)),
            pl.BlockSpec((1, D_MODEL), lambda i, j: (0, 0)),
            pl.BlockSpec((D_MODEL, tn), lambda i, j: (0, j)),
            pl.BlockSpec((D_MODEL, LANES), lambda i, j: (0, 0)),
            pl.BlockSpec((1, LANES), lambda i, j: (0, 0)),
            pl.BlockSpec((1, tn), lambda i, j: (0, jnp.minimum(j, qk_tiles - 1))),
        ],
        out_specs=[
            pl.BlockSpec((tm, tn), lambda i, j: (i, j)),
            pl.BlockSpec((tm, LANES), lambda i, j: (i, 0)),
        ],
        out_shape=[
            jax.ShapeDtypeStruct((m, FOX_MAIN), BF16),
            jax.ShapeDtypeStruct((m, LANES), F32),
        ],
        scratch_shapes=[pltpu.VMEM((tm, D_MODEL), BF16)],
        compiler_params=_params("parallel", "arbitrary"),
        name="fox_proj",
    )(x2, g, w_main, wf, bf, qk_gain)


def _cumsum_kernel(lf_ref, c_ref, carry_ref):
    @pl.when(pl.program_id(1) == 0)
    def _():
        carry_ref[...] = jnp.zeros_like(carry_ref)

    t_len = lf_ref.shape[1]
    row = lax.broadcasted_iota(jnp.int32, (t_len, t_len), 0)
    col = lax.broadcasted_iota(jnp.int32, (t_len, t_len), 1)
    tril = (col <= row).astype(BF16)
    c = _cumsum_rows(tril, lf_ref[0]) + carry_ref[...]
    c_ref[0] = c
    carry_ref[...] = c[t_len - 1:t_len, :]


def _seq_cumsum(lf, t_len):
    b_sz, s_len, _ = lf.shape
    return pl.pallas_call(
        _cumsum_kernel,
        grid=(b_sz, s_len // t_len),
        in_specs=[pl.BlockSpec((1, t_len, LANES), lambda b, t: (b, t, 0))],
        out_specs=pl.BlockSpec((1, t_len, LANES), lambda b, t: (b, t, 0)),
        out_shape=jax.ShapeDtypeStruct(lf.shape, F32),
        scratch_shapes=[pltpu.VMEM((1, LANES), F32)],
        compiler_params=_params("parallel", "arbitrary"),
        name="fox_cumsum",
    )(lf)


def _fox_attn_kernel(q_ref, k_ref, v_ref, og_ref, cq_ref, ck_ref, o_ref, m_ref, l_ref, acc_ref,
                     *, blk):
    i = pl.program_id(2)
    q = q_ref[0]
    cq = cq_ref[0, 0]
    m_ref[...] = jnp.full_like(m_ref, -jnp.inf)
    l_ref[...] = jnp.zeros_like(l_ref)
    acc_ref[...] = jnp.zeros_like(acc_ref)

    def step(j, masked):
        sl = pl.ds(pl.multiple_of(j * blk, blk), blk)
        ks = k_ref[0, sl, :]
        vs = v_ref[0, sl, :]
        ck = ck_ref[0, 0, pl.ds(j, 1), :]
        s = lax.dot_general(q, ks, NT_DIMS, preferred_element_type=F32) + (cq - ck)
        if masked:
            row = lax.broadcasted_iota(jnp.int32, (blk, blk), 0)
            col = lax.broadcasted_iota(jnp.int32, (blk, blk), 1)
            s = jnp.where(col <= row, s, -jnp.inf)
        m_old = m_ref[...]
        m_new = jnp.maximum(m_old, jnp.max(s, axis=-1, keepdims=True))
        p = jnp.exp(s - m_new)
        alpha = jnp.exp(m_old - m_new)
        l_ref[...] = alpha * l_ref[...] + jnp.sum(p, axis=-1, keepdims=True)
        acc_ref[...] = alpha * acc_ref[...] + jnp.dot(p.astype(BF16), vs, preferred_element_type=F32)
        m_ref[...] = m_new

    step(i, True)

    def body(j, carry):
        step(j, False)
        return carry

    lax.fori_loop(0, i, body, 0)
    gate = jax.nn.sigmoid(og_ref[0].astype(F32))
    o_ref[0] = (acc_ref[...] / l_ref[...] * gate).astype(o_ref.dtype)


def _fox_attn(qkvo, cq, ck, b_sz, s_len, blk):
    h_cols = D_MODEL // FOX_HEAD_DIM
    n_blk = s_len // blk
    return pl.pallas_call(
        functools.partial(_fox_attn_kernel, blk=blk),
        grid=(b_sz, FOX_HEADS, n_blk),
        in_specs=[
            pl.BlockSpec((1, blk, FOX_HEAD_DIM), lambda b, h, i: (b, i, h)),
            pl.BlockSpec((1, s_len, FOX_HEAD_DIM), lambda b, h, i: (b, 0, h_cols + h)),
            pl.BlockSpec((1, s_len, FOX_HEAD_DIM), lambda b, h, i: (b, 0, 2 * h_cols + h)),
            pl.BlockSpec((1, blk, FOX_HEAD_DIM), lambda b, h, i: (b, i, 3 * h_cols + h)),
            pl.BlockSpec((1, 1, blk, 1), lambda b, h, i: (b, h, i, 0)),
            pl.BlockSpec((1, 1, n_blk, blk), lambda b, h, i: (b, h, 0, 0)),
        ],
        out_specs=pl.BlockSpec((1, blk, FOX_HEAD_DIM), lambda b, h, i: (b, i, h)),
        out_shape=jax.ShapeDtypeStruct((b_sz, s_len, D_MODEL), BF16),
        scratch_shapes=[
            pltpu.VMEM((blk, 1), F32),
            pltpu.VMEM((blk, 1), F32),
            pltpu.VMEM((blk, FOX_HEAD_DIM), F32),
        ],
        compiler_params=_params("parallel", "parallel", "arbitrary"),
        name="fox_attn",
    )(qkvo, qkvo, qkvo, qkvo, cq, ck)


def _pad_cols(w, n):
    return jnp.pad(w, ((0, 0), (0, n - w.shape[1])))


def kernel(x, norm_mix, norm_ffn, gla_w_in, gla_w_g2, gla_b_g2, gla_o_gain, gla_w_o,
           fox_w_in, fox_b_f, fox_q_gain, fox_k_gain, fox_w_o,
           ffn_w_gate, ffn_w_up, ffn_w_down):
    b_sz, s_len, d = x.shape
    assert d == D_MODEL and s_len % GLA_CHUNK == 0
    m = b_sz * s_len
    tm = min(1024, m)
    tn = 1024
    tm_ffn = min(512, m)
    tf = 512
    gla_t = min(512, s_len)
    attn_blk = min(512, s_len)
    cum_t = min(256, s_len)

    x2 = x.reshape(m, d)

    w_in = gla_w_in[0]
    qkvr, la = _gla_proj(
        x2, norm_mix[0][None, :],
        w_in[:, :GLA_MAIN].astype(BF16),
        _pad_cols(w_in[:, GLA_MAIN:], LANES).astype(BF16),
        jnp.pad(gla_w_g2[0], ((0, LANES - GLA_GATE_RANK), (0, 0))).astype(BF16),
        gla_b_g2[0][None, :], tm, tn)
    og = _gla_mix(qkvr.reshape(b_sz, s_len, GLA_MAIN), la.reshape(b_sz, s_len, GLA_KEY_DIM),
                  gla_o_gain[0][None, :], b_sz, s_len, gla_t)
    x2 = _residual_matmul(x2, og.reshape(m, GLA_VAL_DIM), gla_w_o[0].astype(BF16), tm, tn)
    x2 = _ffn(x2, norm_ffn[0][None, :], ffn_w_gate[0].astype(BF16), ffn_w_up[0].astype(BF16),
              ffn_w_down[0].astype(BF16), tm_ffn, tf)

    w_in = fox_w_in[0]
    qk_gain = jnp.concatenate([jnp.tile(fox_q_gain[0], FOX_HEADS), jnp.tile(fox_k_gain[0], FOX_HEADS)])
    f_lo, f_hi = 3 * D_MODEL, 3 * D_MODEL + FOX_HEADS
    qkvo, lf = _fox_proj(
        x2, norm_mix[1][None, :],
        jnp.concatenate([w_in[:, :f_lo], w_in[:, f_hi:]], axis=1).astype(BF16),
        _pad_cols(w_in[:, f_lo:f_hi], LANES).astype(BF16),
        jnp.pad(fox_b_f[0], (0, LANES - FOX_HEADS))[None, :],
        qk_gain[None, :], tm, tn)
    c = _seq_cumsum(lf.reshape(b_sz, s_len, LANES), cum_t)
    c_hs = c[:, :, :FOX_HEADS].transpose(0, 2, 1)
    cq = c_hs[..., None]
    ck = c_hs.reshape(b_sz, FOX_HEADS, s_len // attn_blk, attn_blk)
    o = _fox_attn(qkvo.reshape(b_sz, s_len, FOX_MAIN), cq, ck, b_sz, s_len, attn_blk)
    x2 = _residual_matmul(x2, o.reshape(m, D_MODEL), fox_w_o[0].astype(BF16), tm, tn)
    x2 = _ffn(x2, norm_ffn[1][None, :], ffn_w_gate[1].astype(BF16), ffn_w_up[1].astype(BF16),
              ffn_w_down[1].astype(BF16), tm_ffn, tf)
    return x2.reshape(b_sz, s_len, d)
```

```python
import functools

import jax
import jax.numpy as jnp
from jax import lax
from jax.experimental import pallas as pl
from jax.experimental.pallas import tpu as pltpu

F32 = jnp.float32
BF16 = jnp.bfloat16

D_MODEL = 2048
RMS_EPS = 1e-6

GLA_HEADS = 4
GLA_KEY_DIM = D_MODEL // 2
GLA_VAL_DIM = D_MODEL
GLA_HEAD_K = GLA_KEY_DIM // GLA_HEADS
GLA_HEAD_V = GLA_VAL_DIM // GLA_HEADS
GLA_GATE_RANK = 16
GLA_GATE_TEMP = 16.0
GLA_CHUNK = 64
GLA_MAIN = 2 * GLA_KEY_DIM + 2 * GLA_VAL_DIM
GLA_DIAG = 8

FOX_HEAD_DIM = 128
FOX_HEADS = D_MODEL // FOX_HEAD_DIM
FOX_MAIN = 4 * D_MODEL
FOX_SCALE = FOX_HEAD_DIM ** -0.5
LOG2E = 1.4426950408889634

D_FF = ((8 * D_MODEL + 2) // 3 + 255) // 256 * 256

LANES = 128
NORM_SLAB = 256
CAST_ROWS = 256
VMEM_LIMIT = 56 * 1024 * 1024

NT_DIMS = (((1,), (1,)), ((), ()))
TN_DIMS = (((0,), (0,)), ((), ()))


def _params(*sem):
    return pltpu.CompilerParams(dimension_semantics=sem, vmem_limit_bytes=VMEM_LIMIT)


def _rmsnorm_rows(x, g):
    ms = jnp.mean(x * x, axis=-1, keepdims=True)
    return x * lax.rsqrt(ms + RMS_EPS) * g


def _log_sigmoid(z):
    return jnp.minimum(z, 0.0) - jnp.log1p(jnp.exp(-jnp.abs(z)))


def _norm_to_scratch(x_ref, g_ref, h_ref):
    rows = x_ref.shape[0]
    slab = min(NORM_SLAB, rows)
    g = g_ref[...]
    for r0 in range(0, rows, slab):
        h_ref[r0:r0 + slab, :] = _rmsnorm_rows(x_ref[r0:r0 + slab, :], g).astype(BF16)


def _split3(a):
    hi = a.astype(BF16)
    r1 = a - hi.astype(F32)
    mid = r1.astype(BF16)
    lo = (r1 - mid.astype(F32)).astype(BF16)
    return hi, mid, lo


def _cumsum_rows(tril, a):
    hi, mid, lo = _split3(a)
    out = jnp.dot(tril, hi, preferred_element_type=F32)
    out += jnp.dot(tril, mid, preferred_element_type=F32)
    out += jnp.dot(tril, lo, preferred_element_type=F32)
    return out


def _cast_kernel(w_ref, o_ref):
    o_ref[...] = w_ref[0].astype(BF16)


def _cast_weight(w3, layer, n_cols):
    _, k_dim, _ = w3.shape
    tr = CAST_ROWS
    assert k_dim % tr == 0 and n_cols % LANES == 0
    return pl.pallas_call(
        _cast_kernel,
        grid=(k_dim // tr,),
        in_specs=[pl.BlockSpec((1, tr, n_cols), lambda i: (layer, i, 0))],
        out_specs=pl.BlockSpec((tr, n_cols), lambda i: (i, 0)),
        out_shape=jax.ShapeDtypeStruct((k_dim, n_cols), BF16),
        compiler_params=_params("parallel"),
        name="cast_weight",
    )(w3)


def _gla_proj_kernel(x_ref, g_ref, w_ref, wg1_ref, wg2_ref, bg_ref, o_ref, la_ref, h_ref):
    @pl.when(pl.program_id(1) == 0)
    def _():
        _norm_to_scratch(x_ref, g_ref, h_ref)
        h = h_ref[...]
        g1 = jnp.dot(h, wg1_ref[...], preferred_element_type=F32)
        z = jnp.dot(g1.astype(BF16), wg2_ref[...], preferred_element_type=F32) + bg_ref[...]
        la_ref[...] = _log_sigmoid(z) * (LOG2E / GLA_GATE_TEMP)

    o_ref[...] = jnp.dot(h_ref[...], w_ref[...], preferred_element_type=F32).astype(o_ref.dtype)


def _gla_proj(x2, g, w_main, wg1, wg2, bg, tm, tn):
    m = x2.shape[0]
    return pl.pallas_call(
        _gla_proj_kernel,
        grid=(m // tm, GLA_MAIN // tn),
        in_specs=[
            pl.BlockSpec((tm, D_MODEL), lambda i, j: (i, 0)),
            pl.BlockSpec((1, D_MODEL), lambda i, j: (0, 0)),
            pl.BlockSpec((D_MODEL, tn), lambda i, j: (0, j)),
            pl.BlockSpec((D_MODEL, LANES), lambda i, j: (0, 0)),
            pl.BlockSpec((LANES, GLA_KEY_DIM), lambda i, j: (0, 0)),
            pl.BlockSpec((1, GLA_KEY_DIM), lambda i, j: (0, 0)),
        ],
        out_specs=[
            pl.BlockSpec((tm, tn), lambda i, j: (i, j)),
            pl.BlockSpec((tm, GLA_KEY_DIM), lambda i, j: (i, 0)),
        ],
        out_shape=[
            jax.ShapeDtypeStruct((m, GLA_MAIN), BF16),
            jax.ShapeDtypeStruct((m, GLA_KEY_DIM), F32),
        ],
        scratch_shapes=[pltpu.VMEM((tm, D_MODEL), BF16)],
        compiler_params=_params("parallel", "arbitrary"),
        name="gla_proj",
    )(x2, g, w_main, wg1, wg2, bg)


def _gla_kernel(q_ref, k_ref, v_ref, r_ref, la_ref, gain_ref, o_ref, st_ref, *, n_chunks, heads):
    c_len = GLA_CHUNK

    @pl.when(pl.program_id(2) == 0)
    def _():
        st_ref[...] = jnp.zeros_like(st_ref)

    row = lax.broadcasted_iota(jnp.int32, (c_len, c_len), 0)
    col = lax.broadcasted_iota(jnp.int32, (c_len, c_len), 1)
    tril = (col <= row).astype(BF16)
    delta = row - col
    band = jnp.where((delta >= 0) & (delta <= row % GLA_DIAG), delta, -1)
    level_masks = []
    block = 2 * GLA_DIAG
    while block <= c_len:
        half = block // 2
        mk = (row // block == col // block) & (row % block >= half) & (col % block < half)
        level_masks.append((block, mk))
        block *= 2
    gain = gain_ref[...]
    scale = GLA_HEAD_K ** -0.5

    def head_chunk(hd, sl):
        ksl = slice(hd * GLA_HEAD_K, (hd + 1) * GLA_HEAD_K)
        vsl = slice(hd * GLA_HEAD_V, (hd + 1) * GLA_HEAD_V)
        la = la_ref[0, sl, ksl]
        b = _cumsum_rows(tril, la)
        b_last = b[c_len - 1:c_len, :]
        qf = q_ref[0, sl, ksl].astype(F32) * scale
        kf = k_ref[0, sl, ksl].astype(F32)
        v = v_ref[0, sl, vsl]

        st = st_ref[hd]
        qi = (qf * jnp.exp2(b)).astype(BF16)
        o = lax.dot_general(qi, st.astype(BF16), NT_DIMS, preferred_element_type=F32)
        kd = (kf * jnp.exp2(b_last - b)).astype(BF16)
        st_ref[hd] = st * jnp.exp2(b_last) + lax.dot_general(
            v, kd, TN_DIMS, preferred_element_type=F32)

        attn = jnp.zeros((c_len, c_len), F32)
        for blk, mk in level_masks:
            half = blk // 2
            ref = jnp.concatenate(
                [jnp.broadcast_to(b[s + half - 1:s + half, :], (blk, GLA_HEAD_K))
                 for s in range(0, c_len, blk)], axis=0)
            ql = (qf * jnp.exp2(b - ref)).astype(BF16)
            kl = (kf * jnp.exp2(ref - b)).astype(BF16)
            attn = jnp.where(mk, lax.dot_general(ql, kl, NT_DIMS, preferred_element_type=F32), attn)
        for d in range(GLA_DIAG):
            k_d = pltpu.roll(kf, d, 0) if d else kf
            b_d = pltpu.roll(b, d, 0) if d else b
            diag = jnp.sum(qf * k_d * jnp.exp2(b - b_d), axis=-1, keepdims=True)
            attn = jnp.where(band == d, diag, attn)

        o += jnp.dot(attn.astype(BF16), v, preferred_element_type=F32)
        on = _rmsnorm_rows(o, gain)
        r = r_ref[0, sl, vsl].astype(F32)
        o_ref[0, sl, vsl] = (on * (r * jax.nn.sigmoid(r))).astype(o_ref.dtype)

    def chunk(c, carry):
        sl = pl.ds(pl.multiple_of(c * c_len, c_len), c_len)
        for hd in range(heads):
            head_chunk(hd, sl)
        return carry

    lax.fori_loop(0, n_chunks, chunk, 0)


def _gla_mix(qkvr, la, gain, b_sz, s_len, t_len, heads):
    n_chunks = t_len // GLA_CHUNK
    wk = heads * GLA_HEAD_K
    wv = heads * GLA_HEAD_V
    kq = GLA_KEY_DIM // wk
    kv = 2 * GLA_KEY_DIM // wv
    kr = kv + GLA_VAL_DIM // wv
    return pl.pallas_call(
        functools.partial(_gla_kernel, n_chunks=n_chunks, heads=heads),
        grid=(b_sz, GLA_HEADS // heads, s_len // t_len),
        in_specs=[
            pl.BlockSpec((1, t_len, wk), lambda b, h, t: (b, t, h)),
            pl.BlockSpec((1, t_len, wk), lambda b, h, t: (b, t, kq + h)),
            pl.BlockSpec((1, t_len, wv), lambda b, h, t: (b, t, kv + h)),
            pl.BlockSpec((1, t_len, wv), lambda b, h, t: (b, t, kr + h)),
            pl.BlockSpec((1, t_len, wk), lambda b, h, t: (b, t, h)),
            pl.BlockSpec((1, GLA_HEAD_V), lambda b, h, t: (0, 0)),
        ],
        out_specs=pl.BlockSpec((1, t_len, wv), lambda b, h, t: (b, t, h)),
        out_shape=jax.ShapeDtypeStruct((b_sz, s_len, GLA_VAL_DIM), BF16),
        scratch_shapes=[pltpu.VMEM((heads, GLA_HEAD_V, GLA_HEAD_K), F32)],
        compiler_params=_params("parallel", "parallel", "arbitrary"),
        name="gla_mix",
    )(qkvr, qkvr, qkvr, qkvr, la, gain)


def _residual_matmul_kernel(x_ref, a_ref, w_ref, o_ref):
    o_ref[...] = x_ref[...] + jnp.dot(a_ref[...], w_ref[...], preferred_element_type=F32)


def _residual_matmul(x2, a, w, tm, tn):
    m, k_dim = a.shape
    n = w.shape[1]
    return pl.pallas_call(
        _residual_matmul_kernel,
        grid=(m // tm, n // tn),
        in_specs=[
            pl.BlockSpec((tm, tn), lambda i, j: (i, j)),
            pl.BlockSpec((tm, k_dim), lambda i, j: (i, 0)),
            pl.BlockSpec((k_dim, tn), lambda i, j: (0, j)),
        ],
        out_specs=pl.BlockSpec((tm, tn), lambda i, j: (i, j)),
        out_shape=jax.ShapeDtypeStruct((m, n), F32),
        compiler_params=_params("parallel", "arbitrary"),
        name="residual_matmul",
    )(x2, a, w)


def _ffn_kernel(x_ref, g_ref, wg_ref, wu_ref, wd_ref, o_ref, h_ref):
    @pl.when(pl.program_id(1) == 0)
    def _():
        _norm_to_scratch(x_ref, g_ref, h_ref)
        o_ref[...] = x_ref[...]

    h = h_ref[...]
    gate = jnp.dot(h, wg_ref[...], preferred_element_type=F32)
    up = jnp.dot(h, wu_ref[...], preferred_element_type=F32)
    act = (gate * jax.nn.sigmoid(gate) * up).astype(BF16)
    o_ref[...] += jnp.dot(act, wd_ref[...], preferred_element_type=F32)


def _ffn(x2, g, wg, wu, wd, tm, tf):
    m = x2.shape[0]
    return pl.pallas_call(
        _ffn_kernel,
        grid=(m // tm, D_FF // tf),
        in_specs=[
            pl.BlockSpec((tm, D_MODEL), lambda i, f: (i, 0)),
            pl.BlockSpec((1, D_MODEL), lambda i, f: (0, 0)),
            pl.BlockSpec((D_MODEL, tf), lambda i, f: (0, f)),
            pl.BlockSpec((D_MODEL, tf), lambda i, f: (0, f)),
            pl.BlockSpec((tf, D_MODEL), lambda i, f: (f, 0)),
        ],
        out_specs=pl.BlockSpec((tm, D_MODEL), lambda i, f: (i, 0)),
        out_shape=jax.ShapeDtypeStruct((m, D_MODEL), F32),
        scratch_shapes=[pltpu.VMEM((tm, D_MODEL), BF16)],
        compiler_params=_params("parallel", "arbitrary"),
        name="ffn",
    )(x2, g, wg, wu, wd)


def _fox_proj_kernel(x_ref, g_ref, wa_ref, wb_ref, wf_ref, bf_ref, qkg_ref, o_ref, lf_ref, h_ref,
                     *, q_tiles, qk_tiles, a_tiles):
    j = pl.program_id(1)

    @pl.when(j == 0)
    def _():
        _norm_to_scratch(x_ref, g_ref, h_ref)
        h = h_ref[...]
        z = jnp.dot(h, wf_ref[...], preferred_element_type=F32) + bf_ref[...]
        lf_ref[...] = _log_sigmoid(z)

    @pl.when(j < qk_tiles)
    def _():
        acc = jnp.dot(h_ref[...], wa_ref[...], preferred_element_type=F32)
        mult = jnp.where(j < q_tiles, FOX_SCALE * LOG2E, 1.0).astype(F32)
        for c in range(acc.shape[1] // FOX_HEAD_DIM):
            cs = slice(c * FOX_HEAD_DIM, (c + 1) * FOX_HEAD_DIM)
            xs = acc[:, cs]
            ms = jnp.mean(xs * xs, axis=-1, keepdims=True)
            o_ref[:, cs] = (xs * (lax.rsqrt(ms + RMS_EPS) * mult) * qkg_ref[:, cs]).astype(o_ref.dtype)

    @pl.when((j >= qk_tiles) & (j < a_tiles))
    def _():
        o_ref[...] = jnp.dot(h_ref[...], wa_ref[...], preferred_element_type=F32).astype(o_ref.dtype)

    @pl.when(j >= a_tiles)
    def _():
        o_ref[...] = jnp.dot(h_ref[...], wb_ref[...], preferred_element_type=F32).astype(o_ref.dtype)


def _fox_proj(x2, g, w_qkv, w_og, wf, bf, qk_gain, tm, tn):
    m = x2.shape[0]
    q_tiles = D_MODEL // tn
    qk_tiles = 2 * q_tiles
    a_tiles = w_qkv.shape[1] // tn
    return pl.pallas_call(
        functools.partial(_fox_proj_kernel, q_tiles=q_tiles, qk_tiles=qk_tiles, a_tiles=a_tiles),
        grid=(m // tm, FOX_MAIN // tn),
        in_specs=[
            pl.BlockSpec((tm, D_MODEL), lambda i, j: (i, 0)),
            pl.BlockSpec((1, D_MODEL), lambda i, j: (0, 0)),
            pl.BlockSpec((D_MODEL, tn), lambda i, j: (0, jnp.minimum(j, a_tiles - 1))),
            pl.BlockSpec((D_MODEL, tn), lambda i, j: (0, jnp.maximum(j - a_tiles, 0))),
            pl.BlockSpec((D_MODEL, LANES), lambda i, j: (0, 0)),
            pl.BlockSpec((1, LANES), lambda i, j: (0, 0)),
            pl.BlockSpec((1, tn), lambda i, j: (0, jnp.minimum(j, qk_tiles - 1))),
        ],
        out_specs=[
            pl.BlockSpec((tm, tn), lambda i, j: (i, j)),
            pl.BlockSpec((tm, LANES), lambda i, j: (i, 0)),
        ],
        out_shape=[
            jax.ShapeDtypeStruct((m, FOX_MAIN), BF16),
            jax.ShapeDtypeStruct((m, LANES), F32),
        ],
        scratch_shapes=[pltpu.VMEM((tm, D_MODEL), BF16)],
        compiler_params=_params("parallel", "arbitrary"),
        name="fox_proj",
    )(x2, g, w_qkv, w_og, wf, bf, qk_gain)


def _cumsum_kernel(lf_ref, c_ref, carry_ref):
    @pl.when(pl.program_id(1) == 0)
    def _():
        carry_ref[...] = jnp.zeros_like(carry_ref)

    t_len = lf_ref.shape[1]
    row = lax.broadcasted_iota(jnp.int32, (t_len, t_len), 0)
    col = lax.broadcasted_iota(jnp.int32, (t_len, t_len), 1)
    tril = (col <= row).astype(BF16)
    c = _cumsum_rows(tril, lf_ref[0]) + carry_ref[...]
    c_ref[0] = c * LOG2E
    carry_ref[...] = c[t_len - 1:t_len, :]


def _seq_cumsum(lf, t_len):
    b_sz, s_len, _ = lf.shape
    return pl.pallas_call(
        _cumsum_kernel,
        grid=(b_sz, s_len // t_len),
        in_specs=[pl.BlockSpec((1, t_len, LANES), lambda b, t: (b, t, 0))],
        out_specs=pl.BlockSpec((1, t_len, LANES), lambda b, t: (b, t, 0)),
        out_shape=jax.ShapeDtypeStruct(lf.shape, F32),
        scratch_shapes=[pltpu.VMEM((1, LANES), F32)],
        compiler_params=_params("parallel", "arbitrary"),
        name="fox_cumsum",
    )(lf)


def _fox_attn_kernel(q_ref, k_ref, v_ref, og_ref, ck_ref, o_ref, acc_ref, *, blk, n_sub):
    i = pl.program_id(2)
    sub = blk // n_sub
    ck_own = ck_ref[0, 0, pl.ds(i, 1), :]
    c0 = ck_own[:, 0:1]
    acc_ref[...] = jnp.zeros_like(acc_ref)

    def update(r, carry, ks, vs, bias, mask_offset):
        m_old, l_old = carry
        rows = slice(r * sub, (r + 1) * sub)
        s = lax.dot_general(q_ref[0, rows, :], ks, NT_DIMS, preferred_element_type=F32) + bias
        if mask_offset is not None:
            row = lax.broadcasted_iota(jnp.int32, s.shape, 0)
            col = lax.broadcasted_iota(jnp.int32, s.shape, 1)
            s = jnp.where(col <= row + mask_offset, s, -jnp.inf)
        m_new = jnp.maximum(m_old, jnp.max(s, axis=-1, keepdims=True))
        p = jnp.exp2(s - m_new)
        alpha = jnp.exp2(m_old - m_new)
        l_new = alpha * l_old + jnp.sum(p, axis=-1, keepdims=True)
        acc_ref[rows, :] = alpha * acc_ref[rows, :] + jnp.dot(
            p.astype(BF16), vs, preferred_element_type=F32)
        return m_new, l_new

    def body(j, carry):
        sl = pl.ds(pl.multiple_of(j * blk, blk), blk)
        ks = k_ref[0, sl, :]
        vs = v_ref[0, sl, :]
        bias = c0 - ck_ref[0, 0, pl.ds(j, 1), :]
        return tuple(update(r, carry[r], ks, vs, bias, None) for r in range(n_sub))

    init = tuple((jnp.full((sub, 1), -jnp.inf, F32), jnp.zeros((sub, 1), F32)) for _ in range(n_sub))
    carry = lax.fori_loop(0, i, body, init)

    base = pl.multiple_of(i * blk, blk)
    for r in range(n_sub):
        width = (r + 1) * sub
        ks = k_ref[0, pl.ds(base, width), :]
        vs = v_ref[0, pl.ds(base, width), :]
        bias = c0 - ck_own[:, 0:width]
        m_fin, l_fin = update(r, carry[r], ks, vs, bias, r * sub)
        rows = slice(r * sub, (r + 1) * sub)
        gate = jax.nn.sigmoid(og_ref[0, rows, :].astype(F32))
        o_ref[0, rows, :] = (acc_ref[rows, :] / l_fin * gate).astype(o_ref.dtype)


def _fox_attn(qkvo, ck, b_sz, s_len, blk, n_sub):
    h_cols = D_MODEL // FOX_HEAD_DIM
    n_blk = s_len // blk
    return pl.pallas_call(
        functools.partial(_fox_attn_kernel, blk=blk, n_sub=n_sub),
        grid=(b_sz, FOX_HEADS, n_blk),
        in_specs=[
            pl.BlockSpec((1, blk, FOX_HEAD_DIM), lambda b, h, i: (b, i, h)),
            pl.BlockSpec((1, s_len, FOX_HEAD_DIM), lambda b, h, i: (b, 0, h_cols + h)),
            pl.BlockSpec((1, s_len, FOX_HEAD_DIM), lambda b, h, i: (b, 0, 2 * h_cols + h)),
            pl.BlockSpec((1, blk, FOX_HEAD_DIM), lambda b, h, i: (b, i, 3 * h_cols + h)),
            pl.BlockSpec((1, 1, n_blk, blk), lambda b, h, i: (b, h, 0, 0)),
        ],
        out_specs=pl.BlockSpec((1, blk, FOX_HEAD_DIM), lambda b, h, i: (b, i, h)),
        out_shape=jax.ShapeDtypeStruct((b_sz, s_len, D_MODEL), BF16),
        scratch_shapes=[pltpu.VMEM((blk, FOX_HEAD_DIM), F32)],
        compiler_params=_params("parallel", "parallel", "arbitrary"),
        name="fox_attn",
    )(qkvo, qkvo, qkvo, qkvo, ck)


def _pad_cols(w, n):
    return jnp.pad(w, ((0, 0), (0, n - w.shape[1])))


def kernel(x, norm_mix, norm_ffn, gla_w_in, gla_w_g2, gla_b_g2, gla_o_gain, gla_w_o,
           fox_w_in, fox_b_f, fox_q_gain, fox_k_gain, fox_w_o,
           ffn_w_gate, ffn_w_up, ffn_w_down):
    b_sz, s_len, d = x.shape
    assert d == D_MODEL and s_len % GLA_CHUNK == 0
    m = b_sz * s_len
    tm = min(1024, m)
    tn = 1024
    tm_ffn = min(512, m)
    tf = 512
    gla_t = min(512, s_len)
    attn_blk = min(512, s_len)
    cum_t = min(256, s_len)

    x2 = x.reshape(m, d)

    def ffn_layer(xin, layer):
        return _ffn(xin, norm_ffn[layer][None, :],
                    _cast_weight(ffn_w_gate, layer, D_FF), _cast_weight(ffn_w_up, layer, D_FF),
                    _cast_weight(ffn_w_down, layer, D_MODEL), tm_ffn, tf)

    qkvr, la = _gla_proj(
        x2, norm_mix[0][None, :],
        _cast_weight(gla_w_in, 0, GLA_MAIN),
        _pad_cols(gla_w_in[0][:, GLA_MAIN:], LANES).astype(BF16),
        jnp.pad(gla_w_g2[0], ((0, LANES - GLA_GATE_RANK), (0, 0))).astype(BF16),
        gla_b_g2[0][None, :], tm, tn)
    og = _gla_mix(qkvr.reshape(b_sz, s_len, GLA_MAIN), la.reshape(b_sz, s_len, GLA_KEY_DIM),
                  gla_o_gain[0][None, :], b_sz, s_len, gla_t, 4)
    x2 = _residual_matmul(x2, og.reshape(m, GLA_VAL_DIM), _cast_weight(gla_w_o, 0, D_MODEL), tm, tn)
    x2 = ffn_layer(x2, 0)

    qk_gain = jnp.concatenate([jnp.tile(fox_q_gain[0], FOX_HEADS), jnp.tile(fox_k_gain[0], FOX_HEADS)])
    f_lo, f_hi = 3 * D_MODEL, 3 * D_MODEL + FOX_HEADS
    qkvo, lf = _fox_proj(
        x2, norm_mix[1][None, :],
        _cast_weight(fox_w_in, 0, f_lo),
        fox_w_in[0][:, f_hi:].astype(BF16),
        _pad_cols(fox_w_in[0][:, f_lo:f_hi], LANES).astype(BF16),
        jnp.pad(fox_b_f[0], (0, LANES - FOX_HEADS))[None, :],
        qk_gain[None, :], tm, tn)
    c = _seq_cumsum(lf.reshape(b_sz, s_len, LANES), cum_t)
    c_hs = c[:, :, :FOX_HEADS].transpose(0, 2, 1)
    ck = c_hs.reshape(b_sz, FOX_HEADS, s_len // attn_blk, attn_blk)
    o = _fox_attn(qkvo.reshape(b_sz, s_len, FOX_MAIN), ck, b_sz, s_len, attn_blk, 2)
    x2 = _residual_matmul(x2, o.reshape(m, D_MODEL), _cast_weight(fox_w_o, 0, D_MODEL), tm, tn)
    x2 = ffn_layer(x2, 1)
    return x2.reshape(b_sz, s_len, d)
```

```python
import functools

import jax
import jax.numpy as jnp
from jax import lax
from jax.experimental import pallas as pl
from jax.experimental.pallas import tpu as pltpu

F32 = jnp.float32
BF16 = jnp.bfloat16

D_MODEL = 2048
RMS_EPS = 1e-6

GLA_HEADS = 4
GLA_KEY_DIM = D_MODEL // 2
GLA_VAL_DIM = D_MODEL
GLA_HEAD_K = GLA_KEY_DIM // GLA_HEADS
GLA_HEAD_V = GLA_VAL_DIM // GLA_HEADS
GLA_GATE_RANK = 16
GLA_GATE_TEMP = 16.0
GLA_CHUNK = 64
GLA_MAIN = 2 * GLA_KEY_DIM + 2 * GLA_VAL_DIM
GLA_DIAG = 8

FOX_HEAD_DIM = 128
FOX_HEADS = D_MODEL // FOX_HEAD_DIM
FOX_MAIN = 4 * D_MODEL
FOX_SCALE = FOX_HEAD_DIM ** -0.5
LOG2E = 1.4426950408889634

D_FF = ((8 * D_MODEL + 2) // 3 + 255) // 256 * 256

LANES = 128
NORM_SLAB = 256
CAST_ROWS = 256
CAST_COLS = 512
VMEM_LIMIT = 56 * 1024 * 1024

NT_DIMS = (((1,), (1,)), ((), ()))
TN_DIMS = (((0,), (0,)), ((), ()))


def _params(*sem, flags=None):
    return pltpu.CompilerParams(dimension_semantics=sem, vmem_limit_bytes=VMEM_LIMIT, flags=flags)


def _rmsnorm_rows(x, g):
    ms = jnp.mean(x * x, axis=-1, keepdims=True)
    return x * lax.rsqrt(ms + RMS_EPS) * g


def _log_sigmoid(z):
    return jnp.minimum(z, 0.0) - jnp.log1p(jnp.exp(-jnp.abs(z)))


def _norm_to_scratch(x_ref, g_ref, h_ref):
    rows = x_ref.shape[0]
    slab = min(NORM_SLAB, rows)
    g = g_ref[...]
    for r0 in range(0, rows, slab):
        h_ref[r0:r0 + slab, :] = _rmsnorm_rows(x_ref[r0:r0 + slab, :], g).astype(BF16)


def _split3(a):
    hi = a.astype(BF16)
    r1 = a - hi.astype(F32)
    mid = r1.astype(BF16)
    lo = (r1 - mid.astype(F32)).astype(BF16)
    return hi, mid, lo


def _cumsum_rows(tril, a):
    hi, mid, lo = _split3(a)
    out = jnp.dot(tril, hi, preferred_element_type=F32)
    out += jnp.dot(tril, mid, preferred_element_type=F32)
    out += jnp.dot(tril, lo, preferred_element_type=F32)
    return out


def _cast_kernel(w_ref, o_ref):
    o_ref[...] = w_ref[0].astype(BF16)


def _cast_weight(w3, layer, n_cols):
    _, k_dim, _ = w3.shape
    tr = CAST_ROWS
    assert k_dim % tr == 0 and n_cols % LANES == 0
    return pl.pallas_call(
        _cast_kernel,
        grid=(k_dim // tr,),
        in_specs=[pl.BlockSpec((1, tr, n_cols), lambda i: (layer, i, 0))],
        out_specs=pl.BlockSpec((tr, n_cols), lambda i: (i, 0)),
        out_shape=jax.ShapeDtypeStruct((k_dim, n_cols), BF16),
        compiler_params=_params("parallel"),
        name="cast_weight",
    )(w3)


def _cast_t_kernel(wt_ref, o_ref):
    o_ref[...] = wt_ref[0].T.astype(BF16)


def _cast_weight_t(wt3, layer, n_cols):
    _, _, k_dim = wt3.shape
    tc = CAST_COLS
    assert n_cols % tc == 0
    return pl.pallas_call(
        _cast_t_kernel,
        grid=(n_cols // tc,),
        in_specs=[pl.BlockSpec((1, tc, k_dim), lambda i: (layer, i, 0))],
        out_specs=pl.BlockSpec((k_dim, tc), lambda i: (0, i)),
        out_shape=jax.ShapeDtypeStruct((k_dim, n_cols), BF16),
        compiler_params=_params("parallel"),
        name="cast_weight_t",
    )(wt3)


def _gla_proj_kernel(x_ref, g_ref, w_ref, wg1_ref, wg2_ref, bg_ref, o_ref, la_ref, h_ref):
    @pl.when(pl.program_id(1) == 0)
    def _():
        _norm_to_scratch(x_ref, g_ref, h_ref)
        h = h_ref[...]
        g1 = lax.dot_general(h, wg1_ref[...].astype(BF16), NT_DIMS, preferred_element_type=F32)
        z = jnp.dot(g1.astype(BF16), wg2_ref[...].astype(BF16),
                    preferred_element_type=F32) + bg_ref[...]
        la_ref[...] = _log_sigmoid(z) * (LOG2E / GLA_GATE_TEMP)

    o_ref[...] = jnp.dot(h_ref[...], w_ref[...], preferred_element_type=F32).astype(o_ref.dtype)


def _gla_proj(x2, g, w_main, wg1, wg2, bg, tm, tn):
    m = x2.shape[0]
    return pl.pallas_call(
        _gla_proj_kernel,
        grid=(m // tm, GLA_MAIN // tn),
        in_specs=[
            pl.BlockSpec((tm, D_MODEL), lambda i, j: (i, 0)),
            pl.BlockSpec((1, D_MODEL), lambda i, j: (0, 0)),
            pl.BlockSpec((D_MODEL, tn), lambda i, j: (0, j)),
            pl.BlockSpec((LANES, D_MODEL), lambda i, j: (0, 0)),
            pl.BlockSpec((LANES, GLA_KEY_DIM), lambda i, j: (0, 0)),
            pl.BlockSpec((1, GLA_KEY_DIM), lambda i, j: (0, 0)),
        ],
        out_specs=[
            pl.BlockSpec((tm, tn), lambda i, j: (i, j)),
            pl.BlockSpec((tm, GLA_KEY_DIM), lambda i, j: (i, 0)),
        ],
        out_shape=[
            jax.ShapeDtypeStruct((m, GLA_MAIN), BF16),
            jax.ShapeDtypeStruct((m, GLA_KEY_DIM), F32),
        ],
        scratch_shapes=[pltpu.VMEM((tm, D_MODEL), BF16)],
        compiler_params=_params("parallel", "arbitrary"),
        name="gla_proj",
    )(x2, g, w_main, wg1, wg2, bg)


def _gla_kernel(q_ref, k_ref, v_ref, r_ref, la_ref, gain_ref, o_ref, st_ref, *, n_chunks, heads):
    c_len = GLA_CHUNK

    @pl.when(pl.program_id(2) == 0)
    def _():
        st_ref[...] = jnp.zeros_like(st_ref)

    row = lax.broadcasted_iota(jnp.int32, (c_len, c_len), 0)
    col = lax.broadcasted_iota(jnp.int32, (c_len, c_len), 1)
    tril = (col <= row).astype(BF16)
    delta = row - col
    band = jnp.where((delta >= 0) & (delta <= row % GLA_DIAG), delta, -1)
    level_masks = []
    block = 2 * GLA_DIAG
    while block <= c_len:
        half = block // 2
        mk = (row // block == col // block) & (row % block >= half) & (col % block < half)
        level_masks.append((block, mk))
        block *= 2
    gain = gain_ref[...]
    scale = GLA_HEAD_K ** -0.5

    def head_chunk(hd, sl):
        ksl = slice(hd * GLA_HEAD_K, (hd + 1) * GLA_HEAD_K)
        vsl = slice(hd * GLA_HEAD_V, (hd + 1) * GLA_HEAD_V)
        la = la_ref[0, sl, ksl]
        b = _cumsum_rows(tril, la)
        b_last = b[c_len - 1:c_len, :]
        qf = q_ref[0, sl, ksl].astype(F32) * scale
        kf = k_ref[0, sl, ksl].astype(F32)
        v = v_ref[0, sl, vsl]

        st = st_ref[hd]
        qi = (qf * jnp.exp2(b)).astype(BF16)
        o = lax.dot_general(qi, st.astype(BF16), NT_DIMS, preferred_element_type=F32)
        kd = (kf * jnp.exp2(b_last - b)).astype(BF16)
        st_ref[hd] = st * jnp.exp2(b_last) + lax.dot_general(
            v, kd, TN_DIMS, preferred_element_type=F32)

        attn = jnp.zeros((c_len, c_len), F32)
        for blk, mk in level_masks:
            half = blk // 2
            ref = jnp.concatenate(
                [jnp.broadcast_to(b[s + half - 1:s + half, :], (blk, GLA_HEAD_K))
                 for s in range(0, c_len, blk)], axis=0)
            ql = (qf * jnp.exp2(b - ref)).astype(BF16)
            kl = (kf * jnp.exp2(ref - b)).astype(BF16)
            attn = jnp.where(mk, lax.dot_general(ql, kl, NT_DIMS, preferred_element_type=F32), attn)
        for d in range(GLA_DIAG):
            k_d = pltpu.roll(kf, d, 0) if d else kf
            b_d = pltpu.roll(b, d, 0) if d else b
            diag = jnp.sum(qf * k_d * jnp.exp2(b - b_d), axis=-1, keepdims=True)
            attn = jnp.where(band == d, diag, attn)

        o += jnp.dot(attn.astype(BF16), v, preferred_element_type=F32)
        on = _rmsnorm_rows(o, gain)
        r = r_ref[0, sl, vsl].astype(F32)
        o_ref[0, sl, vsl] = (on * (r * jax.nn.sigmoid(r))).astype(o_ref.dtype)

    def chunk(c, carry):
        sl = pl.ds(pl.multiple_of(c * c_len, c_len), c_len)
        for hd in range(heads):
            head_chunk(hd, sl)
        return carry

    lax.fori_loop(0, n_chunks, chunk, 0)


def _gla_mix(qkvr, la, gain, b_sz, s_len, t_len, heads):
    n_chunks = t_len // GLA_CHUNK
    wk = heads * GLA_HEAD_K
    wv = heads * GLA_HEAD_V
    kq = GLA_KEY_DIM // wk
    kv = 2 * GLA_KEY_DIM // wv
    kr = kv + GLA_VAL_DIM // wv
    return pl.pallas_call(
        functools.partial(_gla_kernel, n_chunks=n_chunks, heads=heads),
        grid=(b_sz, GLA_HEADS // heads, s_len // t_len),
        in_specs=[
            pl.BlockSpec((1, t_len, wk), lambda b, h, t: (b, t, h)),
            pl.BlockSpec((1, t_len, wk), lambda b, h, t: (b, t, kq + h)),
            pl.BlockSpec((1, t_len, wv), lambda b, h, t: (b, t, kv + h)),
            pl.BlockSpec((1, t_len, wv), lambda b, h, t: (b, t, kr + h)),
            pl.BlockSpec((1, t_len, wk), lambda b, h, t: (b, t, h)),
            pl.BlockSpec((1, GLA_HEAD_V), lambda b, h, t: (0, 0)),
        ],
        out_specs=pl.BlockSpec((1, t_len, wv), lambda b, h, t: (b, t, h)),
        out_shape=jax.ShapeDtypeStruct((b_sz, s_len, GLA_VAL_DIM), BF16),
        scratch_shapes=[pltpu.VMEM((heads, GLA_HEAD_V, GLA_HEAD_K), F32)],
        compiler_params=_params("parallel", "parallel", "arbitrary"),
        name="gla_mix",
    )(qkvr, qkvr, qkvr, qkvr, la, gain)


def _residual_matmul_kernel(x_ref, a_ref, w_ref, o_ref):
    o_ref[...] = x_ref[...] + jnp.dot(a_ref[...], w_ref[...], preferred_element_type=F32)


def _residual_matmul(x2, a, w, tm, tn):
    m, k_dim = a.shape
    n = w.shape[1]
    return pl.pallas_call(
        _residual_matmul_kernel,
        grid=(m // tm, n // tn),
        in_specs=[
            pl.BlockSpec((tm, tn), lambda i, j: (i, j)),
            pl.BlockSpec((tm, k_dim), lambda i, j: (i, 0)),
            pl.BlockSpec((k_dim, tn), lambda i, j: (0, j)),
        ],
        out_specs=pl.BlockSpec((tm, tn), lambda i, j: (i, j)),
        out_shape=jax.ShapeDtypeStruct((m, n), F32),
        compiler_params=_params("parallel", "arbitrary"),
        name="residual_matmul",
    )(x2, a, w)


def _ffn_kernel(x_ref, g_ref, wg_ref, wu_ref, wd_ref, o_ref, h_ref):
    @pl.when(pl.program_id(1) == 0)
    def _():
        _norm_to_scratch(x_ref, g_ref, h_ref)
        o_ref[...] = x_ref[...]

    h = h_ref[...]
    gate = jnp.dot(h, wg_ref[...], preferred_element_type=F32)
    up = jnp.dot(h, wu_ref[...], preferred_element_type=F32)
    act = (gate * jax.nn.sigmoid(gate) * up).astype(BF16)
    o_ref[...] += jnp.dot(act, wd_ref[...], preferred_element_type=F32)


def _ffn(x2, g, wg, wu, wd, tm, tf):
    m = x2.shape[0]
    return pl.pallas_call(
        _ffn_kernel,
        grid=(m // tm, D_FF // tf),
        in_specs=[
            pl.BlockSpec((tm, D_MODEL), lambda i, f: (i, 0)),
            pl.BlockSpec((1, D_MODEL), lambda i, f: (0, 0)),
            pl.BlockSpec((D_MODEL, tf), lambda i, f: (0, f)),
            pl.BlockSpec((D_MODEL, tf), lambda i, f: (0, f)),
            pl.BlockSpec((tf, D_MODEL), lambda i, f: (f, 0)),
        ],
        out_specs=pl.BlockSpec((tm, D_MODEL), lambda i, f: (i, 0)),
        out_shape=jax.ShapeDtypeStruct((m, D_MODEL), F32),
        scratch_shapes=[pltpu.VMEM((tm, D_MODEL), BF16)],
        compiler_params=_params("parallel", "arbitrary"),
        name="ffn",
    )(x2, g, wg, wu, wd)


def _fox_proj_kernel(x_ref, g_ref, wa_ref, wb_ref, wf_ref, bf_ref, qkg_ref, o_ref, lf_ref, h_ref,
                     *, q_tiles, qk_tiles, a_tiles):
    j = pl.program_id(1)

    @pl.when(j == 0)
    def _():
        _norm_to_scratch(x_ref, g_ref, h_ref)
        h = h_ref[...]
        z = lax.dot_general(h, wf_ref[...].astype(BF16), NT_DIMS,
                            preferred_element_type=F32) + bf_ref[...]
        lf_ref[...] = _log_sigmoid(z)

    @pl.when(j < qk_tiles)
    def _():
        acc = jnp.dot(h_ref[...], wa_ref[...], preferred_element_type=F32)
        mult = jnp.where(j < q_tiles, FOX_SCALE * LOG2E, 1.0).astype(F32)
        for c in range(acc.shape[1] // FOX_HEAD_DIM):
            cs = slice(c * FOX_HEAD_DIM, (c + 1) * FOX_HEAD_DIM)
            xs = acc[:, cs]
            ms = jnp.mean(xs * xs, axis=-1, keepdims=True)
            o_ref[:, cs] = (xs * (lax.rsqrt(ms + RMS_EPS) * mult) * qkg_ref[:, cs]).astype(o_ref.dtype)

    @pl.when((j >= qk_tiles) & (j < a_tiles))
    def _():
        o_ref[...] = jnp.dot(h_ref[...], wa_ref[...], preferred_element_type=F32).astype(o_ref.dtype)

    @pl.when(j >= a_tiles)
    def _():
        o_ref[...] = jnp.dot(h_ref[...], wb_ref[...], preferred_element_type=F32).astype(o_ref.dtype)


def _fox_proj(x2, g, w_qkv, w_og, wf, bf, qk_gain, tm, tn):
    m = x2.shape[0]
    q_tiles = D_MODEL // tn
    qk_tiles = 2 * q_tiles
    a_tiles = w_qkv.shape[1] // tn
    return pl.pallas_call(
        functools.partial(_fox_proj_kernel, q_tiles=q_tiles, qk_tiles=qk_tiles, a_tiles=a_tiles),
        grid=(m // tm, FOX_MAIN // tn),
        in_specs=[
            pl.BlockSpec((tm, D_MODEL), lambda i, j: (i, 0)),
            pl.BlockSpec((1, D_MODEL), lambda i, j: (0, 0)),
            pl.BlockSpec((D_MODEL, tn), lambda i, j: (0, jnp.minimum(j, a_tiles - 1))),
            pl.BlockSpec((D_MODEL, tn), lambda i, j: (0, jnp.maximum(j - a_tiles, 0))),
            pl.BlockSpec((LANES, D_MODEL), lambda i, j: (0, 0)),
            pl.BlockSpec((1, LANES), lambda i, j: (0, 0)),
            pl.BlockSpec((1, tn), lambda i, j: (0, jnp.minimum(j, qk_tiles - 1))),
        ],
        out_specs=[
            pl.BlockSpec((tm, tn), lambda i, j: (i, j)),
            pl.BlockSpec((tm, LANES), lambda i, j: (i, 0)),
        ],
        out_shape=[
            jax.ShapeDtypeStruct((m, FOX_MAIN), BF16),
            jax.ShapeDtypeStruct((m, LANES), F32),
        ],
        scratch_shapes=[pltpu.VMEM((tm, D_MODEL), BF16)],
        compiler_params=_params("parallel", "arbitrary"),
        name="fox_proj",
    )(x2, g, w_qkv, w_og, wf, bf, qk_gain)


def _cumsum_kernel(lf_ref, c_ref, carry_ref):
    @pl.when(pl.program_id(1) == 0)
    def _():
        carry_ref[...] = jnp.zeros_like(carry_ref)

    t_len = lf_ref.shape[1]
    row = lax.broadcasted_iota(jnp.int32, (t_len, t_len), 0)
    col = lax.broadcasted_iota(jnp.int32, (t_len, t_len), 1)
    tril = (col <= row).astype(BF16)
    c = _cumsum_rows(tril, lf_ref[0]) + carry_ref[...]
    c_ref[0] = c * LOG2E
    carry_ref[...] = c[t_len - 1:t_len, :]


def _seq_cumsum(lf, t_len):
    b_sz, s_len, _ = lf.shape
    return pl.pallas_call(
        _cumsum_kernel,
        grid=(b_sz, s_len // t_len),
        in_specs=[pl.BlockSpec((1, t_len, LANES), lambda b, t: (b, t, 0))],
        out_specs=pl.BlockSpec((1, t_len, LANES), lambda b, t: (b, t, 0)),
        out_shape=jax.ShapeDtypeStruct(lf.shape, F32),
        scratch_shapes=[pltpu.VMEM((1, LANES), F32)],
        compiler_params=_params("parallel", "arbitrary"),
        name="fox_cumsum",
    )(lf)


def _fox_attn_kernel(q_ref, k_ref, v_ref, og_ref, ck_ref, o_ref, acc_ref, *, blk, n_sub, tk):
    i = pl.program_id(2)
    sub = blk // n_sub
    per_blk = blk // tk
    ck_own = jnp.concatenate(
        [ck_ref[0, 0, pl.ds(i * per_blk + t, 1), :] for t in range(per_blk)], axis=1)
    c0 = ck_own[:, 0:1]
    acc_ref[...] = jnp.zeros_like(acc_ref)

    def update(r, carry, ks, vs, bias, mask_offset):
        m_old, l_old = carry
        rows = slice(r * sub, (r + 1) * sub)
        s = lax.dot_general(q_ref[0, rows, :], ks, NT_DIMS, preferred_element_type=F32) + bias
        if mask_offset is not None:
            row = lax.broadcasted_iota(jnp.int32, s.shape, 0)
            col = lax.broadcasted_iota(jnp.int32, s.shape, 1)
            s = jnp.where(col <= row + mask_offset, s, -jnp.inf)
        m_new = jnp.maximum(m_old, jnp.max(s, axis=-1, keepdims=True))
        p = jnp.exp2(s - m_new)
        alpha = jnp.exp2(m_old - m_new)
        l_new = alpha * l_old + jnp.sum(p, axis=-1, keepdims=True)
        acc_ref[rows, :] = alpha * acc_ref[rows, :] + jnp.dot(
            p.astype(BF16), vs, preferred_element_type=F32)
        return m_new, l_new

    def body(j, carry):
        sl = pl.ds(pl.multiple_of(j * tk, tk), tk)
        ks = k_ref[0, sl, :]
        vs = v_ref[0, sl, :]
        bias = c0 - ck_ref[0, 0, pl.ds(j, 1), :]
        return tuple(update(r, carry[r], ks, vs, bias, None) for r in range(n_sub))

    init = tuple((jnp.full((sub, 1), -jnp.inf, F32), jnp.zeros((sub, 1), F32)) for _ in range(n_sub))
    carry = lax.fori_loop(0, i * per_blk, body, init)

    base = pl.multiple_of(i * blk, blk)
    for r in range(n_sub):
        width = (r + 1) * sub
        ks = k_ref[0, pl.ds(base, width), :]
        vs = v_ref[0, pl.ds(base, width), :]
        bias = c0 - ck_own[:, 0:width]
        m_fin, l_fin = update(r, carry[r], ks, vs, bias, r * sub)
        rows = slice(r * sub, (r + 1) * sub)
        gate = jax.nn.sigmoid(og_ref[0, rows, :].astype(F32))
        o_ref[0, rows, :] = (acc_ref[rows, :] / l_fin * gate).astype(o_ref.dtype)


def _fox_attn(qkvo, ck, b_sz, s_len, blk, n_sub):
    h_cols = D_MODEL // FOX_HEAD_DIM
    n_blk = s_len // blk
    n_kt, tk = ck.shape[2:]
    return pl.pallas_call(
        functools.partial(_fox_attn_kernel, blk=blk, n_sub=n_sub, tk=tk),
        grid=(b_sz, FOX_HEADS, n_blk),
        in_specs=[
            pl.BlockSpec((1, blk, FOX_HEAD_DIM), lambda b, h, i: (b, i, h)),
            pl.BlockSpec((1, s_len, FOX_HEAD_DIM), lambda b, h, i: (b, 0, h_cols + h)),
            pl.BlockSpec((1, s_len, FOX_HEAD_DIM), lambda b, h, i: (b, 0, 2 * h_cols + h)),
            pl.BlockSpec((1, blk, FOX_HEAD_DIM), lambda b, h, i: (b, i, 3 * h_cols + h)),
            pl.BlockSpec((1, 1, n_kt, tk), lambda b, h, i: (b, h, 0, 0)),
        ],
        out_specs=pl.BlockSpec((1, blk, FOX_HEAD_DIM), lambda b, h, i: (b, i, h)),
        out_shape=jax.ShapeDtypeStruct((b_sz, s_len, D_MODEL), BF16),
        scratch_shapes=[pltpu.VMEM((blk, FOX_HEAD_DIM), F32)],
        compiler_params=_params("parallel", "parallel", "arbitrary"),
        name="fox_attn",
    )(qkvo, qkvo, qkvo, qkvo, ck)


def _pad_rows(w, n):
    return jnp.pad(w, ((0, n - w.shape[0]), (0, 0)))


def kernel(x, norm_mix, norm_ffn, gla_w_in, gla_w_g2, gla_b_g2, gla_o_gain, gla_w_o,
           fox_w_in, fox_b_f, fox_q_gain, fox_k_gain, fox_w_o,
           ffn_w_gate, ffn_w_up, ffn_w_down):
    b_sz, s_len, d = x.shape
    assert d == D_MODEL and s_len % GLA_CHUNK == 0
    m = b_sz * s_len
    tm = min(1024, m)
    tn = 1024
    tm_ffn = min(512, m)
    tf = 512
    gla_t = min(512, s_len)
    attn_blk = min(1024, s_len)
    attn_sub = min(512, s_len)
    attn_tk = min(512, s_len)
    cum_t = min(256, s_len)

    x2 = x.reshape(m, d)

    def ffn_layer(xin, layer):
        return _ffn(xin, norm_ffn[layer][None, :],
                    _cast_weight(ffn_w_gate, layer, D_FF), _cast_weight(ffn_w_up, layer, D_FF),
                    _cast_weight(ffn_w_down, layer, D_MODEL), tm_ffn, tf)

    gla_wt = jnp.swapaxes(gla_w_in, 1, 2)
    qkvr, la = _gla_proj(
        x2, norm_mix[0][None, :],
        _cast_weight_t(gla_wt, 0, GLA_MAIN),
        _pad_rows(gla_wt[0, GLA_MAIN:, :], LANES),
        _pad_rows(gla_w_g2[0], LANES),
        gla_b_g2[0][None, :], tm, tn)
    og = _gla_mix(qkvr.reshape(b_sz, s_len, GLA_MAIN), la.reshape(b_sz, s_len, GLA_KEY_DIM),
                  gla_o_gain[0][None, :], b_sz, s_len, gla_t, 4)
    x2 = _residual_matmul(x2, og.reshape(m, GLA_VAL_DIM), _cast_weight(gla_w_o, 0, D_MODEL), tm, tn)
    x2 = ffn_layer(x2, 0)

    qk_gain = jnp.concatenate([jnp.tile(fox_q_gain[0], FOX_HEADS), jnp.tile(fox_k_gain[0], FOX_HEADS)])
    f_lo, f_hi = 3 * D_MODEL, 3 * D_MODEL + FOX_HEADS
    fox_wt = jnp.swapaxes(fox_w_in, 1, 2)
    qkvo, lf = _fox_proj(
        x2, norm_mix[1][None, :],
        _cast_weight_t(fox_wt, 0, f_lo),
        _cast_weight_t(fox_wt[:, f_hi:, :], 0, D_MODEL),
        _pad_rows(fox_wt[0, f_lo:f_hi, :], LANES),
        jnp.pad(fox_b_f[0], (0, LANES - FOX_HEADS))[None, :],
        qk_gain[None, :], tm, tn)
    c = _seq_cumsum(lf.reshape(b_sz, s_len, LANES), cum_t)
    c_hs = c[:, :, :FOX_HEADS].transpose(0, 2, 1)
    ck = c_hs.reshape(b_sz, FOX_HEADS, s_len // attn_tk, attn_tk)
    o = _fox_attn(qkvo.reshape(b_sz, s_len, FOX_MAIN), ck, b_sz, s_len, attn_blk,
                  attn_blk // attn_sub)
    x2 = _residual_matmul(x2, o.reshape(m, D_MODEL), _cast_weight(fox_w_o, 0, D_MODEL), tm, tn)
    x2 = ffn_layer(x2, 1)
    return x2.reshape(b_sz, s_len, d)
```

```python
import functools

import jax
import jax.numpy as jnp
from jax import lax
from jax.experimental import pallas as pl
from jax.experimental.pallas import tpu as pltpu

F32 = jnp.float32
BF16 = jnp.bfloat16

D_MODEL = 2048
RMS_EPS = 1e-6

GLA_HEADS = 4
GLA_KEY_DIM = D_MODEL // 2
GLA_VAL_DIM = D_MODEL
GLA_HEAD_K = GLA_KEY_DIM // GLA_HEADS
GLA_HEAD_V = GLA_VAL_DIM // GLA_HEADS
GLA_GATE_RANK = 16
GLA_GATE_TEMP = 16.0
GLA_CHUNK = 64
GLA_MAIN = 2 * GLA_KEY_DIM + 2 * GLA_VAL_DIM
GLA_DIAG = 8

FOX_HEAD_DIM = 128
FOX_HEADS = D_MODEL // FOX_HEAD_DIM
FOX_MAIN = 4 * D_MODEL
FOX_SCALE = FOX_HEAD_DIM ** -0.5
LOG2E = 1.4426950408889634

D_FF = ((8 * D_MODEL + 2) // 3 + 255) // 256 * 256

LANES = 128
NORM_SLAB = 256
CAST_ROWS = 256
CAST_COLS = 512
VMEM_LIMIT = 56 * 1024 * 1024

NT_DIMS = (((1,), (1,)), ((), ()))
TN_DIMS = (((0,), (0,)), ((), ()))


def _params(*sem, flags=None):
    return pltpu.CompilerParams(dimension_semantics=sem, vmem_limit_bytes=VMEM_LIMIT, flags=flags)


def _rmsnorm_rows(x, g):
    ms = jnp.mean(x * x, axis=-1, keepdims=True)
    return x * lax.rsqrt(ms + RMS_EPS) * g


def _log_sigmoid(z):
    return jnp.minimum(z, 0.0) - jnp.log1p(jnp.exp(-jnp.abs(z)))


def _norm_to_scratch(x_ref, g_ref, h_ref):
    rows = x_ref.shape[0]
    slab = min(NORM_SLAB, rows)
    g = g_ref[...]
    for r0 in range(0, rows, slab):
        h_ref[r0:r0 + slab, :] = _rmsnorm_rows(x_ref[r0:r0 + slab, :], g).astype(BF16)


def _split3(a):
    hi = a.astype(BF16)
    r1 = a - hi.astype(F32)
    mid = r1.astype(BF16)
    lo = (r1 - mid.astype(F32)).astype(BF16)
    return hi, mid, lo


def _cumsum_rows(tril, a):
    hi, mid, lo = _split3(a)
    out = jnp.dot(tril, hi, preferred_element_type=F32)
    out += jnp.dot(tril, mid, preferred_element_type=F32)
    out += jnp.dot(tril, lo, preferred_element_type=F32)
    return out


def _cast_kernel(w_ref, o_ref):
    o_ref[...] = w_ref[0].astype(BF16)


def _cast_weight(w3, layer, n_cols):
    _, k_dim, _ = w3.shape
    tr = CAST_ROWS
    assert k_dim % tr == 0 and n_cols % LANES == 0
    return pl.pallas_call(
        _cast_kernel,
        grid=(k_dim // tr,),
        in_specs=[pl.BlockSpec((1, tr, n_cols), lambda i: (layer, i, 0))],
        out_specs=pl.BlockSpec((tr, n_cols), lambda i: (i, 0)),
        out_shape=jax.ShapeDtypeStruct((k_dim, n_cols), BF16),
        compiler_params=_params("parallel"),
        name="cast_weight",
    )(w3)


def _cast_t_kernel(wt_ref, o_ref):
    o_ref[...] = wt_ref[0].T.astype(BF16)


def _cast_weight_t(wt3, layer, n_cols):
    _, _, k_dim = wt3.shape
    tc = CAST_COLS
    assert n_cols % tc == 0
    return pl.pallas_call(
        _cast_t_kernel,
        grid=(n_cols // tc,),
        in_specs=[pl.BlockSpec((1, tc, k_dim), lambda i: (layer, i, 0))],
        out_specs=pl.BlockSpec((k_dim, tc), lambda i: (0, i)),
        out_shape=jax.ShapeDtypeStruct((k_dim, n_cols), BF16),
        compiler_params=_params("parallel"),
        name="cast_weight_t",
    )(wt3)


def _gla_proj_kernel(x_ref, g_ref, w_ref, wg1_ref, wg2_ref, bg_ref, o_ref, la_ref, h_ref):
    @pl.when(pl.program_id(1) == 0)
    def _():
        _norm_to_scratch(x_ref, g_ref, h_ref)
        h = h_ref[...]
        g1 = lax.dot_general(h, wg1_ref[...].astype(BF16), NT_DIMS, preferred_element_type=F32)
        z = jnp.dot(g1.astype(BF16), wg2_ref[...].astype(BF16),
                    preferred_element_type=F32) + bg_ref[...]
        la_ref[...] = _log_sigmoid(z) * (LOG2E / GLA_GATE_TEMP)

    o_ref[...] = jnp.dot(h_ref[...], w_ref[...], preferred_element_type=F32).astype(o_ref.dtype)


def _gla_proj(x2, g, w_main, wg1, wg2, bg, tm, tn):
    m = x2.shape[0]
    return pl.pallas_call(
        _gla_proj_kernel,
        grid=(m // tm, GLA_MAIN // tn),
        in_specs=[
            pl.BlockSpec((tm, D_MODEL), lambda i, j: (i, 0)),
            pl.BlockSpec((1, D_MODEL), lambda i, j: (0, 0)),
            pl.BlockSpec((D_MODEL, tn), lambda i, j: (0, j)),
            pl.BlockSpec((LANES, D_MODEL), lambda i, j: (0, 0)),
            pl.BlockSpec((LANES, GLA_KEY_DIM), lambda i, j: (0, 0)),
            pl.BlockSpec((1, GLA_KEY_DIM), lambda i, j: (0, 0)),
        ],
        out_specs=[
            pl.BlockSpec((tm, tn), lambda i, j: (i, j)),
            pl.BlockSpec((tm, GLA_KEY_DIM), lambda i, j: (i, 0)),
        ],
        out_shape=[
            jax.ShapeDtypeStruct((m, GLA_MAIN), BF16),
            jax.ShapeDtypeStruct((m, GLA_KEY_DIM), F32),
        ],
        scratch_shapes=[pltpu.VMEM((tm, D_MODEL), BF16)],
        compiler_params=_params("parallel", "arbitrary"),
        name="gla_proj",
    )(x2, g, w_main, wg1, wg2, bg)


def _gla_kernel(q_ref, k_ref, v_ref, r_ref, la_ref, gain_ref, o_ref, st_ref, *, n_chunks, heads):
    c_len = GLA_CHUNK

    @pl.when(pl.program_id(2) == 0)
    def _():
        st_ref[...] = jnp.zeros_like(st_ref)

    row = lax.broadcasted_iota(jnp.int32, (c_len, c_len), 0)
    col = lax.broadcasted_iota(jnp.int32, (c_len, c_len), 1)
    tril = (col <= row).astype(BF16)
    delta = row - col
    band = jnp.where((delta >= 0) & (delta <= row % GLA_DIAG), delta, -1)
    level_masks = []
    block = 2 * GLA_DIAG
    while block <= c_len:
        half = block // 2
        mk = (row // block == col // block) & (row % block >= half) & (col % block < half)
        level_masks.append((block, mk))
        block *= 2
    gain = gain_ref[...]
    scale = GLA_HEAD_K ** -0.5

    def head_chunk(hd, sl):
        ksl = slice(hd * GLA_HEAD_K, (hd + 1) * GLA_HEAD_K)
        vsl = slice(hd * GLA_HEAD_V, (hd + 1) * GLA_HEAD_V)
        la = la_ref[0, sl, ksl]
        b = _cumsum_rows(tril, la)
        b_last = b[c_len - 1:c_len, :]
        qf = q_ref[0, sl, ksl].astype(F32) * scale
        kf = k_ref[0, sl, ksl].astype(F32)
        v = v_ref[0, sl, vsl]

        st = st_ref[hd]
        qi = (qf * jnp.exp2(b)).astype(BF16)
        o = lax.dot_general(qi, st.astype(BF16), NT_DIMS, preferred_element_type=F32)
        kd = (kf * jnp.exp2(b_last - b)).astype(BF16)
        st_ref[hd] = st * jnp.exp2(b_last) + lax.dot_general(
            v, kd, TN_DIMS, preferred_element_type=F32)

        attn = jnp.zeros((c_len, c_len), F32)
        for blk, mk in level_masks:
            half = blk // 2
            ref = jnp.concatenate(
                [jnp.broadcast_to(b[s + half - 1:s + half, :], (blk, GLA_HEAD_K))
                 for s in range(0, c_len, blk)], axis=0)
            ql = (qf * jnp.exp2(b - ref)).astype(BF16)
            kl = (kf * jnp.exp2(ref - b)).astype(BF16)
            attn = jnp.where(mk, lax.dot_general(ql, kl, NT_DIMS, preferred_element_type=F32), attn)
        for d in range(GLA_DIAG):
            k_d = pltpu.roll(kf, d, 0) if d else kf
            b_d = pltpu.roll(b, d, 0) if d else b
            diag = jnp.sum(qf * k_d * jnp.exp2(b - b_d), axis=-1, keepdims=True)
            attn = jnp.where(band == d, diag, attn)

        o += jnp.dot(attn.astype(BF16), v, preferred_element_type=F32)
        on = _rmsnorm_rows(o, gain)
        r = r_ref[0, sl, vsl].astype(F32)
        o_ref[0, sl, vsl] = (on * (r * jax.nn.sigmoid(r))).astype(o_ref.dtype)

    def chunk(c, carry):
        sl = pl.ds(pl.multiple_of(c * c_len, c_len), c_len)
        for hd in range(heads):
            head_chunk(hd, sl)
        return carry

    lax.fori_loop(0, n_chunks, chunk, 0)


def _gla_mix(qkvr, la, gain, b_sz, s_len, t_len, heads):
    n_chunks = t_len // GLA_CHUNK
    wk = heads * GLA_HEAD_K
    wv = heads * GLA_HEAD_V
    kq = GLA_KEY_DIM // wk
    kv = 2 * GLA_KEY_DIM // wv
    kr = kv + GLA_VAL_DIM // wv
    return pl.pallas_call(
        functools.partial(_gla_kernel, n_chunks=n_chunks, heads=heads),
        grid=(b_sz, GLA_HEADS // heads, s_len // t_len),
        in_specs=[
            pl.BlockSpec((1, t_len, wk), lambda b, h, t: (b, t, h)),
            pl.BlockSpec((1, t_len, wk), lambda b, h, t: (b, t, kq + h)),
            pl.BlockSpec((1, t_len, wv), lambda b, h, t: (b, t, kv + h)),
            pl.BlockSpec((1, t_len, wv), lambda b, h, t: (b, t, kr + h)),
            pl.BlockSpec((1, t_len, wk), lambda b, h, t: (b, t, h)),
            pl.BlockSpec((1, GLA_HEAD_V), lambda b, h, t: (0, 0)),
        ],
        out_specs=pl.BlockSpec((1, t_len, wv), lambda b, h, t: (b, t, h)),
        out_shape=jax.ShapeDtypeStruct((b_sz, s_len, GLA_VAL_DIM), BF16),
        scratch_shapes=[pltpu.VMEM((heads, GLA_HEAD_V, GLA_HEAD_K), F32)],
        compiler_params=_params("parallel", "parallel", "arbitrary"),
        name="gla_mix",
    )(qkvr, qkvr, qkvr, qkvr, la, gain)


def _residual_matmul_kernel(x_ref, a_ref, w_ref, o_ref):
    o_ref[...] = x_ref[...] + jnp.dot(a_ref[...], w_ref[...], preferred_element_type=F32)


def _residual_matmul(x2, a, w, tm, tn):
    m, k_dim = a.shape
    n = w.shape[1]
    return pl.pallas_call(
        _residual_matmul_kernel,
        grid=(m // tm, n // tn),
        in_specs=[
            pl.BlockSpec((tm, tn), lambda i, j: (i, j)),
            pl.BlockSpec((tm, k_dim), lambda i, j: (i, 0)),
            pl.BlockSpec((k_dim, tn), lambda i, j: (0, j)),
        ],
        out_specs=pl.BlockSpec((tm, tn), lambda i, j: (i, j)),
        out_shape=jax.ShapeDtypeStruct((m, n), F32),
        compiler_params=_params("parallel", "arbitrary"),
        name="residual_matmul",
    )(x2, a, w)


def _ffn_kernel(x_ref, g_ref, wg_ref, wu_ref, wd_ref, o_ref, h_ref):
    @pl.when(pl.program_id(1) == 0)
    def _():
        _norm_to_scratch(x_ref, g_ref, h_ref)
        o_ref[...] = x_ref[...]

    h = h_ref[...]
    gate = jnp.dot(h, wg_ref[0].astype(BF16), preferred_element_type=F32)
    up = jnp.dot(h, wu_ref[0].astype(BF16), preferred_element_type=F32)
    act = (gate * jax.nn.sigmoid(gate) * up).astype(BF16)
    o_ref[...] += jnp.dot(act, wd_ref[0].astype(BF16), preferred_element_type=F32)


def _ffn(x2, g, wg, wu, wd, layer, tm, tf):
    m = x2.shape[0]
    return pl.pallas_call(
        _ffn_kernel,
        grid=(m // tm, D_FF // tf),
        in_specs=[
            pl.BlockSpec((tm, D_MODEL), lambda i, f: (i, 0)),
            pl.BlockSpec((1, D_MODEL), lambda i, f: (0, 0)),
            pl.BlockSpec((1, D_MODEL, tf), lambda i, f: (layer, 0, f)),
            pl.BlockSpec((1, D_MODEL, tf), lambda i, f: (layer, 0, f)),
            pl.BlockSpec((1, tf, D_MODEL), lambda i, f: (layer, f, 0)),
        ],
        out_specs=pl.BlockSpec((tm, D_MODEL), lambda i, f: (i, 0)),
        out_shape=jax.ShapeDtypeStruct((m, D_MODEL), F32),
        scratch_shapes=[pltpu.VMEM((tm, D_MODEL), BF16)],
        compiler_params=_params("parallel", "arbitrary"),
        name="ffn",
    )(x2, g, wg, wu, wd)


def _fox_proj_kernel(x_ref, g_ref, wa_ref, wb_ref, wf_ref, bf_ref, qkg_ref, o_ref, lf_ref, h_ref,
                     *, q_tiles, qk_tiles, a_tiles):
    j = pl.program_id(1)

    @pl.when(j == 0)
    def _():
        _norm_to_scratch(x_ref, g_ref, h_ref)
        h = h_ref[...]
        z = lax.dot_general(h, wf_ref[...].astype(BF16), NT_DIMS,
                            preferred_element_type=F32) + bf_ref[...]
        lf_ref[...] = _log_sigmoid(z)

    @pl.when(j < qk_tiles)
    def _():
        acc = jnp.dot(h_ref[...], wa_ref[...], preferred_element_type=F32)
        mult = jnp.where(j < q_tiles, FOX_SCALE * LOG2E, 1.0).astype(F32)
        for c in range(acc.shape[1] // FOX_HEAD_DIM):
            cs = slice(c * FOX_HEAD_DIM, (c + 1) * FOX_HEAD_DIM)
            xs = acc[:, cs]
            ms = jnp.mean(xs * xs, axis=-1, keepdims=True)
            o_ref[:, cs] = (xs * (lax.rsqrt(ms + RMS_EPS) * mult) * qkg_ref[:, cs]).astype(o_ref.dtype)

    @pl.when((j >= qk_tiles) & (j < a_tiles))
    def _():
        o_ref[...] = jnp.dot(h_ref[...], wa_ref[...], preferred_element_type=F32).astype(o_ref.dtype)

    @pl.when(j >= a_tiles)
    def _():
        o_ref[...] = jnp.dot(h_ref[...], wb_ref[...], preferred_element_type=F32).astype(o_ref.dtype)


def _fox_proj(x2, g, w_qkv, w_og, wf, bf, qk_gain, tm, tn):
    m = x2.shape[0]
    q_tiles = D_MODEL // tn
    qk_tiles = 2 * q_tiles
    a_tiles = w_qkv.shape[1] // tn
    return pl.pallas_call(
        functools.partial(_fox_proj_kernel, q_tiles=q_tiles, qk_tiles=qk_tiles, a_tiles=a_tiles),
        grid=(m // tm, FOX_MAIN // tn),
        in_specs=[
            pl.BlockSpec((tm, D_MODEL), lambda i, j: (i, 0)),
            pl.BlockSpec((1, D_MODEL), lambda i, j: (0, 0)),
            pl.BlockSpec((D_MODEL, tn), lambda i, j: (0, jnp.minimum(j, a_tiles - 1))),
            pl.BlockSpec((D_MODEL, tn), lambda i, j: (0, jnp.maximum(j - a_tiles, 0))),
            pl.BlockSpec((LANES, D_MODEL), lambda i, j: (0, 0)),
            pl.BlockSpec((1, LANES), lambda i, j: (0, 0)),
            pl.BlockSpec((1, tn), lambda i, j: (0, jnp.minimum(j, qk_tiles - 1))),
        ],
        out_specs=[
            pl.BlockSpec((tm, tn), lambda i, j: (i, j)),
            pl.BlockSpec((tm, LANES), lambda i, j: (i, 0)),
        ],
        out_shape=[
            jax.ShapeDtypeStruct((m, FOX_MAIN), BF16),
            jax.ShapeDtypeStruct((m, LANES), F32),
        ],
        scratch_shapes=[pltpu.VMEM((tm, D_MODEL), BF16)],
        compiler_params=_params("parallel", "arbitrary"),
        name="fox_proj",
    )(x2, g, w_qkv, w_og, wf, bf, qk_gain)


def _cumsum_kernel(lf_ref, c_ref, carry_ref):
    @pl.when(pl.program_id(1) == 0)
    def _():
        carry_ref[...] = jnp.zeros_like(carry_ref)

    t_len = lf_ref.shape[1]
    row = lax.broadcasted_iota(jnp.int32, (t_len, t_len), 0)
    col = lax.broadcasted_iota(jnp.int32, (t_len, t_len), 1)
    tril = (col <= row).astype(BF16)
    c = _cumsum_rows(tril, lf_ref[0]) + carry_ref[...]
    c_ref[0] = c * LOG2E
    carry_ref[...] = c[t_len - 1:t_len, :]


def _seq_cumsum(lf, t_len):
    b_sz, s_len, _ = lf.shape
    return pl.pallas_call(
        _cumsum_kernel,
        grid=(b_sz, s_len // t_len),
        in_specs=[pl.BlockSpec((1, t_len, LANES), lambda b, t: (b, t, 0))],
        out_specs=pl.BlockSpec((1, t_len, LANES), lambda b, t: (b, t, 0)),
        out_shape=jax.ShapeDtypeStruct(lf.shape, F32),
        scratch_shapes=[pltpu.VMEM((1, LANES), F32)],
        compiler_params=_params("parallel", "arbitrary"),
        name="fox_cumsum",
    )(lf)


def _fox_attn_kernel(q_ref, k_ref, v_ref, og_ref, ck_ref, o_ref, acc_ref, *, blk, n_sub, tk):
    i = pl.program_id(2)
    sub = blk // n_sub
    per_blk = blk // tk
    ck_own = jnp.concatenate(
        [ck_ref[0, 0, pl.ds(i * per_blk + t, 1), :] for t in range(per_blk)], axis=1)
    c0 = ck_own[:, 0:1]
    acc_ref[...] = jnp.zeros_like(acc_ref)

    def update(r, carry, ks, vs, bias, mask_offset):
        m_old, l_old = carry
        rows = slice(r * sub, (r + 1) * sub)
        s = lax.dot_general(q_ref[0, rows, :], ks, NT_DIMS, preferred_element_type=F32) + bias
        if mask_offset is not None:
            row = lax.broadcasted_iota(jnp.int32, s.shape, 0)
            col = lax.broadcasted_iota(jnp.int32, s.shape, 1)
            s = jnp.where(col <= row + mask_offset, s, -jnp.inf)
        m_new = jnp.maximum(m_old, jnp.max(s, axis=-1, keepdims=True))
        p = jnp.exp2(s - m_new)
        alpha = jnp.exp2(m_old - m_new)
        l_new = alpha * l_old + jnp.sum(p, axis=-1, keepdims=True)
        acc_ref[rows, :] = alpha * acc_ref[rows, :] + jnp.dot(
            p.astype(BF16), vs, preferred_element_type=F32)
        return m_new, l_new

    def body(j, carry):
        sl = pl.ds(pl.multiple_of(j * tk, tk), tk)
        ks = k_ref[0, sl, :]
        vs = v_ref[0, sl, :]
        bias = c0 - ck_ref[0, 0, pl.ds(j, 1), :]
        return tuple(update(r, carry[r], ks, vs, bias, None) for r in range(n_sub))

    init = tuple((jnp.full((sub, 1), -jnp.inf, F32), jnp.zeros((sub, 1), F32)) for _ in range(n_sub))
    carry = lax.fori_loop(0, i * per_blk, body, init)

    base = pl.multiple_of(i * blk, blk)
    for r in range(n_sub):
        width = (r + 1) * sub
        ks = k_ref[0, pl.ds(base, width), :]
        vs = v_ref[0, pl.ds(base, width), :]
        bias = c0 - ck_own[:, 0:width]
        m_fin, l_fin = update(r, carry[r], ks, vs, bias, r * sub)
        rows = slice(r * sub, (r + 1) * sub)
        gate = jax.nn.sigmoid(og_ref[0, rows, :].astype(F32))
        o_ref[0, rows, :] = (acc_ref[rows, :] / l_fin * gate).astype(o_ref.dtype)


def _fox_attn(qkvo, ck, b_sz, s_len, blk, n_sub):
    h_cols = D_MODEL // FOX_HEAD_DIM
    n_blk = s_len // blk
    n_kt, tk = ck.shape[2:]
    return pl.pallas_call(
        functools.partial(_fox_attn_kernel, blk=blk, n_sub=n_sub, tk=tk),
        grid=(b_sz, FOX_HEADS, n_blk),
        in_specs=[
            pl.BlockSpec((1, blk, FOX_HEAD_DIM), lambda b, h, i: (b, i, h)),
            pl.BlockSpec((1, s_len, FOX_HEAD_DIM), lambda b, h, i: (b, 0, h_cols + h)),
            pl.BlockSpec((1, s_len, FOX_HEAD_DIM), lambda b, h, i: (b, 0, 2 * h_cols + h)),
            pl.BlockSpec((1, blk, FOX_HEAD_DIM), lambda b, h, i: (b, i, 3 * h_cols + h)),
            pl.BlockSpec((1, 1, n_kt, tk), lambda b, h, i: (b, h, 0, 0)),
        ],
        out_specs=pl.BlockSpec((1, blk, FOX_HEAD_DIM), lambda b, h, i: (b, i, h)),
        out_shape=jax.ShapeDtypeStruct((b_sz, s_len, D_MODEL), BF16),
        scratch_shapes=[pltpu.VMEM((blk, FOX_HEAD_DIM), F32)],
        compiler_params=_params("parallel", "parallel", "arbitrary"),
        name="fox_attn",
    )(qkvo, qkvo, qkvo, qkvo, ck)


def _pad_rows(w, n):
    return jnp.pad(w, ((0, n - w.shape[0]), (0, 0)))


def kernel(x, norm_mix, norm_ffn, gla_w_in, gla_w_g2, gla_b_g2, gla_o_gain, gla_w_o,
           fox_w_in, fox_b_f, fox_q_gain, fox_k_gain, fox_w_o,
           ffn_w_gate, ffn_w_up, ffn_w_down):
    b_sz, s_len, d = x.shape
    assert d == D_MODEL and s_len % GLA_CHUNK == 0
    m = b_sz * s_len
    tm = min(1024, m)
    tn = 1024
    tm_ffn = min(1024, m)
    tf = 256
    gla_t = min(512, s_len)
    attn_blk = min(1024, s_len)
    attn_sub = min(512, s_len)
    attn_tk = min(512, s_len)
    cum_t = min(256, s_len)

    x2 = x.reshape(m, d)

    def ffn_layer(xin, layer):
        return _ffn(xin, norm_ffn[layer][None, :], ffn_w_gate, ffn_w_up, ffn_w_down, layer, tm_ffn, tf)

    gla_wt = jnp.swapaxes(gla_w_in, 1, 2)
    qkvr, la = _gla_proj(
        x2, norm_mix[0][None, :],
        _cast_weight_t(gla_wt, 0, GLA_MAIN),
        _pad_rows(gla_wt[0, GLA_MAIN:, :], LANES),
        _pad_rows(gla_w_g2[0], LANES),
        gla_b_g2[0][None, :], tm, tn)
    og = _gla_mix(qkvr.reshape(b_sz, s_len, GLA_MAIN), la.reshape(b_sz, s_len, GLA_KEY_DIM),
                  gla_o_gain[0][None, :], b_sz, s_len, gla_t, 4)
    x2 = _residual_matmul(x2, og.reshape(m, GLA_VAL_DIM), _cast_weight(gla_w_o, 0, D_MODEL), tm, tn)
    x2 = ffn_layer(x2, 0)

    qk_gain = jnp.concatenate([jnp.tile(fox_q_gain[0], FOX_HEADS), jnp.tile(fox_k_gain[0], FOX_HEADS)])
    f_lo, f_hi = 3 * D_MODEL, 3 * D_MODEL + FOX_HEADS
    fox_wt = jnp.swapaxes(fox_w_in, 1, 2)
    qkvo, lf = _fox_proj(
        x2, norm_mix[1][None, :],
        _cast_weight_t(fox_wt, 0, f_lo),
        _cast_weight_t(fox_wt[:, f_hi:, :], 0, D_MODEL),
        _pad_rows(fox_wt[0, f_lo:f_hi, :], LANES),
        jnp.pad(fox_b_f[0], (0, LANES - FOX_HEADS))[None, :],
        qk_gain[None, :], tm, tn)
    c = _seq_cumsum(lf.reshape(b_sz, s_len, LANES), cum_t)
    c_hs = c[:, :, :FOX_HEADS].transpose(0, 2, 1)
    ck = c_hs.reshape(b_sz, FOX_HEADS, s_len // attn_tk, attn_tk)
    o = _fox_attn(qkvo.reshape(b_sz, s_len, FOX_MAIN), ck, b_sz, s_len, attn_blk,
                  attn_blk // attn_sub)
    x2 = _residual_matmul(x2, o.reshape(m, D_MODEL), _cast_weight(fox_w_o, 0, D_MODEL), tm, tn)
    x2 = ffn_layer(x2, 1)
    return x2.reshape(b_sz, s_len, d)
```

```python
import functools

import jax
import jax.numpy as jnp
from jax import lax
from jax.experimental import pallas as pl
from jax.experimental.pallas import tpu as pltpu

F32 = jnp.float32
BF16 = jnp.bfloat16

D_MODEL = 2048
RMS_EPS = 1e-6

GLA_HEADS = 4
GLA_KEY_DIM = D_MODEL // 2
GLA_VAL_DIM = D_MODEL
GLA_HEAD_K = GLA_KEY_DIM // GLA_HEADS
GLA_HEAD_V = GLA_VAL_DIM // GLA_HEADS
GLA_GATE_RANK = 16
GLA_GATE_TEMP = 16.0
GLA_CHUNK = 64
GLA_MAIN = 2 * GLA_KEY_DIM + 2 * GLA_VAL_DIM
GLA_DIAG = 8

FOX_HEAD_DIM = 128
FOX_HEADS = D_MODEL // FOX_HEAD_DIM
FOX_MAIN = 4 * D_MODEL
FOX_SCALE = FOX_HEAD_DIM ** -0.5
LOG2E = 1.4426950408889634
BF16_SLOP = 1.02
F32_EXP2_FLOOR = 160.0

D_FF = ((8 * D_MODEL + 2) // 3 + 255) // 256 * 256

LANES = 128
NORM_SLAB = 256
CAST_ROWS = 256
CAST_COLS = 512
VMEM_LIMIT = 56 * 1024 * 1024

NT_DIMS = (((1,), (1,)), ((), ()))
TN_DIMS = (((0,), (0,)), ((), ()))


def _params(*sem, flags=None):
    return pltpu.CompilerParams(dimension_semantics=sem, vmem_limit_bytes=VMEM_LIMIT, flags=flags)


def _rmsnorm_rows(x, g):
    ms = jnp.mean(x * x, axis=-1, keepdims=True)
    return x * lax.rsqrt(ms + RMS_EPS) * g


def _log_sigmoid(z):
    return jnp.minimum(z, 0.0) - jnp.log1p(jnp.exp(-jnp.abs(z)))


def _norm_to_scratch(x_ref, g_ref, h_ref):
    rows = x_ref.shape[0]
    slab = min(NORM_SLAB, rows)
    g = g_ref[...]
    for r0 in range(0, rows, slab):
        h_ref[r0:r0 + slab, :] = _rmsnorm_rows(x_ref[r0:r0 + slab, :], g).astype(BF16)


def _split3(a):
    hi = a.astype(BF16)
    r1 = a - hi.astype(F32)
    mid = r1.astype(BF16)
    lo = (r1 - mid.astype(F32)).astype(BF16)
    return hi, mid, lo


def _cumsum_rows(tril, a):
    hi, mid, lo = _split3(a)
    out = jnp.dot(tril, hi, preferred_element_type=F32)
    out += jnp.dot(tril, mid, preferred_element_type=F32)
    out += jnp.dot(tril, lo, preferred_element_type=F32)
    return out


def _cast_kernel(w_ref, o_ref):
    o_ref[...] = w_ref[0].astype(BF16)


def _cast_weight(w3, layer, n_cols):
    _, k_dim, _ = w3.shape
    tr = CAST_ROWS
    assert k_dim % tr == 0 and n_cols % LANES == 0
    return pl.pallas_call(
        _cast_kernel,
        grid=(k_dim // tr,),
        in_specs=[pl.BlockSpec((1, tr, n_cols), lambda i: (layer, i, 0))],
        out_specs=pl.BlockSpec((tr, n_cols), lambda i: (i, 0)),
        out_shape=jax.ShapeDtypeStruct((k_dim, n_cols), BF16),
        compiler_params=_params("parallel"),
        name="cast_weight",
    )(w3)


def _cast_t_kernel(wt_ref, o_ref):
    o_ref[...] = wt_ref[0].T.astype(BF16)


def _cast_weight_t(wt3, layer, n_cols):
    _, _, k_dim = wt3.shape
    tc = CAST_COLS
    assert n_cols % tc == 0
    return pl.pallas_call(
        _cast_t_kernel,
        grid=(n_cols // tc,),
        in_specs=[pl.BlockSpec((1, tc, k_dim), lambda i: (layer, i, 0))],
        out_specs=pl.BlockSpec((k_dim, tc), lambda i: (0, i)),
        out_shape=jax.ShapeDtypeStruct((k_dim, n_cols), BF16),
        compiler_params=_params("parallel"),
        name="cast_weight_t",
    )(wt3)


def _gla_proj_kernel(x_ref, g_ref, w_ref, wg1_ref, wg2_ref, bg_ref, o_ref, la_ref, h_ref):
    @pl.when(pl.program_id(1) == 0)
    def _():
        _norm_to_scratch(x_ref, g_ref, h_ref)
        h = h_ref[...]
        g1 = lax.dot_general(h, wg1_ref[...].astype(BF16), NT_DIMS, preferred_element_type=F32)
        z = jnp.dot(g1.astype(BF16), wg2_ref[...].astype(BF16),
                    preferred_element_type=F32) + bg_ref[...]
        la_ref[...] = _log_sigmoid(z) * (LOG2E / GLA_GATE_TEMP)

    o_ref[...] = jnp.dot(h_ref[...], w_ref[...], preferred_element_type=F32).astype(o_ref.dtype)


def _gla_proj(x2, g, w_main, wg1, wg2, bg, tm, tn):
    m = x2.shape[0]
    return pl.pallas_call(
        _gla_proj_kernel,
        grid=(m // tm, GLA_MAIN // tn),
        in_specs=[
            pl.BlockSpec((tm, D_MODEL), lambda i, j: (i, 0)),
            pl.BlockSpec((1, D_MODEL), lambda i, j: (0, 0)),
            pl.BlockSpec((D_MODEL, tn), lambda i, j: (0, j)),
            pl.BlockSpec((LANES, D_MODEL), lambda i, j: (0, 0)),
            pl.BlockSpec((LANES, GLA_KEY_DIM), lambda i, j: (0, 0)),
            pl.BlockSpec((1, GLA_KEY_DIM), lambda i, j: (0, 0)),
        ],
        out_specs=[
            pl.BlockSpec((tm, tn), lambda i, j: (i, j)),
            pl.BlockSpec((tm, GLA_KEY_DIM), lambda i, j: (i, 0)),
        ],
        out_shape=[
            jax.ShapeDtypeStruct((m, GLA_MAIN), BF16),
            jax.ShapeDtypeStruct((m, GLA_KEY_DIM), F32),
        ],
        scratch_shapes=[pltpu.VMEM((tm, D_MODEL), BF16)],
        compiler_params=_params("parallel", "arbitrary"),
        name="gla_proj",
    )(x2, g, w_main, wg1, wg2, bg)


def _gla_kernel(q_ref, k_ref, v_ref, r_ref, la_ref, gain_ref, o_ref, st_ref, *, n_chunks, heads):
    c_len = GLA_CHUNK

    @pl.when(pl.program_id(2) == 0)
    def _():
        st_ref[...] = jnp.zeros_like(st_ref)

    row = lax.broadcasted_iota(jnp.int32, (c_len, c_len), 0)
    col = lax.broadcasted_iota(jnp.int32, (c_len, c_len), 1)
    tril = (col <= row).astype(BF16)
    delta = row - col
    band = jnp.where((delta >= 0) & (delta <= row % GLA_DIAG), delta, -1)
    level_masks = []
    block = 2 * GLA_DIAG
    while block <= c_len:
        half = block // 2
        mk = (row // block == col // block) & (row % block >= half) & (col % block < half)
        level_masks.append((block, mk))
        block *= 2
    gain = gain_ref[...]
    scale = GLA_HEAD_K ** -0.5

    def head_chunk(hd, sl):
        ksl = slice(hd * GLA_HEAD_K, (hd + 1) * GLA_HEAD_K)
        vsl = slice(hd * GLA_HEAD_V, (hd + 1) * GLA_HEAD_V)
        la = la_ref[0, sl, ksl]
        b = _cumsum_rows(tril, la)
        b_last = b[c_len - 1:c_len, :]
        qf = q_ref[0, sl, ksl].astype(F32) * scale
        kf = k_ref[0, sl, ksl].astype(F32)
        v = v_ref[0, sl, vsl]

        st = st_ref[hd]
        qi = (qf * jnp.exp2(b)).astype(BF16)
        o = lax.dot_general(qi, st.astype(BF16), NT_DIMS, preferred_element_type=F32)
        kd = (kf * jnp.exp2(b_last - b)).astype(BF16)
        st_ref[hd] = st * jnp.exp2(b_last) + lax.dot_general(
            v, kd, TN_DIMS, preferred_element_type=F32)

        attn = jnp.zeros((c_len, c_len), F32)
        for blk, mk in level_masks:
            half = blk // 2
            ref = jnp.concatenate(
                [jnp.broadcast_to(b[s + half - 1:s + half, :], (blk, GLA_HEAD_K))
                 for s in range(0, c_len, blk)], axis=0)
            ql = (qf * jnp.exp2(b - ref)).astype(BF16)
            kl = (kf * jnp.exp2(ref - b)).astype(BF16)
            attn = jnp.where(mk, lax.dot_general(ql, kl, NT_DIMS, preferred_element_type=F32), attn)
        for d in range(GLA_DIAG):
            k_d = pltpu.roll(kf, d, 0) if d else kf
            b_d = pltpu.roll(b, d, 0) if d else b
            diag = jnp.sum(qf * k_d * jnp.exp2(b - b_d), axis=-1, keepdims=True)
            attn = jnp.where(band == d, diag, attn)

        o += jnp.dot(attn.astype(BF16), v, preferred_element_type=F32)
        on = _rmsnorm_rows(o, gain)
        r = r_ref[0, sl, vsl].astype(F32)
        o_ref[0, sl, vsl] = (on * (r * jax.nn.sigmoid(r))).astype(o_ref.dtype)

    def chunk(c, carry):
        sl = pl.ds(pl.multiple_of(c * c_len, c_len), c_len)
        for hd in range(heads):
            head_chunk(hd, sl)
        return carry

    lax.fori_loop(0, n_chunks, chunk, 0)


def _gla_mix(qkvr, la, gain, b_sz, s_len, t_len, heads):
    n_chunks = t_len // GLA_CHUNK
    wk = heads * GLA_HEAD_K
    wv = heads * GLA_HEAD_V
    kq = GLA_KEY_DIM // wk
    kv = 2 * GLA_KEY_DIM // wv
    kr = kv + GLA_VAL_DIM // wv
    return pl.pallas_call(
        functools.partial(_gla_kernel, n_chunks=n_chunks, heads=heads),
        grid=(b_sz, GLA_HEADS // heads, s_len // t_len),
        in_specs=[
            pl.BlockSpec((1, t_len, wk), lambda b, h, t: (b, t, h)),
            pl.BlockSpec((1, t_len, wk), lambda b, h, t: (b, t, kq + h)),
            pl.BlockSpec((1, t_len, wv), lambda b, h, t: (b, t, kv + h)),
            pl.BlockSpec((1, t_len, wv), lambda b, h, t: (b, t, kr + h)),
            pl.BlockSpec((1, t_len, wk), lambda b, h, t: (b, t, h)),
            pl.BlockSpec((1, GLA_HEAD_V), lambda b, h, t: (0, 0)),
        ],
        out_specs=pl.BlockSpec((1, t_len, wv), lambda b, h, t: (b, t, h)),
        out_shape=jax.ShapeDtypeStruct((b_sz, s_len, GLA_VAL_DIM), BF16),
        scratch_shapes=[pltpu.VMEM((heads, GLA_HEAD_V, GLA_HEAD_K), F32)],
        compiler_params=_params("parallel", "parallel", "arbitrary"),
        name="gla_mix",
    )(qkvr, qkvr, qkvr, qkvr, la, gain)


def _residual_matmul_kernel(x_ref, a_ref, w_ref, o_ref):
    o_ref[...] = x_ref[...] + jnp.dot(a_ref[...], w_ref[...], preferred_element_type=F32)


def _residual_matmul(x2, a, w, tm, tn):
    m, k_dim = a.shape
    n = w.shape[1]
    return pl.pallas_call(
        _residual_matmul_kernel,
        grid=(m // tm, n // tn),
        in_specs=[
            pl.BlockSpec((tm, tn), lambda i, j: (i, j)),
            pl.BlockSpec((tm, k_dim), lambda i, j: (i, 0)),
            pl.BlockSpec((k_dim, tn), lambda i, j: (0, j)),
        ],
        out_specs=pl.BlockSpec((tm, tn), lambda i, j: (i, j)),
        out_shape=jax.ShapeDtypeStruct((m, n), F32),
        compiler_params=_params("parallel", "arbitrary"),
        name="residual_matmul",
    )(x2, a, w)


def _ffn_kernel(x_ref, g_ref, wg_ref, wu_ref, wd_ref, o_ref, h_ref):
    @pl.when(pl.program_id(1) == 0)
    def _():
        _norm_to_scratch(x_ref, g_ref, h_ref)
        o_ref[...] = x_ref[...]

    h = h_ref[...]
    gate = jnp.dot(h, wg_ref[0].astype(BF16), preferred_element_type=F32)
    up = jnp.dot(h, wu_ref[0].astype(BF16), preferred_element_type=F32)
    act = (gate * jax.nn.sigmoid(gate) * up).astype(BF16)
    o_ref[...] += jnp.dot(act, wd_ref[0].astype(BF16), preferred_element_type=F32)


def _ffn(x2, g, wg, wu, wd, layer, tm, tf):
    m = x2.shape[0]
    return pl.pallas_call(
        _ffn_kernel,
        grid=(m // tm, D_FF // tf),
        in_specs=[
            pl.BlockSpec((tm, D_MODEL), lambda i, f: (i, 0)),
            pl.BlockSpec((1, D_MODEL), lambda i, f: (0, 0)),
            pl.BlockSpec((1, D_MODEL, tf), lambda i, f: (layer, 0, f)),
            pl.BlockSpec((1, D_MODEL, tf), lambda i, f: (layer, 0, f)),
            pl.BlockSpec((1, tf, D_MODEL), lambda i, f: (layer, f, 0)),
        ],
        out_specs=pl.BlockSpec((tm, D_MODEL), lambda i, f: (i, 0)),
        out_shape=jax.ShapeDtypeStruct((m, D_MODEL), F32),
        scratch_shapes=[pltpu.VMEM((tm, D_MODEL), BF16)],
        compiler_params=_params("parallel", "arbitrary"),
        name="ffn",
    )(x2, g, wg, wu, wd)


def _fox_proj_kernel(x_ref, g_ref, wa_ref, wb_ref, wf_ref, bf_ref, qkg_ref, o_ref, lf_ref, h_ref,
                     *, q_tiles, qk_tiles, a_tiles):
    j = pl.program_id(1)

    @pl.when(j == 0)
    def _():
        _norm_to_scratch(x_ref, g_ref, h_ref)
        h = h_ref[...]
        z = lax.dot_general(h, wf_ref[...].astype(BF16), NT_DIMS,
                            preferred_element_type=F32) + bf_ref[...]
        lf_ref[...] = _log_sigmoid(z)

    @pl.when(j < qk_tiles)
    def _():
        acc = jnp.dot(h_ref[...], wa_ref[...], preferred_element_type=F32)
        mult = jnp.where(j < q_tiles, FOX_SCALE * LOG2E, 1.0).astype(F32)
        for c in range(acc.shape[1] // FOX_HEAD_DIM):
            cs = slice(c * FOX_HEAD_DIM, (c + 1) * FOX_HEAD_DIM)
            xs = acc[:, cs]
            ms = jnp.mean(xs * xs, axis=-1, keepdims=True)
            o_ref[:, cs] = (xs * (lax.rsqrt(ms + RMS_EPS) * mult) * qkg_ref[:, cs]).astype(o_ref.dtype)

    @pl.when((j >= qk_tiles) & (j < a_tiles))
    def _():
        o_ref[...] = jnp.dot(h_ref[...], wa_ref[...], preferred_element_type=F32).astype(o_ref.dtype)

    @pl.when(j >= a_tiles)
    def _():
        o_ref[...] = jnp.dot(h_ref[...], wb_ref[...], preferred_element_type=F32).astype(o_ref.dtype)


def _fox_proj(x2, g, w_qkv, w_og, wf, bf, qk_gain, tm, tn):
    m = x2.shape[0]
    q_tiles = D_MODEL // tn
    qk_tiles = 2 * q_tiles
    a_tiles = w_qkv.shape[1] // tn
    return pl.pallas_call(
        functools.partial(_fox_proj_kernel, q_tiles=q_tiles, qk_tiles=qk_tiles, a_tiles=a_tiles),
        grid=(m // tm, FOX_MAIN // tn),
        in_specs=[
            pl.BlockSpec((tm, D_MODEL), lambda i, j: (i, 0)),
            pl.BlockSpec((1, D_MODEL), lambda i, j: (0, 0)),
            pl.BlockSpec((D_MODEL, tn), lambda i, j: (0, jnp.minimum(j, a_tiles - 1))),
            pl.BlockSpec((D_MODEL, tn), lambda i, j: (0, jnp.maximum(j - a_tiles, 0))),
            pl.BlockSpec((LANES, D_MODEL), lambda i, j: (0, 0)),
            pl.BlockSpec((1, LANES), lambda i, j: (0, 0)),
            pl.BlockSpec((1, tn), lambda i, j: (0, jnp.minimum(j, qk_tiles - 1))),
        ],
        out_specs=[
            pl.BlockSpec((tm, tn), lambda i, j: (i, j)),
            pl.BlockSpec((tm, LANES), lambda i, j: (i, 0)),
        ],
        out_shape=[
            jax.ShapeDtypeStruct((m, FOX_MAIN), BF16),
            jax.ShapeDtypeStruct((m, LANES), F32),
        ],
        scratch_shapes=[pltpu.VMEM((tm, D_MODEL), BF16)],
        compiler_params=_params("parallel", "arbitrary"),
        name="fox_proj",
    )(x2, g, w_qkv, w_og, wf, bf, qk_gain)


def _cumsum_kernel(lf_ref, c_ref, carry_ref):
    @pl.when(pl.program_id(1) == 0)
    def _():
        carry_ref[...] = jnp.zeros_like(carry_ref)

    t_len = lf_ref.shape[1]
    row = lax.broadcasted_iota(jnp.int32, (t_len, t_len), 0)
    col = lax.broadcasted_iota(jnp.int32, (t_len, t_len), 1)
    tril = (col <= row).astype(BF16)
    c = _cumsum_rows(tril, lf_ref[0]) + carry_ref[...]
    c_ref[0] = c * LOG2E
    carry_ref[...] = c[t_len - 1:t_len, :]


def _seq_cumsum(lf, t_len):
    b_sz, s_len, _ = lf.shape
    return pl.pallas_call(
        _cumsum_kernel,
        grid=(b_sz, s_len // t_len),
        in_specs=[pl.BlockSpec((1, t_len, LANES), lambda b, t: (b, t, 0))],
        out_specs=pl.BlockSpec((1, t_len, LANES), lambda b, t: (b, t, 0)),
        out_shape=jax.ShapeDtypeStruct(lf.shape, F32),
        scratch_shapes=[pltpu.VMEM((1, LANES), F32)],
        compiler_params=_params("parallel", "arbitrary"),
        name="fox_cumsum",
    )(lf)


def _fox_attn_kernel(ends_ref, firsts_ref, slack_ref, q_ref, k_ref, v_ref, og_ref, ck_ref, o_ref,
                     acc_ref, *, blk, n_sub, tk):
    i = pl.program_id(2)
    sub = blk // n_sub
    per_blk = blk // tk
    n_kt = ck_ref.shape[2]
    bh = pl.program_id(0) * pl.num_programs(1) + pl.program_id(1)
    n_prev = i * per_blk
    limit = firsts_ref[bh * pl.num_programs(2) + i] + slack_ref[0]

    def first_needed(j, lo):
        return jnp.minimum(lo, jnp.where(ends_ref[bh * n_kt + j] > limit, n_prev, j))

    j_start = lax.fori_loop(0, n_prev, first_needed, n_prev)

    ck_own = jnp.concatenate(
        [ck_ref[0, 0, pl.ds(i * per_blk + t, 1), :] for t in range(per_blk)], axis=1)
    c0 = ck_own[:, 0:1]
    acc_ref[...] = jnp.zeros_like(acc_ref)

    def update(r, carry, ks, vs, bias, mask_offset):
        m_old, l_old = carry
        rows = slice(r * sub, (r + 1) * sub)
        s = lax.dot_general(q_ref[0, rows, :], ks, NT_DIMS, preferred_element_type=F32) + bias
        if mask_offset is not None:
            row = lax.broadcasted_iota(jnp.int32, s.shape, 0)
            col = lax.broadcasted_iota(jnp.int32, s.shape, 1)
            s = jnp.where(col <= row + mask_offset, s, -jnp.inf)
        m_new = jnp.maximum(m_old, jnp.max(s, axis=-1, keepdims=True))
        p = jnp.exp2(s - m_new)
        alpha = jnp.exp2(m_old - m_new)
        l_new = alpha * l_old + jnp.sum(p, axis=-1, keepdims=True)
        acc_ref[rows, :] = alpha * acc_ref[rows, :] + jnp.dot(
            p.astype(BF16), vs, preferred_element_type=F32)
        return m_new, l_new

    def body(j, carry):
        sl = pl.ds(pl.multiple_of(j * tk, tk), tk)
        ks = k_ref[0, sl, :]
        vs = v_ref[0, sl, :]
        bias = c0 - ck_ref[0, 0, pl.ds(j, 1), :]
        return tuple(update(r, carry[r], ks, vs, bias, None) for r in range(n_sub))

    init = tuple((jnp.full((sub, 1), -jnp.inf, F32), jnp.zeros((sub, 1), F32)) for _ in range(n_sub))
    carry = lax.fori_loop(j_start, n_prev, body, init)

    base = pl.multiple_of(i * blk, blk)
    for r in range(n_sub):
        width = (r + 1) * sub
        ks = k_ref[0, pl.ds(base, width), :]
        vs = v_ref[0, pl.ds(base, width), :]
        bias = c0 - ck_own[:, 0:width]
        m_fin, l_fin = update(r, carry[r], ks, vs, bias, r * sub)
        rows = slice(r * sub, (r + 1) * sub)
        gate = jax.nn.sigmoid(og_ref[0, rows, :].astype(F32))
        o_ref[0, rows, :] = (acc_ref[rows, :] / l_fin * gate).astype(o_ref.dtype)


def _fox_attn(qkvo, ck, slack, b_sz, s_len, blk, n_sub):
    h_cols = D_MODEL // FOX_HEAD_DIM
    n_blk = s_len // blk
    n_kt, tk = ck.shape[2:]
    ends = ck[:, :, :, tk - 1].reshape(-1)
    firsts = ck.reshape(b_sz, FOX_HEADS, n_blk, blk)[:, :, :, 0].reshape(-1)
    grid_spec = pltpu.PrefetchScalarGridSpec(
        num_scalar_prefetch=3,
        grid=(b_sz, FOX_HEADS, n_blk),
        in_specs=[
            pl.BlockSpec((1, blk, FOX_HEAD_DIM), lambda b, h, i, *_: (b, i, h)),
            pl.BlockSpec((1, s_len, FOX_HEAD_DIM), lambda b, h, i, *_: (b, 0, h_cols + h)),
            pl.BlockSpec((1, s_len, FOX_HEAD_DIM), lambda b, h, i, *_: (b, 0, 2 * h_cols + h)),
            pl.BlockSpec((1, blk, FOX_HEAD_DIM), lambda b, h, i, *_: (b, i, 3 * h_cols + h)),
            pl.BlockSpec((1, 1, n_kt, tk), lambda b, h, i, *_: (b, h, 0, 0)),
        ],
        out_specs=pl.BlockSpec((1, blk, FOX_HEAD_DIM), lambda b, h, i, *_: (b, i, h)),
        scratch_shapes=[pltpu.VMEM((blk, FOX_HEAD_DIM), F32)],
    )
    return pl.pallas_call(
        functools.partial(_fox_attn_kernel, blk=blk, n_sub=n_sub, tk=tk),
        grid_spec=grid_spec,
        out_shape=jax.ShapeDtypeStruct((b_sz, s_len, D_MODEL), BF16),
        compiler_params=_params("parallel", "parallel", "arbitrary"),
        name="fox_attn",
    )(ends, firsts, slack, qkvo, qkvo, qkvo, qkvo, ck)


def _pad_rows(w, n):
    return jnp.pad(w, ((0, n - w.shape[0]), (0, 0)))


def kernel(x, norm_mix, norm_ffn, gla_w_in, gla_w_g2, gla_b_g2, gla_o_gain, gla_w_o,
           fox_w_in, fox_b_f, fox_q_gain, fox_k_gain, fox_w_o,
           ffn_w_gate, ffn_w_up, ffn_w_down):
    b_sz, s_len, d = x.shape
    assert d == D_MODEL and s_len % GLA_CHUNK == 0
    m = b_sz * s_len
    tm = min(1024, m)
    tn = 1024
    tm_ffn = min(1024, m)
    tf = 256
    gla_t = min(512, s_len)
    attn_blk = min(512, s_len)
    attn_sub = min(512, s_len)
    attn_tk = min(512, s_len)
    cum_t = min(256, s_len)

    x2 = x.reshape(m, d)

    def ffn_layer(xin, layer):
        return _ffn(xin, norm_ffn[layer][None, :], ffn_w_gate, ffn_w_up, ffn_w_down, layer, tm_ffn, tf)

    gla_wt = jnp.swapaxes(gla_w_in, 1, 2)
    qkvr, la = _gla_proj(
        x2, norm_mix[0][None, :],
        _cast_weight_t(gla_wt, 0, GLA_MAIN),
        _pad_rows(gla_wt[0, GLA_MAIN:, :], LANES),
        _pad_rows(gla_w_g2[0], LANES),
        gla_b_g2[0][None, :], tm, tn)
    og = _gla_mix(qkvr.reshape(b_sz, s_len, GLA_MAIN), la.reshape(b_sz, s_len, GLA_KEY_DIM),
                  gla_o_gain[0][None, :], b_sz, s_len, gla_t, 4)
    x2 = _residual_matmul(x2, og.reshape(m, GLA_VAL_DIM), _cast_weight(gla_w_o, 0, D_MODEL), tm, tn)
    x2 = ffn_layer(x2, 0)

    qk_gain = jnp.concatenate([jnp.tile(fox_q_gain[0], FOX_HEADS), jnp.tile(fox_k_gain[0], FOX_HEADS)])
    f_lo, f_hi = 3 * D_MODEL, 3 * D_MODEL + FOX_HEADS
    fox_wt = jnp.swapaxes(fox_w_in, 1, 2)
    qkvo, lf = _fox_proj(
        x2, norm_mix[1][None, :],
        _cast_weight_t(fox_wt, 0, f_lo),
        _cast_weight_t(fox_wt[:, f_hi:, :], 0, D_MODEL),
        _pad_rows(fox_wt[0, f_lo:f_hi, :], LANES),
        jnp.pad(fox_b_f[0], (0, LANES - FOX_HEADS))[None, :],
        qk_gain[None, :], tm, tn)
    c = _seq_cumsum(lf.reshape(b_sz, s_len, LANES), cum_t)
    c_hs = c[:, :, :FOX_HEADS].transpose(0, 2, 1)
    ck = c_hs.reshape(b_sz, FOX_HEADS, s_len // attn_tk, attn_tk)
    qk_max = (FOX_HEAD_DIM * FOX_SCALE * LOG2E * BF16_SLOP
              * jnp.max(jnp.abs(fox_q_gain[0])) * jnp.max(jnp.abs(fox_k_gain[0])))
    slack = (2.0 * qk_max + F32_EXP2_FLOOR).reshape(1)
    o = _fox_attn(qkvo.reshape(b_sz, s_len, FOX_MAIN), ck, slack, b_sz, s_len, attn_blk,
                  attn_blk // attn_sub)
    x2 = _residual_matmul(x2, o.reshape(m, D_MODEL), _cast_weight(fox_w_o, 0, D_MODEL), tm, tn)
    x2 = ffn_layer(x2, 1)
    return x2.reshape(b_sz, s_len, d)
```

```python
import functools

import jax
import jax.numpy as jnp
from jax import lax
from jax.experimental import pallas as pl
from jax.experimental.pallas import tpu as pltpu

F32 = jnp.float32
BF16 = jnp.bfloat16

D_MODEL = 2048
RMS_EPS = 1e-6

GLA_HEADS = 4
GLA_KEY_DIM = D_MODEL // 2
GLA_VAL_DIM = D_MODEL
GLA_HEAD_K = GLA_KEY_DIM // GLA_HEADS
GLA_HEAD_V = GLA_VAL_DIM // GLA_HEADS
GLA_GATE_RANK = 16
GLA_GATE_TEMP = 16.0
GLA_CHUNK = 64
GLA_MAIN = 2 * GLA_KEY_DIM + 2 * GLA_VAL_DIM
GLA_COARSE_LEVELS = (64, 32)
GLA_FINE_LEVELS = (16, 8)
GLA_DIAG = 4

FOX_HEAD_DIM = 128
FOX_HEADS = D_MODEL // FOX_HEAD_DIM
FOX_MAIN = 4 * D_MODEL
FOX_SCALE = FOX_HEAD_DIM ** -0.5
LOG2E = 1.4426950408889634
BF16_SLOP = 1.02
F32_EXP2_FLOOR = 160.0

D_FF = ((8 * D_MODEL + 2) // 3 + 255) // 256 * 256

LANES = 128
SUBLANES = 8
NORM_SLAB = 256
CAST_COLS = 512
VMEM_LIMIT = 58 * 1024 * 1024

_DONE = object()


def _lockstep(chains):
    pending = list(chains)
    while pending:
        pending = [g for g in pending if next(g, _DONE) is not _DONE]


NT_DIMS = (((1,), (1,)), ((), ()))
TN_DIMS = (((0,), (0,)), ((), ()))


def _params(*sem, flags=None):
    return pltpu.CompilerParams(dimension_semantics=sem, vmem_limit_bytes=VMEM_LIMIT, flags=flags)


def _rmsnorm_rows(x, g):
    ms = jnp.mean(x * x, axis=-1, keepdims=True)
    return x * lax.rsqrt(ms + RMS_EPS) * g


def _log_sigmoid(z):
    return jnp.minimum(z, 0.0) - jnp.log1p(jnp.exp(-jnp.abs(z)))


def _norm_to_scratch(x_ref, g_ref, h_ref):
    rows = x_ref.shape[0]
    slab = min(NORM_SLAB, rows)
    g = g_ref[...]
    for r0 in range(0, rows, slab):
        h_ref[r0:r0 + slab, :] = _rmsnorm_rows(x_ref[r0:r0 + slab, :], g).astype(BF16)


def _split3(a):
    hi = a.astype(BF16)
    r1 = a - hi.astype(F32)
    mid = r1.astype(BF16)
    lo = (r1 - mid.astype(F32)).astype(BF16)
    return hi, mid, lo


def _cumsum_rows(tril, a):
    hi, mid, lo = _split3(a)
    out = jnp.dot(tril, hi, preferred_element_type=F32)
    out += jnp.dot(tril, mid, preferred_element_type=F32)
    out += jnp.dot(tril, lo, preferred_element_type=F32)
    return out


def _cast_t_kernel(wt_ref, o_ref):
    o_ref[...] = wt_ref[0].T.astype(BF16)


def _cast_weight_t(wt3, layer, n_cols):
    _, _, k_dim = wt3.shape
    tc = CAST_COLS
    assert n_cols % tc == 0
    return pl.pallas_call(
        _cast_t_kernel,
        grid=(n_cols // tc,),
        in_specs=[pl.BlockSpec((1, tc, k_dim), lambda i: (layer, i, 0))],
        out_specs=pl.BlockSpec((k_dim, tc), lambda i: (0, i)),
        out_shape=jax.ShapeDtypeStruct((k_dim, n_cols), BF16),
        compiler_params=_params("parallel"),
        name="cast_weight_t",
    )(wt3)


def _gla_proj_kernel(x_ref, g_ref, w_ref, wg1_ref, wg2_ref, bg_ref, o_ref, la_ref, h_ref):
    @pl.when(pl.program_id(1) == 0)
    def _():
        _norm_to_scratch(x_ref, g_ref, h_ref)
        h = h_ref[...]
        g1 = lax.dot_general(h, wg1_ref[...].astype(BF16), NT_DIMS, preferred_element_type=F32)
        z = jnp.dot(g1.astype(BF16), wg2_ref[...].astype(BF16),
                    preferred_element_type=F32) + bg_ref[...]
        la_ref[...] = _log_sigmoid(z) * (LOG2E / GLA_GATE_TEMP)

    o_ref[...] = lax.dot_general(h_ref[...], w_ref[0].astype(BF16), NT_DIMS,
                                 preferred_element_type=F32).astype(o_ref.dtype)


def _gla_proj(x2, g, wt3, wg1, wg2, bg, tm, tn):
    m = x2.shape[0]
    return pl.pallas_call(
        _gla_proj_kernel,
        grid=(m // tm, GLA_MAIN // tn),
        in_specs=[
            pl.BlockSpec((tm, D_MODEL), lambda i, j: (i, 0)),
            pl.BlockSpec((1, D_MODEL), lambda i, j: (0, 0)),
            pl.BlockSpec((1, tn, D_MODEL), lambda i, j: (0, j, 0)),
            pl.BlockSpec((LANES, D_MODEL), lambda i, j: (0, 0)),
            pl.BlockSpec((LANES, GLA_KEY_DIM), lambda i, j: (0, 0)),
            pl.BlockSpec((1, GLA_KEY_DIM), lambda i, j: (0, 0)),
        ],
        out_specs=[
            pl.BlockSpec((tm, tn), lambda i, j: (i, j)),
            pl.BlockSpec((tm, GLA_KEY_DIM), lambda i, j: (i, 0)),
        ],
        out_shape=[
            jax.ShapeDtypeStruct((m, GLA_MAIN), BF16),
            jax.ShapeDtypeStruct((m, GLA_KEY_DIM), F32),
        ],
        scratch_shapes=[pltpu.VMEM((tm, D_MODEL), BF16)],
        compiler_params=_params("parallel", "arbitrary"),
        name="gla_proj",
    )(x2, g, wt3, wg1, wg2, bg)


def _gla_kernel(q_ref, k_ref, v_ref, r_ref, la_ref, gain_ref, o_ref, st_ref, *, n_chunks, heads):
    c_len = GLA_CHUNK

    @pl.when(pl.program_id(2) == 0)
    def _():
        st_ref[...] = jnp.zeros_like(st_ref)

    row = lax.broadcasted_iota(jnp.int32, (c_len, c_len), 0)
    col = lax.broadcasted_iota(jnp.int32, (c_len, c_len), 1)
    tril = (col <= row).astype(BF16)
    delta = row - col
    band = jnp.where((delta >= 0) & (delta <= row % GLA_DIAG), delta, -1)
    fine_masks = [
        (blk, (row // blk == col // blk) & (row % blk >= blk // 2) & (col % blk < blk // 2))
        for blk in GLA_FINE_LEVELS]
    half_rows = c_len // 2
    hrow = lax.broadcasted_iota(jnp.int32, (half_rows, half_rows), 0)
    hcol = lax.broadcasted_iota(jnp.int32, (half_rows, half_rows), 1)
    gain = gain_ref[...]
    scale = GLA_HEAD_K ** -0.5

    def gather_rows(a, blk, second_half):
        half = blk // 2
        off = half if second_half else 0
        return jnp.concatenate([a[s + off:s + off + half] for s in range(0, c_len, blk)], axis=0)

    def head_chunk(hd, sl):
        ksl = slice(hd * GLA_HEAD_K, (hd + 1) * GLA_HEAD_K)
        vsl = slice(hd * GLA_HEAD_V, (hd + 1) * GLA_HEAD_V)
        la = la_ref[0, sl, ksl]
        b = _cumsum_rows(tril, la)
        yield
        b_last = b[c_len - 1:c_len, :]
        qf = q_ref[0, sl, ksl].astype(F32)
        kf = k_ref[0, sl, ksl].astype(F32)
        v = v_ref[0, sl, vsl]

        st = st_ref[hd]
        qi = (qf * jnp.exp2(b)).astype(BF16)
        o = lax.dot_general(qi, st.astype(BF16), NT_DIMS, preferred_element_type=F32)
        kd = (kf * jnp.exp2(b_last - b)).astype(BF16)
        st_ref[hd] = st * jnp.exp2(b_last) + lax.dot_general(
            v, kd, TN_DIMS, preferred_element_type=F32)
        yield

        parts = []
        for blk in GLA_COARSE_LEVELS:
            half = blk // 2
            ref = jnp.concatenate(
                [jnp.broadcast_to(b[s + half - 1:s + half, :], (half, GLA_HEAD_K))
                 for s in range(0, c_len, blk)], axis=0)
            ql = (gather_rows(qf, blk, True) * jnp.exp2(gather_rows(b, blk, True) - ref)).astype(BF16)
            kl = (gather_rows(kf, blk, False) * jnp.exp2(ref - gather_rows(b, blk, False))).astype(BF16)
            a_l = lax.dot_general(ql, kl, NT_DIMS, preferred_element_type=F32)
            yield
            if blk < c_len:
                a_l = jnp.where(hrow // half == hcol // half, a_l, 0.0)
            parts.append((blk, jnp.dot(a_l.astype(BF16), gather_rows(v, blk, False),
                                       preferred_element_type=F32)))
            yield

        attn = jnp.zeros((c_len, c_len), F32)
        for blk, mk in fine_masks:
            half = blk // 2
            ref = jnp.concatenate(
                [jnp.broadcast_to(b[s + half - 1:s + half, :], (blk, GLA_HEAD_K))
                 for s in range(0, c_len, blk)], axis=0)
            ql = (qf * jnp.exp2(b - ref)).astype(BF16)
            kl = (kf * jnp.exp2(ref - b)).astype(BF16)
            a_l = lax.dot_general(ql, kl, NT_DIMS, preferred_element_type=F32)
            yield
            attn = jnp.where(mk, a_l, attn)
        for d in range(GLA_DIAG):
            if d:
                prod = qf * pltpu.roll(kf, d, 0) * jnp.exp2(b - pltpu.roll(b, d, 0))
            else:
                prod = qf * kf
            attn = jnp.where(band == d, jnp.sum(prod, axis=-1, keepdims=True), attn)
        o += jnp.dot(attn.astype(BF16), v, preferred_element_type=F32)
        yield

        groups = [o[g:g + SUBLANES] for g in range(0, c_len, SUBLANES)]
        for blk, part in parts:
            half = blk // 2
            src = 0
            for s in range(0, c_len, blk):
                for g in range((s + half) // SUBLANES, (s + blk) // SUBLANES):
                    groups[g] = groups[g] + part[src:src + SUBLANES]
                    src += SUBLANES
        o = jnp.concatenate(groups, axis=0)

        ms = jnp.mean(o * o, axis=-1, keepdims=True)
        factor = scale * lax.rsqrt(scale * scale * ms + RMS_EPS)
        r = r_ref[0, sl, vsl].astype(F32)
        o_ref[0, sl, vsl] = (o * factor * gain * (r * jax.nn.sigmoid(r))).astype(o_ref.dtype)

    def chunk(c, carry):
        sl = pl.ds(pl.multiple_of(c * c_len, c_len), c_len)
        _lockstep([head_chunk(hd, sl) for hd in range(heads)])
        return carry

    lax.fori_loop(0, n_chunks, chunk, 0)


def _gla_mix(qkvr, la, gain, b_sz, s_len, t_len, heads):
    n_chunks = t_len // GLA_CHUNK
    wk = heads * GLA_HEAD_K
    wv = heads * GLA_HEAD_V
    kq = GLA_KEY_DIM // wk
    kv = 2 * GLA_KEY_DIM // wv
    kr = kv + GLA_VAL_DIM // wv
    return pl.pallas_call(
        functools.partial(_gla_kernel, n_chunks=n_chunks, heads=heads),
        grid=(b_sz, GLA_HEADS // heads, s_len // t_len),
        in_specs=[
            pl.BlockSpec((1, t_len, wk), lambda b, h, t: (b, t, h)),
            pl.BlockSpec((1, t_len, wk), lambda b, h, t: (b, t, kq + h)),
            pl.BlockSpec((1, t_len, wv), lambda b, h, t: (b, t, kv + h)),
            pl.BlockSpec((1, t_len, wv), lambda b, h, t: (b, t, kr + h)),
            pl.BlockSpec((1, t_len, wk), lambda b, h, t: (b, t, h)),
            pl.BlockSpec((1, GLA_HEAD_V), lambda b, h, t: (0, 0)),
        ],
        out_specs=pl.BlockSpec((1, t_len, wv), lambda b, h, t: (b, t, h)),
        out_shape=jax.ShapeDtypeStruct((b_sz, s_len, GLA_VAL_DIM), BF16),
        scratch_shapes=[pltpu.VMEM((heads, GLA_HEAD_V, GLA_HEAD_K), F32)],
        compiler_params=_params("parallel", "parallel", "arbitrary"),
        name="gla_mix",
    )(qkvr, qkvr, qkvr, qkvr, la, gain)


def _residual_matmul_kernel(x_ref, a_ref, w_ref, o_ref):
    o_ref[...] = x_ref[...] + jnp.dot(a_ref[...], w_ref[0].astype(BF16),
                                      preferred_element_type=F32)


def _residual_matmul(x2, a, w3, tm, tn):
    m, k_dim = a.shape
    n = w3.shape[2]
    return pl.pallas_call(
        _residual_matmul_kernel,
        grid=(m // tm, n // tn),
        in_specs=[
            pl.BlockSpec((tm, tn), lambda i, j: (i, j)),
            pl.BlockSpec((tm, k_dim), lambda i, j: (i, 0)),
            pl.BlockSpec((1, k_dim, tn), lambda i, j: (0, 0, j)),
        ],
        out_specs=pl.BlockSpec((tm, tn), lambda i, j: (i, j)),
        out_shape=jax.ShapeDtypeStruct((m, n), F32),
        compiler_params=_params("parallel", "arbitrary"),
        name="residual_matmul",
    )(x2, a, w3)


def _ffn_kernel(x_ref, g_ref, wg_ref, wu_ref, wd_ref, o_ref, h_ref):
    @pl.when(pl.program_id(1) == 0)
    def _():
        _norm_to_scratch(x_ref, g_ref, h_ref)
        o_ref[...] = x_ref[...]

    h = h_ref[...]
    gate = jnp.dot(h, wg_ref[0].astype(BF16), preferred_element_type=F32)
    up = jnp.dot(h, wu_ref[0].astype(BF16), preferred_element_type=F32)
    act = (gate * jax.nn.sigmoid(gate) * up).astype(BF16)
    o_ref[...] += jnp.dot(act, wd_ref[0].astype(BF16), preferred_element_type=F32)


def _ffn(x2, g, wg, wu, wd, layer, tm, tf):
    m = x2.shape[0]
    return pl.pallas_call(
        _ffn_kernel,
        grid=(m // tm, D_FF // tf),
        in_specs=[
            pl.BlockSpec((tm, D_MODEL), lambda i, f: (i, 0)),
            pl.BlockSpec((1, D_MODEL), lambda i, f: (0, 0)),
            pl.BlockSpec((1, D_MODEL, tf), lambda i, f: (layer, 0, f)),
            pl.BlockSpec((1, D_MODEL, tf), lambda i, f: (layer, 0, f)),
            pl.BlockSpec((1, tf, D_MODEL), lambda i, f: (layer, f, 0)),
        ],
        out_specs=pl.BlockSpec((tm, D_MODEL), lambda i, f: (i, 0)),
        out_shape=jax.ShapeDtypeStruct((m, D_MODEL), F32),
        scratch_shapes=[pltpu.VMEM((tm, D_MODEL), BF16)],
        compiler_params=_params("parallel", "arbitrary"),
        name="ffn",
    )(x2, g, wg, wu, wd)


def _fox_proj_kernel(x_ref, g_ref, wa_ref, wb_ref, wf_ref, bf_ref, qkg_ref, o_ref, lf_ref, h_ref,
                     *, q_tiles, qk_tiles, a_tiles):
    j = pl.program_id(1)

    @pl.when(j == 0)
    def _():
        _norm_to_scratch(x_ref, g_ref, h_ref)
        h = h_ref[...]
        z = lax.dot_general(h, wf_ref[...].astype(BF16), NT_DIMS,
                            preferred_element_type=F32) + bf_ref[...]
        lf_ref[...] = _log_sigmoid(z)

    @pl.when(j < qk_tiles)
    def _():
        acc = jnp.dot(h_ref[...], wa_ref[...], preferred_element_type=F32)
        mult = jnp.where(j < q_tiles, FOX_SCALE * LOG2E, 1.0).astype(F32)
        for c in range(acc.shape[1] // FOX_HEAD_DIM):
            cs = slice(c * FOX_HEAD_DIM, (c + 1) * FOX_HEAD_DIM)
            xs = acc[:, cs]
            ms = jnp.mean(xs * xs, axis=-1, keepdims=True)
            o_ref[:, cs] = (xs * (lax.rsqrt(ms + RMS_EPS) * mult) * qkg_ref[:, cs]).astype(o_ref.dtype)

    @pl.when((j >= qk_tiles) & (j < a_tiles))
    def _():
        o_ref[...] = jnp.dot(h_ref[...], wa_ref[...], preferred_element_type=F32).astype(o_ref.dtype)

    @pl.when(j >= a_tiles)
    def _():
        o_ref[...] = jnp.dot(h_ref[...], wb_ref[...], preferred_element_type=F32).astype(o_ref.dtype)


def _fox_proj(x2, g, w_qkv, w_og, wf, bf, qk_gain, tm, tn):
    m = x2.shape[0]
    q_tiles = D_MODEL // tn
    qk_tiles = 2 * q_tiles
    a_tiles = w_qkv.shape[1] // tn
    return pl.pallas_call(
        functools.partial(_fox_proj_kernel, q_tiles=q_tiles, qk_tiles=qk_tiles, a_tiles=a_tiles),
        grid=(m // tm, FOX_MAIN // tn),
        in_specs=[
            pl.BlockSpec((tm, D_MODEL), lambda i, j: (i, 0)),
            pl.BlockSpec((1, D_MODEL), lambda i, j: (0, 0)),
            pl.BlockSpec((D_MODEL, tn), lambda i, j: (0, jnp.minimum(j, a_tiles - 1))),
            pl.BlockSpec((D_MODEL, tn), lambda i, j: (0, jnp.maximum(j - a_tiles, 0))),
            pl.BlockSpec((LANES, D_MODEL), lambda i, j: (0, 0)),
            pl.BlockSpec((1, LANES), lambda i, j: (0, 0)),
            pl.BlockSpec((1, tn), lambda i, j: (0, jnp.minimum(j, qk_tiles - 1))),
        ],
        out_specs=[
            pl.BlockSpec((tm, tn), lambda i, j: (i, j)),
            pl.BlockSpec((tm, LANES), lambda i, j: (i, 0)),
        ],
        out_shape=[
            jax.ShapeDtypeStruct((m, FOX_MAIN), BF16),
            jax.ShapeDtypeStruct((m, LANES), F32),
        ],
        scratch_shapes=[pltpu.VMEM((tm, D_MODEL), BF16)],
        compiler_params=_params("parallel", "arbitrary"),
        name="fox_proj",
    )(x2, g, w_qkv, w_og, wf, bf, qk_gain)


def _cumsum_kernel(lf_ref, c_ref, carry_ref):
    @pl.when(pl.program_id(1) == 0)
    def _():
        carry_ref[...] = jnp.zeros_like(carry_ref)

    t_len = lf_ref.shape[1]
    row = lax.broadcasted_iota(jnp.int32, (t_len, t_len), 0)
    col = lax.broadcasted_iota(jnp.int32, (t_len, t_len), 1)
    tril = (col <= row).astype(BF16)
    c = _cumsum_rows(tril, lf_ref[0]) + carry_ref[...]
    c_ref[0] = c * LOG2E
    carry_ref[...] = c[t_len - 1:t_len, :]


def _seq_cumsum(lf, t_len):
    b_sz, s_len, _ = lf.shape
    return pl.pallas_call(
        _cumsum_kernel,
        grid=(b_sz, s_len // t_len),
        in_specs=[pl.BlockSpec((1, t_len, LANES), lambda b, t: (b, t, 0))],
        out_specs=pl.BlockSpec((1, t_len, LANES), lambda b, t: (b, t, 0)),
        out_shape=jax.ShapeDtypeStruct(lf.shape, F32),
        scratch_shapes=[pltpu.VMEM((1, LANES), F32)],
        compiler_params=_params("parallel", "arbitrary"),
        name="fox_cumsum",
    )(lf)


def _fox_attn_kernel(ends_ref, firsts_ref, slack_ref, q_ref, k_ref, v_ref, og_ref, ck_ref, o_ref,
                     acc_ref, *, blk, n_sub, tk):
    i = pl.program_id(2)
    sub = blk // n_sub
    per_blk = blk // tk
    n_kt = ck_ref.shape[2]
    bh = pl.program_id(0) * pl.num_programs(1) + pl.program_id(1)
    n_prev = i * per_blk
    limit = firsts_ref[bh * pl.num_programs(2) + i] + slack_ref[0]

    def first_needed(j, lo):
        return jnp.minimum(lo, jnp.where(ends_ref[bh * n_kt + j] > limit, n_prev, j))

    j_start = lax.fori_loop(0, n_prev, first_needed, n_prev)

    ck_own = jnp.concatenate(
        [ck_ref[0, 0, pl.ds(i * per_blk + t, 1), :] for t in range(per_blk)], axis=1)
    c0 = ck_own[:, 0:1]
    acc_ref[...] = jnp.zeros_like(acc_ref)

    def update(r, out, carry, ks, vs, bias, mask_offset):
        m_old, l_old = carry
        rows = slice(r * sub, (r + 1) * sub)
        s = lax.dot_general(q_ref[0, rows, :], ks, NT_DIMS, preferred_element_type=F32) + bias
        yield
        if mask_offset is not None:
            row = lax.broadcasted_iota(jnp.int32, s.shape, 0)
            col = lax.broadcasted_iota(jnp.int32, s.shape, 1)
            s = jnp.where(col <= row + mask_offset, s, -jnp.inf)
        m_new = jnp.maximum(m_old, jnp.max(s, axis=-1, keepdims=True))
        p = jnp.exp2(s - m_new)
        alpha = jnp.exp2(m_old - m_new)
        l_new = alpha * l_old + jnp.sum(p, axis=-1, keepdims=True)
        acc_ref[rows, :] = alpha * acc_ref[rows, :] + jnp.dot(
            p.astype(BF16), vs, preferred_element_type=F32)
        out[r] = (m_new, l_new)

    def body(j, carry):
        sl = pl.ds(pl.multiple_of(j * tk, tk), tk)
        ks = k_ref[0, sl, :]
        vs = v_ref[0, sl, :]
        bias = c0 - ck_ref[0, 0, pl.ds(j, 1), :]
        out = [None] * n_sub
        _lockstep([update(r, out, carry[r], ks, vs, bias, None) for r in range(n_sub)])
        return tuple(out)

    init = tuple((jnp.full((sub, 1), -jnp.inf, F32), jnp.zeros((sub, 1), F32)) for _ in range(n_sub))
    carry = lax.fori_loop(j_start, n_prev, body, init)

    base = pl.multiple_of(i * blk, blk)
    final = [None] * n_sub
    steps = []
    for r in range(n_sub):
        width = (r + 1) * sub
        ks = k_ref[0, pl.ds(base, width), :]
        vs = v_ref[0, pl.ds(base, width), :]
        bias = c0 - ck_own[:, 0:width]
        steps.append(update(r, final, carry[r], ks, vs, bias, r * sub))
    _lockstep(steps)
    for r in range(n_sub):
        rows = slice(r * sub, (r + 1) * sub)
        gate = jax.nn.sigmoid(og_ref[0, rows, :].astype(F32))
        o_ref[0, rows, :] = (acc_ref[rows, :] / final[r][1] * gate).astype(o_ref.dtype)


def _fox_attn(qkvo, ck, slack, b_sz, s_len, blk, n_sub):
    h_cols = D_MODEL // FOX_HEAD_DIM
    n_blk = s_len // blk
    n_kt, tk = ck.shape[2:]
    ends = ck[:, :, :, tk - 1].reshape(-1)
    firsts = ck.reshape(b_sz, FOX_HEADS, n_blk, blk)[:, :, :, 0].reshape(-1)
    grid_spec = pltpu.PrefetchScalarGridSpec(
        num_scalar_prefetch=3,
        grid=(b_sz, FOX_HEADS, n_blk),
        in_specs=[
            pl.BlockSpec((1, blk, FOX_HEAD_DIM), lambda b, h, i, *_: (b, i, h)),
            pl.BlockSpec((1, s_len, FOX_HEAD_DIM), lambda b, h, i, *_: (b, 0, h_cols + h)),
            pl.BlockSpec((1, s_len, FOX_HEAD_DIM), lambda b, h, i, *_: (b, 0, 2 * h_cols + h)),
            pl.BlockSpec((1, blk, FOX_HEAD_DIM), lambda b, h, i, *_: (b, i, 3 * h_cols + h)),
            pl.BlockSpec((1, 1, n_kt, tk), lambda b, h, i, *_: (b, h, 0, 0)),
        ],
        out_specs=pl.BlockSpec((1, blk, FOX_HEAD_DIM), lambda b, h, i, *_: (b, i, h)),
        scratch_shapes=[pltpu.VMEM((blk, FOX_HEAD_DIM), F32)],
    )
    return pl.pallas_call(
        functools.partial(_fox_attn_kernel, blk=blk, n_sub=n_sub, tk=tk),
        grid_spec=grid_spec,
        out_shape=jax.ShapeDtypeStruct((b_sz, s_len, D_MODEL), BF16),
        compiler_params=_params("parallel", "parallel", "arbitrary"),
        name="fox_attn",
    )(ends, firsts, slack, qkvo, qkvo, qkvo, qkvo, ck)


def _pad_rows(w, n):
    return jnp.pad(w, ((0, n - w.shape[0]), (0, 0)))


def kernel(x, norm_mix, norm_ffn, gla_w_in, gla_w_g2, gla_b_g2, gla_o_gain, gla_w_o,
           fox_w_in, fox_b_f, fox_q_gain, fox_k_gain, fox_w_o,
           ffn_w_gate, ffn_w_up, ffn_w_down):
    b_sz, s_len, d = x.shape
    assert d == D_MODEL and s_len % GLA_CHUNK == 0
    m = b_sz * s_len
    tm = min(1024, m)
    tn = 1024
    tm_ffn = min(1024, m)
    tf = 256
    gla_t = min(512, s_len)
    attn_blk = min(512, s_len)
    attn_sub = min(512, s_len)
    attn_tk = min(512, s_len)
    cum_t = min(256, s_len)

    x2 = x.reshape(m, d)

    def ffn_layer(xin, layer):
        return _ffn(xin, norm_ffn[layer][None, :], ffn_w_gate, ffn_w_up, ffn_w_down, layer, tm_ffn, tf)

    gla_wt = jnp.swapaxes(gla_w_in, 1, 2)
    qkvr, la = _gla_proj(
        x2, norm_mix[0][None, :],
        gla_wt,
        _pad_rows(gla_wt[0, GLA_MAIN:, :], LANES),
        _pad_rows(gla_w_g2[0], LANES),
        gla_b_g2[0][None, :], tm, tn)
    og = _gla_mix(qkvr.reshape(b_sz, s_len, GLA_MAIN), la.reshape(b_sz, s_len, GLA_KEY_DIM),
                  gla_o_gain[0][None, :], b_sz, s_len, gla_t, 4)
    x2 = _residual_matmul(x2, og.reshape(m, GLA_VAL_DIM), gla_w_o, tm, tn)
    x2 = ffn_layer(x2, 0)

    qk_gain = jnp.concatenate([jnp.tile(fox_q_gain[0], FOX_HEADS), jnp.tile(fox_k_gain[0], FOX_HEADS)])
    f_lo, f_hi = 3 * D_MODEL, 3 * D_MODEL + FOX_HEADS
    fox_wt = jnp.swapaxes(fox_w_in, 1, 2)
    qkvo, lf = _fox_proj(
        x2, norm_mix[1][None, :],
        _cast_weight_t(fox_wt, 0, f_lo),
        _cast_weight_t(fox_wt[:, f_hi:, :], 0, D_MODEL),
        _pad_rows(fox_wt[0, f_lo:f_hi, :], LANES),
        jnp.pad(fox_b_f[0], (0, LANES - FOX_HEADS))[None, :],
        qk_gain[None, :], tm, tn)
    c = _seq_cumsum(lf.reshape(b_sz, s_len, LANES), cum_t)
    c_hs = c[:, :, :FOX_HEADS].transpose(0, 2, 1)
    ck = c_hs.reshape(b_sz, FOX_HEADS, s_len // attn_tk, attn_tk)
    qk_max = (FOX_HEAD_DIM * FOX_SCALE * LOG2E * BF16_SLOP
              * jnp.max(jnp.abs(fox_q_gain[0])) * jnp.max(jnp.abs(fox_k_gain[0])))
    slack = (2.0 * qk_max + F32_EXP2_FLOOR).reshape(1)
    o = _fox_attn(qkvo.reshape(b_sz, s_len, FOX_MAIN), ck, slack, b_sz, s_len, attn_blk,
                  attn_blk // attn_sub)
    x2 = _residual_matmul(x2, o.reshape(m, D_MODEL), fox_w_o, tm, tn)
    x2 = ffn_layer(x2, 1)
    return x2.reshape(b_sz, s_len, d)
```

```python
import functools

import jax
import jax.numpy as jnp
from jax import lax
from jax.experimental import pallas as pl
from jax.experimental.pallas import tpu as pltpu

F32 = jnp.float32
BF16 = jnp.bfloat16

D_MODEL = 2048
RMS_EPS = 1e-6

GLA_HEADS = 4
GLA_KEY_DIM = D_MODEL // 2
GLA_VAL_DIM = D_MODEL
GLA_HEAD_K = GLA_KEY_DIM // GLA_HEADS
GLA_HEAD_V = GLA_VAL_DIM // GLA_HEADS
GLA_GATE_RANK = 16
GLA_GATE_TEMP = 16.0
GLA_CHUNK = 64
GLA_MAIN = 2 * GLA_KEY_DIM + 2 * GLA_VAL_DIM
GLA_COARSE_LEVELS = (64, 32)
GLA_FINE_LEVELS = (16, 8)
GLA_DIAG = 4

FOX_HEAD_DIM = 128
FOX_HEADS = D_MODEL // FOX_HEAD_DIM
FOX_MAIN = 4 * D_MODEL
FOX_SCALE = FOX_HEAD_DIM ** -0.5
LOG2E = 1.4426950408889634
BF16_SLOP = 1.02
F32_EXP2_FLOOR = 160.0

D_FF = ((8 * D_MODEL + 2) // 3 + 255) // 256 * 256

LANES = 128
SUBLANES = 8
NORM_SLAB = 256
CAST_COLS = 512
VMEM_LIMIT = 58 * 1024 * 1024

_DONE = object()


def _lockstep(chains):
    pending = list(chains)
    while pending:
        pending = [g for g in pending if next(g, _DONE) is not _DONE]


NT_DIMS = (((1,), (1,)), ((), ()))
TN_DIMS = (((0,), (0,)), ((), ()))


def _params(*sem, flags=None):
    return pltpu.CompilerParams(dimension_semantics=sem, vmem_limit_bytes=VMEM_LIMIT, flags=flags)


def _rmsnorm_rows(x, g):
    ms = jnp.mean(x * x, axis=-1, keepdims=True)
    return x * lax.rsqrt(ms + RMS_EPS) * g


def _log_sigmoid(z):
    return jnp.minimum(z, 0.0) - jnp.log1p(jnp.exp(-jnp.abs(z)))


def _norm_to_scratch(x_ref, g_ref, h_ref):
    rows = x_ref.shape[0]
    slab = min(NORM_SLAB, rows)
    g = g_ref[...]
    for r0 in range(0, rows, slab):
        h_ref[r0:r0 + slab, :] = _rmsnorm_rows(x_ref[r0:r0 + slab, :], g).astype(BF16)


def _split3(a):
    hi = a.astype(BF16)
    r1 = a - hi.astype(F32)
    mid = r1.astype(BF16)
    lo = (r1 - mid.astype(F32)).astype(BF16)
    return hi, mid, lo


def _cumsum_rows(tril, a):
    hi, mid, lo = _split3(a)
    out = jnp.dot(tril, hi, preferred_element_type=F32)
    out += jnp.dot(tril, mid, preferred_element_type=F32)
    out += jnp.dot(tril, lo, preferred_element_type=F32)
    return out


def _cast_t_kernel(wt_ref, o_ref):
    o_ref[...] = wt_ref[0].T.astype(BF16)


def _cast_weight_t(wt3, layer, n_cols):
    _, _, k_dim = wt3.shape
    tc = CAST_COLS
    assert n_cols % tc == 0
    return pl.pallas_call(
        _cast_t_kernel,
        grid=(n_cols // tc,),
        in_specs=[pl.BlockSpec((1, tc, k_dim), lambda i: (layer, i, 0))],
        out_specs=pl.BlockSpec((k_dim, tc), lambda i: (0, i)),
        out_shape=jax.ShapeDtypeStruct((k_dim, n_cols), BF16),
        compiler_params=_params("parallel"),
        name="cast_weight_t",
    )(wt3)


def _gla_proj_kernel(x_ref, g_ref, w_ref, wg1_ref, wg2_ref, bg_ref, o_ref, la_ref, h_ref):
    @pl.when(pl.program_id(1) == 0)
    def _():
        _norm_to_scratch(x_ref, g_ref, h_ref)
        h = h_ref[...]
        g1 = lax.dot_general(h, wg1_ref[...].astype(BF16), NT_DIMS, preferred_element_type=F32)
        z = jnp.dot(g1.astype(BF16), wg2_ref[...].astype(BF16),
                    preferred_element_type=F32) + bg_ref[...]
        la_ref[...] = _log_sigmoid(z) * (LOG2E / GLA_GATE_TEMP)

    o_ref[...] = lax.dot_general(h_ref[...], w_ref[0].astype(BF16), NT_DIMS,
                                 preferred_element_type=F32).astype(o_ref.dtype)


def _gla_proj(x2, g, wt3, wg1, wg2, bg, tm, tn):
    m = x2.shape[0]
    return pl.pallas_call(
        _gla_proj_kernel,
        grid=(m // tm, GLA_MAIN // tn),
        in_specs=[
            pl.BlockSpec((tm, D_MODEL), lambda i, j: (i, 0)),
            pl.BlockSpec((1, D_MODEL), lambda i, j: (0, 0)),
            pl.BlockSpec((1, tn, D_MODEL), lambda i, j: (0, j, 0)),
            pl.BlockSpec((LANES, D_MODEL), lambda i, j: (0, 0)),
            pl.BlockSpec((LANES, GLA_KEY_DIM), lambda i, j: (0, 0)),
            pl.BlockSpec((1, GLA_KEY_DIM), lambda i, j: (0, 0)),
        ],
        out_specs=[
            pl.BlockSpec((tm, tn), lambda i, j: (i, j)),
            pl.BlockSpec((tm, GLA_KEY_DIM), lambda i, j: (i, 0)),
        ],
        out_shape=[
            jax.ShapeDtypeStruct((m, GLA_MAIN), BF16),
            jax.ShapeDtypeStruct((m, GLA_KEY_DIM), F32),
        ],
        scratch_shapes=[pltpu.VMEM((tm, D_MODEL), BF16)],
        compiler_params=_params("parallel", "arbitrary"),
        name="gla_proj",
    )(x2, g, wt3, wg1, wg2, bg)


def _gla_kernel(q_ref, k_ref, v_ref, r_ref, la_ref, gain_ref, o_ref, st_ref, *, n_chunks, heads):
    c_len = GLA_CHUNK

    @pl.when(pl.program_id(2) == 0)
    def _():
        st_ref[...] = jnp.zeros_like(st_ref)

    row = lax.broadcasted_iota(jnp.int32, (c_len, c_len), 0)
    col = lax.broadcasted_iota(jnp.int32, (c_len, c_len), 1)
    tril = (col <= row).astype(BF16)
    delta = row - col
    band = jnp.where((delta >= 0) & (delta <= row % GLA_DIAG), delta, -1)
    fine_masks = [
        (blk, (row // blk == col // blk) & (row % blk >= blk // 2) & (col % blk < blk // 2))
        for blk in GLA_FINE_LEVELS]
    half_rows = c_len // 2
    hrow = lax.broadcasted_iota(jnp.int32, (half_rows, half_rows), 0)
    hcol = lax.broadcasted_iota(jnp.int32, (half_rows, half_rows), 1)
    gain = gain_ref[...]
    scale = GLA_HEAD_K ** -0.5

    def gather_rows(a, blk, second_half):
        half = blk // 2
        off = half if second_half else 0
        return jnp.concatenate([a[s + off:s + off + half] for s in range(0, c_len, blk)], axis=0)

    def head_chunk(hd, sl):
        ksl = slice(hd * GLA_HEAD_K, (hd + 1) * GLA_HEAD_K)
        vsl = slice(hd * GLA_HEAD_V, (hd + 1) * GLA_HEAD_V)
        la = la_ref[0, sl, ksl]
        b = _cumsum_rows(tril, la)
        yield
        b_last = b[c_len - 1:c_len, :]
        qf = q_ref[0, sl, ksl].astype(F32)
        kf = k_ref[0, sl, ksl].astype(F32)
        v = v_ref[0, sl, vsl]

        st = st_ref[hd]
        qi = (qf * jnp.exp2(b)).astype(BF16)
        o = lax.dot_general(qi, st.astype(BF16), NT_DIMS, preferred_element_type=F32)
        kd = (kf * jnp.exp2(b_last - b)).astype(BF16)
        st_ref[hd] = st * jnp.exp2(b_last) + lax.dot_general(
            v, kd, TN_DIMS, preferred_element_type=F32)
        yield

        parts = []
        for blk in GLA_COARSE_LEVELS:
            half = blk // 2
            ref = jnp.concatenate(
                [jnp.broadcast_to(b[s + half - 1:s + half, :], (half, GLA_HEAD_K))
                 for s in range(0, c_len, blk)], axis=0)
            ql = (gather_rows(qf, blk, True) * jnp.exp2(gather_rows(b, blk, True) - ref)).astype(BF16)
            kl = (gather_rows(kf, blk, False) * jnp.exp2(ref - gather_rows(b, blk, False))).astype(BF16)
            a_l = lax.dot_general(ql, kl, NT_DIMS, preferred_element_type=F32)
            yield
            if blk < c_len:
                a_l = jnp.where(hrow // half == hcol // half, a_l, 0.0)
            parts.append((blk, jnp.dot(a_l.astype(BF16), gather_rows(v, blk, False),
                                       preferred_element_type=F32)))
            yield

        attn = jnp.zeros((c_len, c_len), F32)
        for blk, mk in fine_masks:
            half = blk // 2
            ref = jnp.concatenate(
                [jnp.broadcast_to(b[s + half - 1:s + half, :], (blk, GLA_HEAD_K))
                 for s in range(0, c_len, blk)], axis=0)
            ql = (qf * jnp.exp2(b - ref)).astype(BF16)
            kl = (kf * jnp.exp2(ref - b)).astype(BF16)
            a_l = lax.dot_general(ql, kl, NT_DIMS, preferred_element_type=F32)
            yield
            attn = jnp.where(mk, a_l, attn)
        for d in range(GLA_DIAG):
            if d:
                prod = qf * pltpu.roll(kf, d, 0) * jnp.exp2(b - pltpu.roll(b, d, 0))
            else:
                prod = qf * kf
            attn = jnp.where(band == d, jnp.sum(prod, axis=-1, keepdims=True), attn)
        o += jnp.dot(attn.astype(BF16), v, preferred_element_type=F32)
        yield

        groups = [o[g:g + SUBLANES] for g in range(0, c_len, SUBLANES)]
        for blk, part in parts:
            half = blk // 2
            src = 0
            for s in range(0, c_len, blk):
                for g in range((s + half) // SUBLANES, (s + blk) // SUBLANES):
                    groups[g] = groups[g] + part[src:src + SUBLANES]
                    src += SUBLANES
        o = jnp.concatenate(groups, axis=0)

        ms = jnp.mean(o * o, axis=-1, keepdims=True)
        factor = scale * lax.rsqrt(scale * scale * ms + RMS_EPS)
        r = r_ref[0, sl, vsl].astype(F32)
        o_ref[0, sl, vsl] = (o * factor * gain * (r * jax.nn.sigmoid(r))).astype(o_ref.dtype)

    def chunk(c, carry):
        sl = pl.ds(pl.multiple_of(c * c_len, c_len), c_len)
        _lockstep([head_chunk(hd, sl) for hd in range(heads)])
        return carry

    lax.fori_loop(0, n_chunks, chunk, 0)


def _gla_mix(qkvr, la, gain, b_sz, s_len, t_len, heads):
    n_chunks = t_len // GLA_CHUNK
    wk = heads * GLA_HEAD_K
    wv = heads * GLA_HEAD_V
    kq = GLA_KEY_DIM // wk
    kv = 2 * GLA_KEY_DIM // wv
    kr = kv + GLA_VAL_DIM // wv
    return pl.pallas_call(
        functools.partial(_gla_kernel, n_chunks=n_chunks, heads=heads),
        grid=(b_sz, GLA_HEADS // heads, s_len // t_len),
        in_specs=[
            pl.BlockSpec((1, t_len, wk), lambda b, h, t: (b, t, h)),
            pl.BlockSpec((1, t_len, wk), lambda b, h, t: (b, t, kq + h)),
            pl.BlockSpec((1, t_len, wv), lambda b, h, t: (b, t, kv + h)),
            pl.BlockSpec((1, t_len, wv), lambda b, h, t: (b, t, kr + h)),
            pl.BlockSpec((1, t_len, wk), lambda b, h, t: (b, t, h)),
            pl.BlockSpec((1, GLA_HEAD_V), lambda b, h, t: (0, 0)),
        ],
        out_specs=pl.BlockSpec((1, t_len, wv), lambda b, h, t: (b, t, h)),
        out_shape=jax.ShapeDtypeStruct((b_sz, s_len, GLA_VAL_DIM), BF16),
        scratch_shapes=[pltpu.VMEM((heads, GLA_HEAD_V, GLA_HEAD_K), F32)],
        compiler_params=_params("parallel", "parallel", "arbitrary"),
        name="gla_mix",
    )(qkvr, qkvr, qkvr, qkvr, la, gain)


def _residual_matmul_kernel(x_ref, a_ref, w_ref, o_ref, wb_ref):
    @pl.when(pl.program_id(0) == 0)
    def _():
        for r0 in range(0, w_ref.shape[1], NORM_SLAB):
            wb_ref[r0:r0 + NORM_SLAB, :] = w_ref[0, r0:r0 + NORM_SLAB, :].astype(BF16)

    o_ref[...] = x_ref[...] + jnp.dot(a_ref[...], wb_ref[...], preferred_element_type=F32)


def _residual_matmul(x2, a, w3, tm):
    m, k_dim = a.shape
    n = w3.shape[2]
    return pl.pallas_call(
        _residual_matmul_kernel,
        grid=(m // tm,),
        in_specs=[
            pl.BlockSpec((tm, n), lambda i: (i, 0)),
            pl.BlockSpec((tm, k_dim), lambda i: (i, 0)),
            pl.BlockSpec((1, k_dim, n), lambda i: (0, 0, 0), pipeline_mode=pl.Buffered(1)),
        ],
        out_specs=pl.BlockSpec((tm, n), lambda i: (i, 0)),
        out_shape=jax.ShapeDtypeStruct((m, n), F32),
        scratch_shapes=[pltpu.VMEM((k_dim, n), BF16)],
        compiler_params=_params("arbitrary"),
        name="residual_matmul",
    )(x2, a, w3)


def _ffn_kernel(x_ref, g_ref, wg_ref, wu_ref, wd_ref, o_ref, h_ref):
    @pl.when(pl.program_id(1) == 0)
    def _():
        _norm_to_scratch(x_ref, g_ref, h_ref)
        o_ref[...] = x_ref[...]

    h = h_ref[...]
    gate = jnp.dot(h, wg_ref[0].astype(BF16), preferred_element_type=F32)
    up = jnp.dot(h, wu_ref[0].astype(BF16), preferred_element_type=F32)
    act = (gate * jax.nn.sigmoid(gate) * up).astype(BF16)
    o_ref[...] += jnp.dot(act, wd_ref[0].astype(BF16), preferred_element_type=F32)


def _ffn(x2, g, wg, wu, wd, layer, tm, tf):
    m = x2.shape[0]
    return pl.pallas_call(
        _ffn_kernel,
        grid=(m // tm, D_FF // tf),
        in_specs=[
            pl.BlockSpec((tm, D_MODEL), lambda i, f: (i, 0)),
            pl.BlockSpec((1, D_MODEL), lambda i, f: (0, 0)),
            pl.BlockSpec((1, D_MODEL, tf), lambda i, f: (layer, 0, f)),
            pl.BlockSpec((1, D_MODEL, tf), lambda i, f: (layer, 0, f)),
            pl.BlockSpec((1, tf, D_MODEL), lambda i, f: (layer, f, 0)),
        ],
        out_specs=pl.BlockSpec((tm, D_MODEL), lambda i, f: (i, 0)),
        out_shape=jax.ShapeDtypeStruct((m, D_MODEL), F32),
        scratch_shapes=[pltpu.VMEM((tm, D_MODEL), BF16)],
        compiler_params=_params("parallel", "arbitrary"),
        name="ffn",
    )(x2, g, wg, wu, wd)


def _fox_proj_kernel(x_ref, g_ref, wa_ref, wb_ref, wf_ref, bf_ref, qkg_ref, o_ref, lf_ref, h_ref,
                     *, q_tiles, qk_tiles, a_tiles):
    j = pl.program_id(1)

    @pl.when(j == 0)
    def _():
        _norm_to_scratch(x_ref, g_ref, h_ref)
        h = h_ref[...]
        z = lax.dot_general(h, wf_ref[...].astype(BF16), NT_DIMS,
                            preferred_element_type=F32) + bf_ref[...]
        lf_ref[...] = _log_sigmoid(z)

    @pl.when(j < qk_tiles)
    def _():
        acc = jnp.dot(h_ref[...], wa_ref[...], preferred_element_type=F32)
        mult = jnp.where(j < q_tiles, FOX_SCALE * LOG2E, 1.0).astype(F32)
        for c in range(acc.shape[1] // FOX_HEAD_DIM):
            cs = slice(c * FOX_HEAD_DIM, (c + 1) * FOX_HEAD_DIM)
            xs = acc[:, cs]
            ms = jnp.mean(xs * xs, axis=-1, keepdims=True)
            o_ref[:, cs] = (xs * (lax.rsqrt(ms + RMS_EPS) * mult) * qkg_ref[:, cs]).astype(o_ref.dtype)

    @pl.when((j >= qk_tiles) & (j < a_tiles))
    def _():
        o_ref[...] = jnp.dot(h_ref[...], wa_ref[...], preferred_element_type=F32).astype(o_ref.dtype)

    @pl.when(j >= a_tiles)
    def _():
        o_ref[...] = jnp.dot(h_ref[...], wb_ref[...], preferred_element_type=F32).astype(o_ref.dtype)


def _fox_proj(x2, g, w_qkv, w_og, wf, bf, qk_gain, tm, tn):
    m = x2.shape[0]
    q_tiles = D_MODEL // tn
    qk_tiles = 2 * q_tiles
    a_tiles = w_qkv.shape[1] // tn
    return pl.pallas_call(
        functools.partial(_fox_proj_kernel, q_tiles=q_tiles, qk_tiles=qk_tiles, a_tiles=a_tiles),
        grid=(m // tm, FOX_MAIN // tn),
        in_specs=[
            pl.BlockSpec((tm, D_MODEL), lambda i, j: (i, 0)),
            pl.BlockSpec((1, D_MODEL), lambda i, j: (0, 0)),
            pl.BlockSpec((D_MODEL, tn), lambda i, j: (0, jnp.minimum(j, a_tiles - 1))),
            pl.BlockSpec((D_MODEL, tn), lambda i, j: (0, jnp.maximum(j - a_tiles, 0))),
            pl.BlockSpec((LANES, D_MODEL), lambda i, j: (0, 0)),
            pl.BlockSpec((1, LANES), lambda i, j: (0, 0)),
            pl.BlockSpec((1, tn), lambda i, j: (0, jnp.minimum(j, qk_tiles - 1))),
        ],
        out_specs=[
            pl.BlockSpec((tm, tn), lambda i, j: (i, j)),
            pl.BlockSpec((tm, LANES), lambda i, j: (i, 0)),
        ],
        out_shape=[
            jax.ShapeDtypeStruct((m, FOX_MAIN), BF16),
            jax.ShapeDtypeStruct((m, LANES), F32),
        ],
        scratch_shapes=[pltpu.VMEM((tm, D_MODEL), BF16)],
        compiler_params=_params("parallel", "arbitrary"),
        name="fox_proj",
    )(x2, g, w_qkv, w_og, wf, bf, qk_gain)


def _cumsum_kernel(lf_ref, c_ref, carry_ref):
    @pl.when(pl.program_id(1) == 0)
    def _():
        carry_ref[...] = jnp.zeros_like(carry_ref)

    t_len = lf_ref.shape[1]
    row = lax.broadcasted_iota(jnp.int32, (t_len, t_len), 0)
    col = lax.broadcasted_iota(jnp.int32, (t_len, t_len), 1)
    tril = (col <= row).astype(BF16)
    c = _cumsum_rows(tril, lf_ref[0]) + carry_ref[...]
    c_ref[0] = c * LOG2E
    carry_ref[...] = c[t_len - 1:t_len, :]


def _seq_cumsum(lf, t_len):
    b_sz, s_len, _ = lf.shape
    return pl.pallas_call(
        _cumsum_kernel,
        grid=(b_sz, s_len // t_len),
        in_specs=[pl.BlockSpec((1, t_len, LANES), lambda b, t: (b, t, 0))],
        out_specs=pl.BlockSpec((1, t_len, LANES), lambda b, t: (b, t, 0)),
        out_shape=jax.ShapeDtypeStruct(lf.shape, F32),
        scratch_shapes=[pltpu.VMEM((1, LANES), F32)],
        compiler_params=_params("parallel", "arbitrary"),
        name="fox_cumsum",
    )(lf)


def _fox_attn_kernel(ends_ref, firsts_ref, slack_ref, q_ref, k_ref, v_ref, og_ref, ck_ref, o_ref,
                     acc_ref, *, blk, n_sub, tk):
    i = pl.program_id(2)
    sub = blk // n_sub
    assert sub == tk
    n_kt = ck_ref.shape[2]
    bh = pl.program_id(0) * pl.num_programs(1) + pl.program_id(1)
    n_loop = jnp.maximum(i * n_sub - 1, 0)
    limit = firsts_ref[bh * pl.num_programs(2) + i] + slack_ref[0]

    def first_needed(j, lo):
        return jnp.minimum(lo, jnp.where(ends_ref[bh * n_kt + j] > limit, n_loop, j))

    j_start = lax.fori_loop(0, n_loop, first_needed, n_loop)

    c0 = ck_ref[0, 0, pl.ds(i * n_sub, 1), :][:, 0:1]
    acc_ref[...] = jnp.zeros_like(acc_ref)

    def update(r, out, carry, ks, vs, bias, mask_offset):
        m_old, l_old = carry
        rows = slice(r * sub, (r + 1) * sub)
        s = lax.dot_general(q_ref[0, rows, :], ks, NT_DIMS, preferred_element_type=F32) + bias
        yield
        if mask_offset is not None:
            row = lax.broadcasted_iota(jnp.int32, s.shape, 0)
            col = lax.broadcasted_iota(jnp.int32, s.shape, 1)
            s = jnp.where(col <= row + mask_offset, s, -jnp.inf)
        m_new = jnp.maximum(m_old, jnp.max(s, axis=-1, keepdims=True))
        p = jnp.exp2(s - m_new)
        alpha = jnp.exp2(m_old - m_new)
        l_new = alpha * l_old + jnp.sum(p, axis=-1, keepdims=True)
        acc_ref[rows, :] = alpha * acc_ref[rows, :] + jnp.dot(
            p.astype(BF16), vs, preferred_element_type=F32)
        out[r] = (m_new, l_new)

    def body(j, carry):
        sl = pl.ds(pl.multiple_of(j * tk, tk), tk)
        ks = k_ref[0, sl, :]
        vs = v_ref[0, sl, :]
        bias = c0 - ck_ref[0, 0, pl.ds(j, 1), :]
        out = [None] * n_sub
        _lockstep([update(r, out, carry[r], ks, vs, bias, None) for r in range(n_sub)])
        return tuple(out)

    init = tuple((jnp.full((sub, 1), -jnp.inf, F32), jnp.zeros((sub, 1), F32)) for _ in range(n_sub))
    carry = lax.fori_loop(j_start, n_loop, body, init)

    carry = list(carry)
    for r in range(1, n_sub):
        t_prev = i * n_sub + r - 1
        limit_r = ends_ref[bh * n_kt + t_prev] + slack_ref[0]
        for u in range(r):
            g = i * n_sub - 1 + u

            def visit(g=g, r=r):
                out = [None] * n_sub
                sl = pl.ds(pl.multiple_of(g * tk, tk), tk)
                bias = c0 - ck_ref[0, 0, pl.ds(g, 1), :]
                _lockstep([update(r, out, carry[r], k_ref[0, sl, :], v_ref[0, sl, :], bias, None)])
                return out[r]

            needed = (i > 0) & (ends_ref[bh * n_kt + jnp.maximum(g, 0)] <= limit_r)
            carry[r] = lax.cond(needed, visit, lambda r=r: carry[r])

    final = [None] * n_sub
    steps = []
    for r in range(n_sub):
        t_own = i * n_sub + r
        t_lo = jnp.maximum(t_own - 1, 0)
        win = pl.ds(pl.multiple_of(t_lo * tk, tk), 2 * tk)
        ck_win = jnp.concatenate(
            [ck_ref[0, 0, pl.ds(t_lo, 1), :], ck_ref[0, 0, pl.ds(t_lo + 1, 1), :]], axis=1)
        steps.append(update(r, final, carry[r], k_ref[0, win, :], v_ref[0, win, :],
                            c0 - ck_win, (t_own - t_lo) * tk))
    _lockstep(steps)
    for r in range(n_sub):
        rows = slice(r * sub, (r + 1) * sub)
        gate = jax.nn.sigmoid(og_ref[0, rows, :].astype(F32))
        o_ref[0, rows, :] = (acc_ref[rows, :] / final[r][1] * gate).astype(o_ref.dtype)


def _fox_attn(qkvo, ck, slack, b_sz, s_len, blk, n_sub):
    h_cols = D_MODEL // FOX_HEAD_DIM
    n_blk = s_len // blk
    n_kt, tk = ck.shape[2:]
    ends = ck[:, :, :, tk - 1].reshape(-1)
    firsts = ck.reshape(b_sz, FOX_HEADS, n_blk, blk)[:, :, :, 0].reshape(-1)
    grid_spec = pltpu.PrefetchScalarGridSpec(
        num_scalar_prefetch=3,
        grid=(b_sz, FOX_HEADS, n_blk),
        in_specs=[
            pl.BlockSpec((1, blk, FOX_HEAD_DIM), lambda b, h, i, *_: (b, i, h)),
            pl.BlockSpec((1, s_len, FOX_HEAD_DIM), lambda b, h, i, *_: (b, 0, h_cols + h)),
            pl.BlockSpec((1, s_len, FOX_HEAD_DIM), lambda b, h, i, *_: (b, 0, 2 * h_cols + h)),
            pl.BlockSpec((1, blk, FOX_HEAD_DIM), lambda b, h, i, *_: (b, i, 3 * h_cols + h)),
            pl.BlockSpec((1, 1, n_kt, tk), lambda b, h, i, *_: (b, h, 0, 0)),
        ],
        out_specs=pl.BlockSpec((1, blk, FOX_HEAD_DIM), lambda b, h, i, *_: (b, i, h)),
        scratch_shapes=[pltpu.VMEM((blk, FOX_HEAD_DIM), F32)],
    )
    return pl.pallas_call(
        functools.partial(_fox_attn_kernel, blk=blk, n_sub=n_sub, tk=tk),
        grid_spec=grid_spec,
        out_shape=jax.ShapeDtypeStruct((b_sz, s_len, D_MODEL), BF16),
        compiler_params=_params("parallel", "parallel", "arbitrary"),
        name="fox_attn",
    )(ends, firsts, slack, qkvo, qkvo, qkvo, qkvo, ck)


def _pad_rows(w, n):
    return jnp.pad(w, ((0, n - w.shape[0]), (0, 0)))


def kernel(x, norm_mix, norm_ffn, gla_w_in, gla_w_g2, gla_b_g2, gla_o_gain, gla_w_o,
           fox_w_in, fox_b_f, fox_q_gain, fox_k_gain, fox_w_o,
           ffn_w_gate, ffn_w_up, ffn_w_down):
    b_sz, s_len, d = x.shape
    assert d == D_MODEL and s_len % GLA_CHUNK == 0
    m = b_sz * s_len
    tm = min(1024, m)
    tn = 1024
    tm_res = min(512, m)
    tm_ffn = min(1024, m)
    tf = 256
    gla_t = min(512, s_len)
    attn_blk = min(1024, s_len)
    attn_tk = min(512, s_len // 2)
    attn_sub = attn_tk
    cum_t = min(256, s_len)

    x2 = x.reshape(m, d)

    def ffn_layer(xin, layer):
        return _ffn(xin, norm_ffn[layer][None, :], ffn_w_gate, ffn_w_up, ffn_w_down, layer, tm_ffn, tf)

    gla_wt = jnp.swapaxes(gla_w_in, 1, 2)
    qkvr, la = _gla_proj(
        x2, norm_mix[0][None, :],
        gla_wt,
        _pad_rows(gla_wt[0, GLA_MAIN:, :], LANES),
        _pad_rows(gla_w_g2[0], LANES),
        gla_b_g2[0][None, :], tm, tn)
    og = _gla_mix(qkvr.reshape(b_sz, s_len, GLA_MAIN), la.reshape(b_sz, s_len, GLA_KEY_DIM),
                  gla_o_gain[0][None, :], b_sz, s_len, gla_t, 4)
    x2 = _residual_matmul(x2, og.reshape(m, GLA_VAL_DIM), gla_w_o, tm_res)
    x2 = ffn_layer(x2, 0)

    qk_gain = jnp.concatenate([jnp.tile(fox_q_gain[0], FOX_HEADS), jnp.tile(fox_k_gain[0], FOX_HEADS)])
    f_lo, f_hi = 3 * D_MODEL, 3 * D_MODEL + FOX_HEADS
    fox_wt = jnp.swapaxes(fox_w_in, 1, 2)
    qkvo, lf = _fox_proj(
        x2, norm_mix[1][None, :],
        _cast_weight_t(fox_wt, 0, f_lo),
        _cast_weight_t(fox_wt[:, f_hi:, :], 0, D_MODEL),
        _pad_rows(fox_wt[0, f_lo:f_hi, :], LANES),
        jnp.pad(fox_b_f[0], (0, LANES - FOX_HEADS))[None, :],
        qk_gain[None, :], tm, tn)
    c = _seq_cumsum(lf.reshape(b_sz, s_len, LANES), cum_t)
    c_hs = c[:, :, :FOX_HEADS].transpose(0, 2, 1)
    ck = c_hs.reshape(b_sz, FOX_HEADS, s_len // attn_tk, attn_tk)
    qk_max = (FOX_HEAD_DIM * FOX_SCALE * LOG2E * BF16_SLOP
              * jnp.max(jnp.abs(fox_q_gain[0])) * jnp.max(jnp.abs(fox_k_gain[0])))
    slack = (2.0 * qk_max + F32_EXP2_FLOOR).reshape(1)
    o = _fox_attn(qkvo.reshape(b_sz, s_len, FOX_MAIN), ck, slack, b_sz, s_len, attn_blk,
                  attn_blk // attn_sub)
    x2 = _residual_matmul(x2, o.reshape(m, D_MODEL), fox_w_o, tm_res)
    x2 = ffn_layer(x2, 1)
    return x2.reshape(b_sz, s_len, d)
```

```python
import functools

import jax
import jax.numpy as jnp
from jax import lax
from jax.experimental import pallas as pl
from jax.experimental.pallas import tpu as pltpu

F32 = jnp.float32
BF16 = jnp.bfloat16

D_MODEL = 2048
RMS_EPS = 1e-6

GLA_HEADS = 4
GLA_KEY_DIM = D_MODEL // 2
GLA_VAL_DIM = D_MODEL
GLA_HEAD_K = GLA_KEY_DIM // GLA_HEADS
GLA_HEAD_V = GLA_VAL_DIM // GLA_HEADS
GLA_GATE_RANK = 16
GLA_GATE_TEMP = 16.0
GLA_CHUNK = 64
GLA_MAIN = 2 * GLA_KEY_DIM + 2 * GLA_VAL_DIM
GLA_COARSE_LEVELS = (64, 32)
GLA_FINE_LEVELS = (16, 8)
GLA_DIAG = 4

FOX_HEAD_DIM = 128
FOX_HEADS = D_MODEL // FOX_HEAD_DIM
FOX_MAIN = 4 * D_MODEL
FOX_SCALE = FOX_HEAD_DIM ** -0.5
LOG2E = 1.4426950408889634
BF16_SLOP = 1.02
F32_EXP2_FLOOR = 160.0

D_FF = ((8 * D_MODEL + 2) // 3 + 255) // 256 * 256

LANES = 128
SUBLANES = 8
NORM_SLAB = 256
CAST_COLS = 512
VMEM_LIMIT = 58 * 1024 * 1024

_DONE = object()


def _lockstep(chains):
    pending = list(chains)
    while pending:
        pending = [g for g in pending if next(g, _DONE) is not _DONE]


NT_DIMS = (((1,), (1,)), ((), ()))
TN_DIMS = (((0,), (0,)), ((), ()))


def _params(*sem, flags=None):
    return pltpu.CompilerParams(dimension_semantics=sem, vmem_limit_bytes=VMEM_LIMIT, flags=flags)


def _rmsnorm_rows(x, g):
    ms = jnp.mean(x * x, axis=-1, keepdims=True)
    return x * lax.rsqrt(ms + RMS_EPS) * g


def _log_sigmoid(z):
    return jnp.minimum(z, 0.0) - jnp.log1p(jnp.exp(-jnp.abs(z)))


def _norm_to_scratch(x_ref, g_ref, h_ref):
    rows = x_ref.shape[0]
    slab = min(NORM_SLAB, rows)
    g = g_ref[...]
    for r0 in range(0, rows, slab):
        h_ref[r0:r0 + slab, :] = _rmsnorm_rows(x_ref[r0:r0 + slab, :], g).astype(BF16)


def _split3(a):
    hi = a.astype(BF16)
    r1 = a - hi.astype(F32)
    mid = r1.astype(BF16)
    lo = (r1 - mid.astype(F32)).astype(BF16)
    return hi, mid, lo


def _cumsum_rows(tril, a):
    hi, mid, lo = _split3(a)
    out = jnp.dot(tril, hi, preferred_element_type=F32)
    out += jnp.dot(tril, mid, preferred_element_type=F32)
    out += jnp.dot(tril, lo, preferred_element_type=F32)
    return out


def _cast_t_kernel(wt_ref, o_ref):
    o_ref[...] = wt_ref[0].T.astype(BF16)


def _cast_weight_t(wt3, layer, n_cols):
    _, _, k_dim = wt3.shape
    tc = CAST_COLS
    assert n_cols % tc == 0
    return pl.pallas_call(
        _cast_t_kernel,
        grid=(n_cols // tc,),
        in_specs=[pl.BlockSpec((1, tc, k_dim), lambda i: (layer, i, 0))],
        out_specs=pl.BlockSpec((k_dim, tc), lambda i: (0, i)),
        out_shape=jax.ShapeDtypeStruct((k_dim, n_cols), BF16),
        compiler_params=_params("parallel"),
        name="cast_weight_t",
    )(wt3)


def _gla_proj_kernel(x_ref, g_ref, w_ref, wg1_ref, wg2_ref, bg_ref, o_ref, la_ref, h_ref):
    @pl.when(pl.program_id(1) == 0)
    def _():
        _norm_to_scratch(x_ref, g_ref, h_ref)
        h = h_ref[...]
        g1 = lax.dot_general(h, wg1_ref[...].astype(BF16), NT_DIMS, preferred_element_type=F32)
        z = jnp.dot(g1.astype(BF16), wg2_ref[...].astype(BF16),
                    preferred_element_type=F32) + bg_ref[...]
        la_ref[...] = _log_sigmoid(z) * (LOG2E / GLA_GATE_TEMP)

    o_ref[...] = lax.dot_general(h_ref[...], w_ref[0].astype(BF16), NT_DIMS,
                                 preferred_element_type=F32).astype(o_ref.dtype)


def _gla_proj(x2, g, wt3, wg1, wg2, bg, tm, tn):
    m = x2.shape[0]
    return pl.pallas_call(
        _gla_proj_kernel,
        grid=(m // tm, GLA_MAIN // tn),
        in_specs=[
            pl.BlockSpec((tm, D_MODEL), lambda i, j: (i, 0)),
            pl.BlockSpec((1, D_MODEL), lambda i, j: (0, 0)),
            pl.BlockSpec((1, tn, D_MODEL), lambda i, j: (0, j, 0)),
            pl.BlockSpec((LANES, D_MODEL), lambda i, j: (0, 0)),
            pl.BlockSpec((LANES, GLA_KEY_DIM), lambda i, j: (0, 0)),
            pl.BlockSpec((1, GLA_KEY_DIM), lambda i, j: (0, 0)),
        ],
        out_specs=[
            pl.BlockSpec((tm, tn), lambda i, j: (i, j)),
            pl.BlockSpec((tm, GLA_KEY_DIM), lambda i, j: (i, 0)),
        ],
        out_shape=[
            jax.ShapeDtypeStruct((m, GLA_MAIN), BF16),
            jax.ShapeDtypeStruct((m, GLA_KEY_DIM), F32),
        ],
        scratch_shapes=[pltpu.VMEM((tm, D_MODEL), BF16)],
        compiler_params=_params("parallel", "arbitrary"),
        name="gla_proj",
    )(x2, g, wt3, wg1, wg2, bg)


def _gla_kernel(q_ref, k_ref, v_ref, r_ref, la_ref, gain_ref, o_ref, st_ref, *, n_chunks, heads):
    c_len = GLA_CHUNK

    @pl.when(pl.program_id(2) == 0)
    def _():
        st_ref[...] = jnp.zeros_like(st_ref)

    row = lax.broadcasted_iota(jnp.int32, (c_len, c_len), 0)
    col = lax.broadcasted_iota(jnp.int32, (c_len, c_len), 1)
    tril = (col <= row).astype(BF16)
    delta = row - col
    band = jnp.where((delta >= 0) & (delta <= row % GLA_DIAG), delta, -1)
    fine_masks = [
        (blk, (row // blk == col // blk) & (row % blk >= blk // 2) & (col % blk < blk // 2))
        for blk in GLA_FINE_LEVELS]
    half_rows = c_len // 2
    hrow = lax.broadcasted_iota(jnp.int32, (half_rows, half_rows), 0)
    hcol = lax.broadcasted_iota(jnp.int32, (half_rows, half_rows), 1)
    gain = gain_ref[...]
    scale = GLA_HEAD_K ** -0.5

    def gather_rows(a, blk, second_half):
        half = blk // 2
        off = half if second_half else 0
        return jnp.concatenate([a[s + off:s + off + half] for s in range(0, c_len, blk)], axis=0)

    def head_chunk(hd, sl):
        ksl = slice(hd * GLA_HEAD_K, (hd + 1) * GLA_HEAD_K)
        vsl = slice(hd * GLA_HEAD_V, (hd + 1) * GLA_HEAD_V)
        la = la_ref[0, sl, ksl]
        b = _cumsum_rows(tril, la)
        yield
        b_last = b[c_len - 1:c_len, :]
        qf = q_ref[0, sl, ksl].astype(F32)
        kf = k_ref[0, sl, ksl].astype(F32)
        v = v_ref[0, sl, vsl]

        st = st_ref[hd]
        qi = (qf * jnp.exp2(b)).astype(BF16)
        o = lax.dot_general(qi, st.astype(BF16), NT_DIMS, preferred_element_type=F32)
        kd = (kf * jnp.exp2(b_last - b)).astype(BF16)
        st_ref[hd] = st * jnp.exp2(b_last) + lax.dot_general(
            v, kd, TN_DIMS, preferred_element_type=F32)
        yield

        parts = []
        for blk in GLA_COARSE_LEVELS:
            half = blk // 2
            ref = jnp.concatenate(
                [jnp.broadcast_to(b[s + half - 1:s + half, :], (half, GLA_HEAD_K))
                 for s in range(0, c_len, blk)], axis=0)
            ql = (gather_rows(qf, blk, True) * jnp.exp2(gather_rows(b, blk, True) - ref)).astype(BF16)
            kl = (gather_rows(kf, blk, False) * jnp.exp2(ref - gather_rows(b, blk, False))).astype(BF16)
            a_l = lax.dot_general(ql, kl, NT_DIMS, preferred_element_type=F32)
            yield
            if blk < c_len:
                a_l = jnp.where(hrow // half == hcol // half, a_l, 0.0)
            parts.append((blk, jnp.dot(a_l.astype(BF16), gather_rows(v, blk, False),
                                       preferred_element_type=F32)))
            yield

        attn = jnp.zeros((c_len, c_len), F32)
        for blk, mk in fine_masks:
            half = blk // 2
            ref = jnp.concatenate(
                [jnp.broadcast_to(b[s + half - 1:s + half, :], (blk, GLA_HEAD_K))
                 for s in range(0, c_len, blk)], axis=0)
            ql = (qf * jnp.exp2(b - ref)).astype(BF16)
            kl = (kf * jnp.exp2(ref - b)).astype(BF16)
            a_l = lax.dot_general(ql, kl, NT_DIMS, preferred_element_type=F32)
            yield
            attn = jnp.where(mk, a_l, attn)
        for d in range(GLA_DIAG):
            if d:
                prod = qf * pltpu.roll(kf, d, 0) * jnp.exp2(b - pltpu.roll(b, d, 0))
            else:
                prod = qf * kf
            attn = jnp.where(band == d, jnp.sum(prod, axis=-1, keepdims=True), attn)
        o += jnp.dot(attn.astype(BF16), v, preferred_element_type=F32)
        yield

        groups = [o[g:g + SUBLANES] for g in range(0, c_len, SUBLANES)]
        for blk, part in parts:
            half = blk // 2
            src = 0
            for s in range(0, c_len, blk):
                for g in range((s + half) // SUBLANES, (s + blk) // SUBLANES):
                    groups[g] = groups[g] + part[src:src + SUBLANES]
                    src += SUBLANES
        o = jnp.concatenate(groups, axis=0)

        ms = jnp.mean(o * o, axis=-1, keepdims=True)
        factor = scale * lax.rsqrt(scale * scale * ms + RMS_EPS)
        r = r_ref[0, sl, vsl].astype(F32)
        o_ref[0, sl, vsl] = (o * factor * gain * (r * jax.nn.sigmoid(r))).astype(o_ref.dtype)

    def chunk(c, carry):
        sl = pl.ds(pl.multiple_of(c * c_len, c_len), c_len)
        _lockstep([head_chunk(hd, sl) for hd in range(heads)])
        return carry

    lax.fori_loop(0, n_chunks, chunk, 0)


def _gla_mix(qkvr, la, gain, b_sz, s_len, t_len, heads):
    n_chunks = t_len // GLA_CHUNK
    wk = heads * GLA_HEAD_K
    wv = heads * GLA_HEAD_V
    kq = GLA_KEY_DIM // wk
    kv = 2 * GLA_KEY_DIM // wv
    kr = kv + GLA_VAL_DIM // wv
    return pl.pallas_call(
        functools.partial(_gla_kernel, n_chunks=n_chunks, heads=heads),
        grid=(b_sz, GLA_HEADS // heads, s_len // t_len),
        in_specs=[
            pl.BlockSpec((1, t_len, wk), lambda b, h, t: (b, t, h)),
            pl.BlockSpec((1, t_len, wk), lambda b, h, t: (b, t, kq + h)),
            pl.BlockSpec((1, t_len, wv), lambda b, h, t: (b, t, kv + h)),
            pl.BlockSpec((1, t_len, wv), lambda b, h, t: (b, t, kr + h)),
            pl.BlockSpec((1, t_len, wk), lambda b, h, t: (b, t, h)),
            pl.BlockSpec((1, GLA_HEAD_V), lambda b, h, t: (0, 0)),
        ],
        out_specs=pl.BlockSpec((1, t_len, wv), lambda b, h, t: (b, t, h)),
        out_shape=jax.ShapeDtypeStruct((b_sz, s_len, GLA_VAL_DIM), BF16),
        scratch_shapes=[pltpu.VMEM((heads, GLA_HEAD_V, GLA_HEAD_K), F32)],
        compiler_params=_params("parallel", "parallel", "arbitrary"),
        name="gla_mix",
    )(qkvr, qkvr, qkvr, qkvr, la, gain)


def _residual_matmul_kernel(x_ref, a_ref, w_ref, o_ref, wb_ref):
    @pl.when(pl.program_id(0) == 0)
    def _():
        for r0 in range(0, w_ref.shape[1], NORM_SLAB):
            wb_ref[r0:r0 + NORM_SLAB, :] = w_ref[0, r0:r0 + NORM_SLAB, :].astype(BF16)

    o_ref[...] = x_ref[...] + jnp.dot(a_ref[...], wb_ref[...], preferred_element_type=F32)


def _residual_matmul(x2, a, w3, tm):
    m, k_dim = a.shape
    n = w3.shape[2]
    return pl.pallas_call(
        _residual_matmul_kernel,
        grid=(m // tm,),
        in_specs=[
            pl.BlockSpec((tm, n), lambda i: (i, 0)),
            pl.BlockSpec((tm, k_dim), lambda i: (i, 0)),
            pl.BlockSpec((1, k_dim, n), lambda i: (0, 0, 0), pipeline_mode=pl.Buffered(1)),
        ],
        out_specs=pl.BlockSpec((tm, n), lambda i: (i, 0)),
        out_shape=jax.ShapeDtypeStruct((m, n), F32),
        scratch_shapes=[pltpu.VMEM((k_dim, n), BF16)],
        compiler_params=_params("arbitrary"),
        name="residual_matmul",
    )(x2, a, w3)


def _ffn_kernel(x_ref, g_ref, wg_hbm, wu_hbm, wd_hbm, o_ref, h_ref, wg_buf, wu_buf, wd_buf, sem,
                *, layer, tf):
    i = pl.program_id(0)
    n_f = D_FF // tf
    assert n_f % 2 == 0

    def tile_copies(f, slot):
        cols = pl.ds(pl.multiple_of(f * tf, tf), tf)
        return (
            pltpu.make_async_copy(wg_hbm.at[layer, :, cols], wg_buf.at[slot], sem.at[0, slot]),
            pltpu.make_async_copy(wu_hbm.at[layer, :, cols], wu_buf.at[slot], sem.at[1, slot]),
            pltpu.make_async_copy(wd_hbm.at[layer, cols, :], wd_buf.at[slot], sem.at[2, slot]),
        )

    @pl.when(i == 0)
    def _():
        for cp in tile_copies(0, 0):
            cp.start()

    _norm_to_scratch(x_ref, g_ref, h_ref)

    def use_tile(f, slot, acc_ref):
        for cp in tile_copies(f, slot):
            cp.wait()
        nxt = jnp.where(f + 1 == n_f, 0, f + 1)
        for cp in tile_copies(nxt, 1 - slot):
            cp.start()
        h = h_ref[...]
        gate = jnp.dot(h, wg_buf[slot].astype(BF16), preferred_element_type=F32)
        up = jnp.dot(h, wu_buf[slot].astype(BF16), preferred_element_type=F32)
        act = (gate * jax.nn.sigmoid(gate) * up).astype(BF16)
        o_ref[...] = acc_ref[...] + jnp.dot(act, wd_buf[slot].astype(BF16),
                                            preferred_element_type=F32)

    def pair(p, carry):
        use_tile(2 * p, 0, o_ref)
        use_tile(2 * p + 1, 1, o_ref)
        return carry

    use_tile(0, 0, x_ref)
    use_tile(1, 1, o_ref)
    lax.fori_loop(1, n_f // 2, pair, 0)

    @pl.when(i == pl.num_programs(0) - 1)
    def _():
        for cp in tile_copies(0, 0):
            cp.wait()


def _ffn(x2, g, wg, wu, wd, layer, tm, tf):
    m = x2.shape[0]
    return pl.pallas_call(
        functools.partial(_ffn_kernel, layer=layer, tf=tf),
        grid=(m // tm,),
        in_specs=[
            pl.BlockSpec((tm, D_MODEL), lambda i: (i, 0)),
            pl.BlockSpec((1, D_MODEL), lambda i: (0, 0)),
            pl.BlockSpec(memory_space=pl.ANY),
            pl.BlockSpec(memory_space=pl.ANY),
            pl.BlockSpec(memory_space=pl.ANY),
        ],
        out_specs=pl.BlockSpec((tm, D_MODEL), lambda i: (i, 0)),
        out_shape=jax.ShapeDtypeStruct((m, D_MODEL), F32),
        scratch_shapes=[
            pltpu.VMEM((tm, D_MODEL), BF16),
            pltpu.VMEM((2, D_MODEL, tf), F32),
            pltpu.VMEM((2, D_MODEL, tf), F32),
            pltpu.VMEM((2, tf, D_MODEL), F32),
            pltpu.SemaphoreType.DMA((3, 2)),
        ],
        compiler_params=_params("arbitrary"),
        name="ffn",
    )(x2, g, wg, wu, wd)


def _fox_proj_kernel(x_ref, g_ref, wa_ref, wb_ref, wf_ref, bf_ref, qkg_ref, o_ref, lf_ref, h_ref,
                     *, q_tiles, qk_tiles, a_tiles):
    j = pl.program_id(1)

    @pl.when(j == 0)
    def _():
        _norm_to_scratch(x_ref, g_ref, h_ref)
        h = h_ref[...]
        z = lax.dot_general(h, wf_ref[...].astype(BF16), NT_DIMS,
                            preferred_element_type=F32) + bf_ref[...]
        lf_ref[...] = _log_sigmoid(z)

    @pl.when(j < qk_tiles)
    def _():
        acc = jnp.dot(h_ref[...], wa_ref[...], preferred_element_type=F32)
        mult = jnp.where(j < q_tiles, FOX_SCALE * LOG2E, 1.0).astype(F32)
        for c in range(acc.shape[1] // FOX_HEAD_DIM):
            cs = slice(c * FOX_HEAD_DIM, (c + 1) * FOX_HEAD_DIM)
            xs = acc[:, cs]
            ms = jnp.mean(xs * xs, axis=-1, keepdims=True)
            o_ref[:, cs] = (xs * (lax.rsqrt(ms + RMS_EPS) * mult) * qkg_ref[:, cs]).astype(o_ref.dtype)

    @pl.when((j >= qk_tiles) & (j < a_tiles))
    def _():
        o_ref[...] = jnp.dot(h_ref[...], wa_ref[...], preferred_element_type=F32).astype(o_ref.dtype)

    @pl.when(j >= a_tiles)
    def _():
        o_ref[...] = jnp.dot(h_ref[...], wb_ref[...], preferred_element_type=F32).astype(o_ref.dtype)


def _fox_proj(x2, g, w_qkv, w_og, wf, bf, qk_gain, tm, tn):
    m = x2.shape[0]
    q_tiles = D_MODEL // tn
    qk_tiles = 2 * q_tiles
    a_tiles = w_qkv.shape[1] // tn
    return pl.pallas_call(
        functools.partial(_fox_proj_kernel, q_tiles=q_tiles, qk_tiles=qk_tiles, a_tiles=a_tiles),
        grid=(m // tm, FOX_MAIN // tn),
        in_specs=[
            pl.BlockSpec((tm, D_MODEL), lambda i, j: (i, 0)),
            pl.BlockSpec((1, D_MODEL), lambda i, j: (0, 0)),
            pl.BlockSpec((D_MODEL, tn), lambda i, j: (0, jnp.minimum(j, a_tiles - 1))),
            pl.BlockSpec((D_MODEL, tn), lambda i, j: (0, jnp.maximum(j - a_tiles, 0))),
            pl.BlockSpec((LANES, D_MODEL), lambda i, j: (0, 0)),
            pl.BlockSpec((1, LANES), lambda i, j: (0, 0)),
            pl.BlockSpec((1, tn), lambda i, j: (0, jnp.minimum(j, qk_tiles - 1))),
        ],
        out_specs=[
            pl.BlockSpec((tm, tn), lambda i, j: (i, j)),
            pl.BlockSpec((tm, LANES), lambda i, j: (i, 0)),
        ],
        out_shape=[
            jax.ShapeDtypeStruct((m, FOX_MAIN), BF16),
            jax.ShapeDtypeStruct((m, LANES), F32),
        ],
        scratch_shapes=[pltpu.VMEM((tm, D_MODEL), BF16)],
        compiler_params=_params("parallel", "arbitrary"),
        name="fox_proj",
    )(x2, g, w_qkv, w_og, wf, bf, qk_gain)


def _cumsum_kernel(lf_ref, c_ref, carry_ref):
    @pl.when(pl.program_id(1) == 0)
    def _():
        carry_ref[...] = jnp.zeros_like(carry_ref)

    t_len = lf_ref.shape[1]
    row = lax.broadcasted_iota(jnp.int32, (t_len, t_len), 0)
    col = lax.broadcasted_iota(jnp.int32, (t_len, t_len), 1)
    tril = (col <= row).astype(BF16)
    c = _cumsum_rows(tril, lf_ref[0]) + carry_ref[...]
    c_ref[0] = c * LOG2E
    carry_ref[...] = c[t_len - 1:t_len, :]


def _seq_cumsum(lf, t_len):
    b_sz, s_len, _ = lf.shape
    return pl.pallas_call(
        _cumsum_kernel,
        grid=(b_sz, s_len // t_len),
        in_specs=[pl.BlockSpec((1, t_len, LANES), lambda b, t: (b, t, 0))],
        out_specs=pl.BlockSpec((1, t_len, LANES), lambda b, t: (b, t, 0)),
        out_shape=jax.ShapeDtypeStruct(lf.shape, F32),
        scratch_shapes=[pltpu.VMEM((1, LANES), F32)],
        compiler_params=_params("parallel", "arbitrary"),
        name="fox_cumsum",
    )(lf)


def _fox_attn_kernel(ends_ref, firsts_ref, slack_ref, q_ref, k_ref, v_ref, og_ref, ck_ref, o_ref,
                     acc_ref, *, blk, n_sub, tk):
    i = pl.program_id(2)
    sub = blk // n_sub
    assert sub == tk
    n_kt = ck_ref.shape[2]
    bh = pl.program_id(0) * pl.num_programs(1) + pl.program_id(1)
    n_loop = jnp.maximum(i * n_sub - 1, 0)
    limit = firsts_ref[bh * pl.num_programs(2) + i] + slack_ref[0]

    def first_needed(j, lo):
        return jnp.minimum(lo, jnp.where(ends_ref[bh * n_kt + j] > limit, n_loop, j))

    j_start = lax.fori_loop(0, n_loop, first_needed, n_loop)

    c0 = ck_ref[0, 0, pl.ds(i * n_sub, 1), :][:, 0:1]
    acc_ref[...] = jnp.zeros_like(acc_ref)

    def update(r, out, carry, ks, vs, bias, mask_offset):
        m_old, l_old = carry
        rows = slice(r * sub, (r + 1) * sub)
        s = lax.dot_general(q_ref[0, rows, :], ks, NT_DIMS, preferred_element_type=F32) + bias
        yield
        if mask_offset is not None:
            row = lax.broadcasted_iota(jnp.int32, s.shape, 0)
            col = lax.broadcasted_iota(jnp.int32, s.shape, 1)
            s = jnp.where(col <= row + mask_offset, s, -jnp.inf)
        m_new = jnp.maximum(m_old, jnp.max(s, axis=-1, keepdims=True))
        p = jnp.exp2(s - m_new)
        alpha = jnp.exp2(m_old - m_new)
        l_new = alpha * l_old + jnp.sum(p, axis=-1, keepdims=True)
        acc_ref[rows, :] = alpha * acc_ref[rows, :] + jnp.dot(
            p.astype(BF16), vs, preferred_element_type=F32)
        out[r] = (m_new, l_new)

    def body(j, carry):
        sl = pl.ds(pl.multiple_of(j * tk, tk), tk)
        ks = k_ref[0, sl, :]
        vs = v_ref[0, sl, :]
        bias = c0 - ck_ref[0, 0, pl.ds(j, 1), :]
        out = [None] * n_sub
        _lockstep([update(r, out, carry[r], ks, vs, bias, None) for r in range(n_sub)])
        return tuple(out)

    init = tuple((jnp.full((sub, 1), -jnp.inf, F32), jnp.zeros((sub, 1), F32)) for _ in range(n_sub))
    carry = lax.fori_loop(j_start, n_loop, body, init)

    carry = list(carry)
    for r in range(1, n_sub):
        t_prev = i * n_sub + r - 1
        limit_r = ends_ref[bh * n_kt + t_prev] + slack_ref[0]
        for u in range(r):
            g = i * n_sub - 1 + u

            def visit(g=g, r=r):
                out = [None] * n_sub
                sl = pl.ds(pl.multiple_of(g * tk, tk), tk)
                bias = c0 - ck_ref[0, 0, pl.ds(g, 1), :]
                _lockstep([update(r, out, carry[r], k_ref[0, sl, :], v_ref[0, sl, :], bias, None)])
                return out[r]

            needed = (i > 0) & (ends_ref[bh * n_kt + jnp.maximum(g, 0)] <= limit_r)
            carry[r] = lax.cond(needed, visit, lambda r=r: carry[r])

    final = [None] * n_sub
    steps = []
    for r in range(n_sub):
        t_own = i * n_sub + r
        t_lo = jnp.maximum(t_own - 1, 0)
        win = pl.ds(pl.multiple_of(t_lo * tk, tk), 2 * tk)
        ck_win = jnp.concatenate(
            [ck_ref[0, 0, pl.ds(t_lo, 1), :], ck_ref[0, 0, pl.ds(t_lo + 1, 1), :]], axis=1)
        steps.append(update(r, final, carry[r], k_ref[0, win, :], v_ref[0, win, :],
                            c0 - ck_win, (t_own - t_lo) * tk))
    _lockstep(steps)
    for r in range(n_sub):
        rows = slice(r * sub, (r + 1) * sub)
        gate = jax.nn.sigmoid(og_ref[0, rows, :].astype(F32))
        o_ref[0, rows, :] = (acc_ref[rows, :] / final[r][1] * gate).astype(o_ref.dtype)


def _fox_attn(qkvo, ck, slack, b_sz, s_len, blk, n_sub):
    h_cols = D_MODEL // FOX_HEAD_DIM
    n_blk = s_len // blk
    n_kt, tk = ck.shape[2:]
    ends = ck[:, :, :, tk - 1].reshape(-1)
    firsts = ck.reshape(b_sz, FOX_HEADS, n_blk, blk)[:, :, :, 0].reshape(-1)
    grid_spec = pltpu.PrefetchScalarGridSpec(
        num_scalar_prefetch=3,
        grid=(b_sz, FOX_HEADS, n_blk),
        in_specs=[
            pl.BlockSpec((1, blk, FOX_HEAD_DIM), lambda b, h, i, *_: (b, i, h)),
            pl.BlockSpec((1, s_len, FOX_HEAD_DIM), lambda b, h, i, *_: (b, 0, h_cols + h)),
            pl.BlockSpec((1, s_len, FOX_HEAD_DIM), lambda b, h, i, *_: (b, 0, 2 * h_cols + h)),
            pl.BlockSpec((1, blk, FOX_HEAD_DIM), lambda b, h, i, *_: (b, i, 3 * h_cols + h)),
            pl.BlockSpec((1, 1, n_kt, tk), lambda b, h, i, *_: (b, h, 0, 0)),
        ],
        out_specs=pl.BlockSpec((1, blk, FOX_HEAD_DIM), lambda b, h, i, *_: (b, i, h)),
        scratch_shapes=[pltpu.VMEM((blk, FOX_HEAD_DIM), F32)],
    )
    return pl.pallas_call(
        functools.partial(_fox_attn_kernel, blk=blk, n_sub=n_sub, tk=tk),
        grid_spec=grid_spec,
        out_shape=jax.ShapeDtypeStruct((b_sz, s_len, D_MODEL), BF16),
        compiler_params=_params("parallel", "parallel", "arbitrary"),
        name="fox_attn",
    )(ends, firsts, slack, qkvo, qkvo, qkvo, qkvo, ck)


def _pad_rows(w, n):
    return jnp.pad(w, ((0, n - w.shape[0]), (0, 0)))


def kernel(x, norm_mix, norm_ffn, gla_w_in, gla_w_g2, gla_b_g2, gla_o_gain, gla_w_o,
           fox_w_in, fox_b_f, fox_q_gain, fox_k_gain, fox_w_o,
           ffn_w_gate, ffn_w_up, ffn_w_down):
    b_sz, s_len, d = x.shape
    assert d == D_MODEL and s_len % GLA_CHUNK == 0
    m = b_sz * s_len
    tm = min(1024, m)
    tn = 1024
    tm_res = min(512, m)
    tm_ffn = min(1024, m)
    tf = 256
    gla_t = min(512, s_len)
    attn_blk = min(1024, s_len)
    attn_tk = min(512, s_len // 2)
    attn_sub = attn_tk
    cum_t = min(256, s_len)

    x2 = x.reshape(m, d)

    def ffn_layer(xin, layer):
        return _ffn(xin, norm_ffn[layer][None, :], ffn_w_gate, ffn_w_up, ffn_w_down, layer, tm_ffn, tf)

    gla_wt = jnp.swapaxes(gla_w_in, 1, 2)
    qkvr, la = _gla_proj(
        x2, norm_mix[0][None, :],
        gla_wt,
        _pad_rows(gla_wt[0, GLA_MAIN:, :], LANES),
        _pad_rows(gla_w_g2[0], LANES),
        gla_b_g2[0][None, :], tm, tn)
    og = _gla_mix(qkvr.reshape(b_sz, s_len, GLA_MAIN), la.reshape(b_sz, s_len, GLA_KEY_DIM),
                  gla_o_gain[0][None, :], b_sz, s_len, gla_t, 4)
    x2 = _residual_matmul(x2, og.reshape(m, GLA_VAL_DIM), gla_w_o, tm_res)
    x2 = ffn_layer(x2, 0)

    qk_gain = jnp.concatenate([jnp.tile(fox_q_gain[0], FOX_HEADS), jnp.tile(fox_k_gain[0], FOX_HEADS)])
    f_lo, f_hi = 3 * D_MODEL, 3 * D_MODEL + FOX_HEADS
    fox_wt = jnp.swapaxes(fox_w_in, 1, 2)
    qkvo, lf = _fox_proj(
        x2, norm_mix[1][None, :],
        _cast_weight_t(fox_wt, 0, f_lo),
        _cast_weight_t(fox_wt[:, f_hi:, :], 0, D_MODEL),
        _pad_rows(fox_wt[0, f_lo:f_hi, :], LANES),
        jnp.pad(fox_b_f[0], (0, LANES - FOX_HEADS))[None, :],
        qk_gain[None, :], tm, tn)
    c = _seq_cumsum(lf.reshape(b_sz, s_len, LANES), cum_t)
    c_hs = c[:, :, :FOX_HEADS].transpose(0, 2, 1)
    ck = c_hs.reshape(b_sz, FOX_HEADS, s_len // attn_tk, attn_tk)
    qk_max = (FOX_HEAD_DIM * FOX_SCALE * LOG2E * BF16_SLOP
              * jnp.max(jnp.abs(fox_q_gain[0])) * jnp.max(jnp.abs(fox_k_gain[0])))
    slack = (2.0 * qk_max + F32_EXP2_FLOOR).reshape(1)
    o = _fox_attn(qkvo.reshape(b_sz, s_len, FOX_MAIN), ck, slack, b_sz, s_len, attn_blk,
                  attn_blk // attn_sub)
    x2 = _residual_matmul(x2, o.reshape(m, D_MODEL), fox_w_o, tm_res)
    x2 = ffn_layer(x2, 1)
    return x2.reshape(b_sz, s_len, d)
```

```python
import functools

import jax
import jax.numpy as jnp
from jax import lax
from jax.experimental import pallas as pl
from jax.experimental.pallas import tpu as pltpu

F32 = jnp.float32
BF16 = jnp.bfloat16

D_MODEL = 2048
RMS_EPS = 1e-6

GLA_HEADS = 4
GLA_KEY_DIM = D_MODEL // 2
GLA_VAL_DIM = D_MODEL
GLA_HEAD_K = GLA_KEY_DIM // GLA_HEADS
GLA_HEAD_V = GLA_VAL_DIM // GLA_HEADS
GLA_GATE_RANK = 16
GLA_GATE_TEMP = 16.0
GLA_CHUNK = 64
GLA_MAIN = 2 * GLA_KEY_DIM + 2 * GLA_VAL_DIM
GLA_COARSE_LEVELS = (64, 32)
GLA_FINE_LEVELS = (16, 8)
GLA_DIAG = 4
GLA_CHUNKS_PER_ITER = 2

FOX_HEAD_DIM = 128
FOX_HEADS = D_MODEL // FOX_HEAD_DIM
FOX_MAIN = 4 * D_MODEL
FOX_SCALE = FOX_HEAD_DIM ** -0.5
LOG2E = 1.4426950408889634
BF16_SLOP = 1.02
F32_EXP2_FLOOR = 160.0

D_FF = ((8 * D_MODEL + 2) // 3 + 255) // 256 * 256

LANES = 128
SUBLANES = 8
NORM_SLAB = 256
CAST_COLS = 512
VMEM_LIMIT = 58 * 1024 * 1024

_DONE = object()


def _lockstep(chains):
    pending = list(chains)
    while pending:
        pending = [g for g in pending if next(g, _DONE) is not _DONE]


NT_DIMS = (((1,), (1,)), ((), ()))
TN_DIMS = (((0,), (0,)), ((), ()))


def _params(*sem, flags=None):
    return pltpu.CompilerParams(dimension_semantics=sem, vmem_limit_bytes=VMEM_LIMIT, flags=flags)


def _rmsnorm_rows(x, g):
    ms = jnp.mean(x * x, axis=-1, keepdims=True)
    return x * lax.rsqrt(ms + RMS_EPS) * g


def _log_sigmoid(z):
    return jnp.minimum(z, 0.0) - jnp.log(1.0 + jnp.exp(-jnp.abs(z)))


def _norm_to_scratch(x_ref, g_ref, h_ref):
    rows = x_ref.shape[0]
    slab = min(NORM_SLAB, rows)
    g = g_ref[...]
    for r0 in range(0, rows, slab):
        h_ref[r0:r0 + slab, :] = _rmsnorm_rows(x_ref[r0:r0 + slab, :], g).astype(BF16)


def _split3(a):
    hi = a.astype(BF16)
    r1 = a - hi.astype(F32)
    mid = r1.astype(BF16)
    lo = (r1 - mid.astype(F32)).astype(BF16)
    return hi, mid, lo


def _cumsum_rows(tril, a):
    hi, mid, lo = _split3(a)
    out = jnp.dot(tril, hi, preferred_element_type=F32)
    out += jnp.dot(tril, mid, preferred_element_type=F32)
    out += jnp.dot(tril, lo, preferred_element_type=F32)
    return out


def _cast_t_kernel(wt_ref, o_ref):
    o_ref[...] = wt_ref[0].T.astype(BF16)


def _cast_weight_t(wt3, layer, n_cols):
    _, _, k_dim = wt3.shape
    tc = CAST_COLS
    assert n_cols % tc == 0
    return pl.pallas_call(
        _cast_t_kernel,
        grid=(n_cols // tc,),
        in_specs=[pl.BlockSpec((1, tc, k_dim), lambda i: (layer, i, 0))],
        out_specs=pl.BlockSpec((k_dim, tc), lambda i: (0, i)),
        out_shape=jax.ShapeDtypeStruct((k_dim, n_cols), BF16),
        compiler_params=_params("parallel"),
        name="cast_weight_t",
    )(wt3)


def _gla_proj_kernel(x_ref, g_ref, w_ref, wg1_ref, wg2_ref, bg_ref, o_ref, la_ref, h_ref):
    @pl.when(pl.program_id(1) == 0)
    def _():
        _norm_to_scratch(x_ref, g_ref, h_ref)
        h = h_ref[...]
        g1 = lax.dot_general(h, wg1_ref[...].astype(BF16), NT_DIMS, preferred_element_type=F32)
        z = jnp.dot(g1.astype(BF16), wg2_ref[...].astype(BF16),
                    preferred_element_type=F32) + bg_ref[...]
        la_ref[...] = _log_sigmoid(z) * (LOG2E / GLA_GATE_TEMP)

    o_ref[...] = lax.dot_general(h_ref[...], w_ref[0].astype(BF16), NT_DIMS,
                                 preferred_element_type=F32).astype(o_ref.dtype)


def _gla_proj(x2, g, wt3, wg1, wg2, bg, tm, tn):
    m = x2.shape[0]
    return pl.pallas_call(
        _gla_proj_kernel,
        grid=(m // tm, GLA_MAIN // tn),
        in_specs=[
            pl.BlockSpec((tm, D_MODEL), lambda i, j: (i, 0)),
            pl.BlockSpec((1, D_MODEL), lambda i, j: (0, 0)),
            pl.BlockSpec((1, tn, D_MODEL), lambda i, j: (0, j, 0)),
            pl.BlockSpec((LANES, D_MODEL), lambda i, j: (0, 0)),
            pl.BlockSpec((LANES, GLA_KEY_DIM), lambda i, j: (0, 0)),
            pl.BlockSpec((1, GLA_KEY_DIM), lambda i, j: (0, 0)),
        ],
        out_specs=[
            pl.BlockSpec((tm, tn), lambda i, j: (i, j)),
            pl.BlockSpec((tm, GLA_KEY_DIM), lambda i, j: (i, 0)),
        ],
        out_shape=[
            jax.ShapeDtypeStruct((m, GLA_MAIN), BF16),
            jax.ShapeDtypeStruct((m, GLA_KEY_DIM), F32),
        ],
        scratch_shapes=[pltpu.VMEM((tm, D_MODEL), BF16)],
        compiler_params=_params("parallel", "arbitrary"),
        name="gla_proj",
    )(x2, g, wt3, wg1, wg2, bg)


def _gla_kernel(q_ref, k_ref, v_ref, r_ref, la_ref, gain_ref, o_ref, st_ref, *, n_chunks, heads):
    c_len = GLA_CHUNK

    @pl.when(pl.program_id(2) == 0)
    def _():
        st_ref[...] = jnp.zeros_like(st_ref)

    row = lax.broadcasted_iota(jnp.int32, (c_len, c_len), 0)
    col = lax.broadcasted_iota(jnp.int32, (c_len, c_len), 1)
    tril = (col <= row).astype(BF16)
    delta = row - col
    band = jnp.where((delta >= 0) & (delta <= row % GLA_DIAG), delta, -1)
    fine_masks = [
        (blk, (row // blk == col // blk) & (row % blk >= blk // 2) & (col % blk < blk // 2))
        for blk in GLA_FINE_LEVELS]
    half_rows = c_len // 2
    hrow = lax.broadcasted_iota(jnp.int32, (half_rows, half_rows), 0)
    hcol = lax.broadcasted_iota(jnp.int32, (half_rows, half_rows), 1)
    gain = gain_ref[...]
    scale = GLA_HEAD_K ** -0.5

    def gather_rows(a, blk, second_half):
        half = blk // 2
        off = half if second_half else 0
        return jnp.concatenate([a[s + off:s + off + half] for s in range(0, c_len, blk)], axis=0)

    def head_chunk(hd, sl):
        ksl = slice(hd * GLA_HEAD_K, (hd + 1) * GLA_HEAD_K)
        vsl = slice(hd * GLA_HEAD_V, (hd + 1) * GLA_HEAD_V)
        la = la_ref[0, sl, ksl]
        b = _cumsum_rows(tril, la)
        yield
        b_last = b[c_len - 1:c_len, :]
        qf = q_ref[0, sl, ksl].astype(F32)
        kf = k_ref[0, sl, ksl].astype(F32)
        v = v_ref[0, sl, vsl]

        st = st_ref[hd]
        qi = (qf * jnp.exp2(b)).astype(BF16)
        o = lax.dot_general(qi, st.astype(BF16), NT_DIMS, preferred_element_type=F32)
        kd = (kf * jnp.exp2(b_last - b)).astype(BF16)
        st_ref[hd] = st * jnp.exp2(b_last) + lax.dot_general(
            v, kd, TN_DIMS, preferred_element_type=F32)
        yield

        parts = []
        for blk in GLA_COARSE_LEVELS:
            half = blk // 2
            ref = jnp.concatenate(
                [jnp.broadcast_to(b[s + half - 1:s + half, :], (half, GLA_HEAD_K))
                 for s in range(0, c_len, blk)], axis=0)
            ql = (gather_rows(qf, blk, True) * jnp.exp2(gather_rows(b, blk, True) - ref)).astype(BF16)
            kl = (gather_rows(kf, blk, False) * jnp.exp2(ref - gather_rows(b, blk, False))).astype(BF16)
            a_l = lax.dot_general(ql, kl, NT_DIMS, preferred_element_type=F32)
            yield
            if blk < c_len:
                a_l = jnp.where(hrow // half == hcol // half, a_l, 0.0)
            parts.append((blk, jnp.dot(a_l.astype(BF16), gather_rows(v, blk, False),
                                       preferred_element_type=F32)))
            yield

        attn = jnp.zeros((c_len, c_len), F32)
        for blk, mk in fine_masks:
            half = blk // 2
            ref = jnp.concatenate(
                [jnp.broadcast_to(b[s + half - 1:s + half, :], (blk, GLA_HEAD_K))
                 for s in range(0, c_len, blk)], axis=0)
            ql = (qf * jnp.exp2(b - ref)).astype(BF16)
            kl = (kf * jnp.exp2(ref - b)).astype(BF16)
            a_l = lax.dot_general(ql, kl, NT_DIMS, preferred_element_type=F32)
            yield
            attn = jnp.where(mk, a_l, attn)
        for d in range(GLA_DIAG):
            if d:
                prod = qf * pltpu.roll(kf, d, 0) * jnp.exp2(b - pltpu.roll(b, d, 0))
            else:
                prod = qf * kf
            attn = jnp.where(band == d, jnp.sum(prod, axis=-1, keepdims=True), attn)
        o += jnp.dot(attn.astype(BF16), v, preferred_element_type=F32)
        yield

        groups = [o[g:g + SUBLANES] for g in range(0, c_len, SUBLANES)]
        for blk, part in parts:
            half = blk // 2
            src = 0
            for s in range(0, c_len, blk):
                for g in range((s + half) // SUBLANES, (s + blk) // SUBLANES):
                    groups[g] = groups[g] + part[src:src + SUBLANES]
                    src += SUBLANES
        o = jnp.concatenate(groups, axis=0)

        ms = jnp.mean(o * o, axis=-1, keepdims=True)
        factor = scale * lax.rsqrt(scale * scale * ms + RMS_EPS)
        r = r_ref[0, sl, vsl].astype(F32)
        o_ref[0, sl, vsl] = (o * factor * gain * (r * jax.nn.sigmoid(r))).astype(o_ref.dtype)

    def chunk_group(c, carry):
        chains = []
        for u in range(GLA_CHUNKS_PER_ITER):
            sl = pl.ds(pl.multiple_of((c * GLA_CHUNKS_PER_ITER + u) * c_len, c_len), c_len)
            chains += [head_chunk(hd, sl) for hd in range(heads)]
        _lockstep(chains)
        return carry

    assert n_chunks % GLA_CHUNKS_PER_ITER == 0
    lax.fori_loop(0, n_chunks // GLA_CHUNKS_PER_ITER, chunk_group, 0)


def _gla_mix(qkvr, la, gain, b_sz, s_len, t_len, heads):
    n_chunks = t_len // GLA_CHUNK
    wk = heads * GLA_HEAD_K
    wv = heads * GLA_HEAD_V
    kq = GLA_KEY_DIM // wk
    kv = 2 * GLA_KEY_DIM // wv
    kr = kv + GLA_VAL_DIM // wv
    return pl.pallas_call(
        functools.partial(_gla_kernel, n_chunks=n_chunks, heads=heads),
        grid=(b_sz, GLA_HEADS // heads, s_len // t_len),
        in_specs=[
            pl.BlockSpec((1, t_len, wk), lambda b, h, t: (b, t, h)),
            pl.BlockSpec((1, t_len, wk), lambda b, h, t: (b, t, kq + h)),
            pl.BlockSpec((1, t_len, wv), lambda b, h, t: (b, t, kv + h)),
            pl.BlockSpec((1, t_len, wv), lambda b, h, t: (b, t, kr + h)),
            pl.BlockSpec((1, t_len, wk), lambda b, h, t: (b, t, h)),
            pl.BlockSpec((1, GLA_HEAD_V), lambda b, h, t: (0, 0)),
        ],
        out_specs=pl.BlockSpec((1, t_len, wv), lambda b, h, t: (b, t, h)),
        out_shape=jax.ShapeDtypeStruct((b_sz, s_len, GLA_VAL_DIM), BF16),
        scratch_shapes=[pltpu.VMEM((heads, GLA_HEAD_V, GLA_HEAD_K), F32)],
        compiler_params=_params("parallel", "parallel", "arbitrary"),
        name="gla_mix",
    )(qkvr, qkvr, qkvr, qkvr, la, gain)


def _residual_matmul_kernel(x_ref, a_ref, w_ref, o_ref, wb_ref):
    @pl.when(pl.program_id(0) == 0)
    def _():
        for r0 in range(0, w_ref.shape[1], NORM_SLAB):
            wb_ref[r0:r0 + NORM_SLAB, :] = w_ref[0, r0:r0 + NORM_SLAB, :].astype(BF16)

    o_ref[...] = x_ref[...] + jnp.dot(a_ref[...], wb_ref[...], preferred_element_type=F32)


def _residual_matmul(x2, a, w3, tm):
    m, k_dim = a.shape
    n = w3.shape[2]
    return pl.pallas_call(
        _residual_matmul_kernel,
        grid=(m // tm,),
        in_specs=[
            pl.BlockSpec((tm, n), lambda i: (i, 0)),
            pl.BlockSpec((tm, k_dim), lambda i: (i, 0)),
            pl.BlockSpec((1, k_dim, n), lambda i: (0, 0, 0), pipeline_mode=pl.Buffered(1)),
        ],
        out_specs=pl.BlockSpec((tm, n), lambda i: (i, 0)),
        out_shape=jax.ShapeDtypeStruct((m, n), F32),
        scratch_shapes=[pltpu.VMEM((k_dim, n), BF16)],
        compiler_params=_params("arbitrary"),
        name="residual_matmul",
    )(x2, a, w3)


def _ffn_kernel(x_ref, g_ref, wg_hbm, wu_hbm, wd_hbm, o_ref, h_ref, wg_buf, wu_buf, wd_buf, sem,
                *, layer, tf):
    i = pl.program_id(0)
    n_f = D_FF // tf
    assert n_f % 2 == 0

    def tile_copies(f, slot):
        cols = pl.ds(pl.multiple_of(f * tf, tf), tf)
        return (
            pltpu.make_async_copy(wg_hbm.at[layer, :, cols], wg_buf.at[slot], sem.at[0, slot]),
            pltpu.make_async_copy(wu_hbm.at[layer, :, cols], wu_buf.at[slot], sem.at[1, slot]),
            pltpu.make_async_copy(wd_hbm.at[layer, cols, :], wd_buf.at[slot], sem.at[2, slot]),
        )

    @pl.when(i == 0)
    def _():
        for cp in tile_copies(0, 0):
            cp.start()

    _norm_to_scratch(x_ref, g_ref, h_ref)

    def use_tile(f, slot, acc_ref):
        for cp in tile_copies(f, slot):
            cp.wait()
        nxt = jnp.where(f + 1 == n_f, 0, f + 1)
        for cp in tile_copies(nxt, 1 - slot):
            cp.start()
        h = h_ref[...]
        gate = jnp.dot(h, wg_buf[slot].astype(BF16), preferred_element_type=F32)
        up = jnp.dot(h, wu_buf[slot].astype(BF16), preferred_element_type=F32)
        act = (gate * jax.nn.sigmoid(gate) * up).astype(BF16)
        o_ref[...] = acc_ref[...] + jnp.dot(act, wd_buf[slot].astype(BF16),
                                            preferred_element_type=F32)

    def pair(p, carry):
        use_tile(2 * p, 0, o_ref)
        use_tile(2 * p + 1, 1, o_ref)
        return carry

    use_tile(0, 0, x_ref)
    use_tile(1, 1, o_ref)
    lax.fori_loop(1, n_f // 2, pair, 0)

    @pl.when(i == pl.num_programs(0) - 1)
    def _():
        for cp in tile_copies(0, 0):
            cp.wait()


def _ffn(x2, g, wg, wu, wd, layer, tm, tf):
    m = x2.shape[0]
    return pl.pallas_call(
        functools.partial(_ffn_kernel, layer=layer, tf=tf),
        grid=(m // tm,),
        in_specs=[
            pl.BlockSpec((tm, D_MODEL), lambda i: (i, 0)),
            pl.BlockSpec((1, D_MODEL), lambda i: (0, 0)),
            pl.BlockSpec(memory_space=pl.ANY),
            pl.BlockSpec(memory_space=pl.ANY),
            pl.BlockSpec(memory_space=pl.ANY),
        ],
        out_specs=pl.BlockSpec((tm, D_MODEL), lambda i: (i, 0)),
        out_shape=jax.ShapeDtypeStruct((m, D_MODEL), F32),
        scratch_shapes=[
            pltpu.VMEM((tm, D_MODEL), BF16),
            pltpu.VMEM((2, D_MODEL, tf), F32),
            pltpu.VMEM((2, D_MODEL, tf), F32),
            pltpu.VMEM((2, tf, D_MODEL), F32),
            pltpu.SemaphoreType.DMA((3, 2)),
        ],
        compiler_params=_params("arbitrary"),
        name="ffn",
    )(x2, g, wg, wu, wd)


def _fox_proj_kernel(x_ref, g_ref, wa_ref, wb_ref, wf_ref, bf_ref, qkg_ref, o_ref, lf_ref, h_ref,
                     *, q_tiles, qk_tiles, a_tiles):
    j = pl.program_id(1)

    @pl.when(j == 0)
    def _():
        _norm_to_scratch(x_ref, g_ref, h_ref)
        h = h_ref[...]
        z = lax.dot_general(h, wf_ref[...].astype(BF16), NT_DIMS,
                            preferred_element_type=F32) + bf_ref[...]
        lf_ref[...] = _log_sigmoid(z)

    @pl.when(j < qk_tiles)
    def _():
        acc = jnp.dot(h_ref[...], wa_ref[...], preferred_element_type=F32)
        mult = jnp.where(j < q_tiles, FOX_SCALE * LOG2E, 1.0).astype(F32)
        for c in range(acc.shape[1] // FOX_HEAD_DIM):
            cs = slice(c * FOX_HEAD_DIM, (c + 1) * FOX_HEAD_DIM)
            xs = acc[:, cs]
            ms = jnp.mean(xs * xs, axis=-1, keepdims=True)
            o_ref[:, cs] = (xs * (lax.rsqrt(ms + RMS_EPS) * mult) * qkg_ref[:, cs]).astype(o_ref.dtype)

    @pl.when((j >= qk_tiles) & (j < a_tiles))
    def _():
        o_ref[...] = jnp.dot(h_ref[...], wa_ref[...], preferred_element_type=F32).astype(o_ref.dtype)

    @pl.when(j >= a_tiles)
    def _():
        o_ref[...] = jnp.dot(h_ref[...], wb_ref[...], preferred_element_type=F32).astype(o_ref.dtype)


def _fox_proj(x2, g, w_qkv, w_og, wf, bf, qk_gain, tm, tn):
    m = x2.shape[0]
    q_tiles = D_MODEL // tn
    qk_tiles = 2 * q_tiles
    a_tiles = w_qkv.shape[1] // tn
    return pl.pallas_call(
        functools.partial(_fox_proj_kernel, q_tiles=q_tiles, qk_tiles=qk_tiles, a_tiles=a_tiles),
        grid=(m // tm, FOX_MAIN // tn),
        in_specs=[
            pl.BlockSpec((tm, D_MODEL), lambda i, j: (i, 0)),
            pl.BlockSpec((1, D_MODEL), lambda i, j: (0, 0)),
            pl.BlockSpec((D_MODEL, tn), lambda i, j: (0, jnp.minimum(j, a_tiles - 1))),
            pl.BlockSpec((D_MODEL, tn), lambda i, j: (0, jnp.maximum(j - a_tiles, 0))),
            pl.BlockSpec((LANES, D_MODEL), lambda i, j: (0, 0)),
            pl.BlockSpec((1, LANES), lambda i, j: (0, 0)),
            pl.BlockSpec((1, tn), lambda i, j: (0, jnp.minimum(j, qk_tiles - 1))),
        ],
        out_specs=[
            pl.BlockSpec((tm, tn), lambda i, j: (i, j)),
            pl.BlockSpec((tm, LANES), lambda i, j: (i, 0)),
        ],
        out_shape=[
            jax.ShapeDtypeStruct((m, FOX_MAIN), BF16),
            jax.ShapeDtypeStruct((m, LANES), F32),
        ],
        scratch_shapes=[pltpu.VMEM((tm, D_MODEL), BF16)],
        compiler_params=_params("parallel", "arbitrary"),
        name="fox_proj",
    )(x2, g, w_qkv, w_og, wf, bf, qk_gain)


def _cumsum_kernel(lf_ref, c_ref, *, t_len):
    row = lax.broadcasted_iota(jnp.int32, (t_len, t_len), 0)
    col = lax.broadcasted_iota(jnp.int32, (t_len, t_len), 1)
    tril = (col <= row).astype(BF16)

    def block(t, carry):
        sl = pl.ds(pl.multiple_of(t * t_len, t_len), t_len)
        c = _cumsum_rows(tril, lf_ref[0, sl, :]) + carry
        c_ref[0, sl, :] = c * LOG2E
        return c[t_len - 1:t_len, :]

    lax.fori_loop(0, lf_ref.shape[1] // t_len, block, jnp.zeros((1, LANES), F32))


def _seq_cumsum(lf, t_len):
    b_sz, s_len, _ = lf.shape
    return pl.pallas_call(
        functools.partial(_cumsum_kernel, t_len=t_len),
        grid=(b_sz,),
        in_specs=[pl.BlockSpec((1, s_len, LANES), lambda b: (b, 0, 0))],
        out_specs=pl.BlockSpec((1, s_len, LANES), lambda b: (b, 0, 0)),
        out_shape=jax.ShapeDtypeStruct(lf.shape, F32),
        compiler_params=_params("parallel"),
        name="fox_cumsum",
    )(lf)


def _fox_attn_kernel(ends_ref, firsts_ref, slack_ref, q_ref, k_ref, v_ref, og_ref, ck_ref, o_ref,
                     acc_ref, *, blk, n_sub, tk):
    i = pl.program_id(2)
    sub = blk // n_sub
    assert sub == tk
    n_kt = ck_ref.shape[2]
    bh = pl.program_id(0) * pl.num_programs(1) + pl.program_id(1)
    n_loop = jnp.maximum(i * n_sub - 1, 0)
    limit = firsts_ref[bh * pl.num_programs(2) + i] + slack_ref[0]

    def first_needed(j, lo):
        return jnp.minimum(lo, jnp.where(ends_ref[bh * n_kt + j] > limit, n_loop, j))

    j_start = lax.fori_loop(0, n_loop, first_needed, n_loop)

    c0 = ck_ref[0, 0, pl.ds(i * n_sub, 1), :][:, 0:1]
    acc_ref[...] = jnp.zeros_like(acc_ref)

    def update(r, out, carry, ks, vs, bias, mask_offset):
        m_old, l_old = carry
        rows = slice(r * sub, (r + 1) * sub)
        s = lax.dot_general(q_ref[0, rows, :], ks, NT_DIMS, preferred_element_type=F32) + bias
        yield
        if mask_offset is not None:
            row = lax.broadcasted_iota(jnp.int32, s.shape, 0)
            col = lax.broadcasted_iota(jnp.int32, s.shape, 1)
            s = jnp.where(col <= row + mask_offset, s, -jnp.inf)
        m_new = jnp.maximum(m_old, jnp.max(s, axis=-1, keepdims=True))
        p = jnp.exp2(s - m_new)
        alpha = jnp.exp2(m_old - m_new)
        l_new = alpha * l_old + jnp.sum(p, axis=-1, keepdims=True)
        acc_ref[rows, :] = alpha * acc_ref[rows, :] + jnp.dot(
            p.astype(BF16), vs, preferred_element_type=F32)
        out[r] = (m_new, l_new)

    def body(j, carry):
        sl = pl.ds(pl.multiple_of(j * tk, tk), tk)
        ks = k_ref[0, sl, :]
        vs = v_ref[0, sl, :]
        bias = c0 - ck_ref[0, 0, pl.ds(j, 1), :]
        out = [None] * n_sub
        _lockstep([update(r, out, carry[r], ks, vs, bias, None) for r in range(n_sub)])
        return tuple(out)

    init = tuple((jnp.full((sub, 1), -jnp.inf, F32), jnp.zeros((sub, 1), F32)) for _ in range(n_sub))
    carry = lax.fori_loop(j_start, n_loop, body, init)

    carry = list(carry)
    for r in range(1, n_sub):
        t_prev = i * n_sub + r - 1
        limit_r = ends_ref[bh * n_kt + t_prev] + slack_ref[0]
        for u in range(r):
            g = i * n_sub - 1 + u

            def visit(g=g, r=r):
                out = [None] * n_sub
                sl = pl.ds(pl.multiple_of(g * tk, tk), tk)
                bias = c0 - ck_ref[0, 0, pl.ds(g, 1), :]
                _lockstep([update(r, out, carry[r], k_ref[0, sl, :], v_ref[0, sl, :], bias, None)])
                return out[r]

            needed = (i > 0) & (ends_ref[bh * n_kt + jnp.maximum(g, 0)] <= limit_r)
            carry[r] = lax.cond(needed, visit, lambda r=r: carry[r])

    final = [None] * n_sub
    steps = []
    for r in range(n_sub):
        t_own = i * n_sub + r
        t_lo = jnp.maximum(t_own - 1, 0)
        win = pl.ds(pl.multiple_of(t_lo * tk, tk), 2 * tk)
        ck_win = jnp.concatenate(
            [ck_ref[0, 0, pl.ds(t_lo, 1), :], ck_ref[0, 0, pl.ds(t_lo + 1, 1), :]], axis=1)
        steps.append(update(r, final, carry[r], k_ref[0, win, :], v_ref[0, win, :],
                            c0 - ck_win, (t_own - t_lo) * tk))
    _lockstep(steps)
    for r in range(n_sub):
        rows = slice(r * sub, (r + 1) * sub)
        gate = jax.nn.sigmoid(og_ref[0, rows, :].astype(F32))
        o_ref[0, rows, :] = (acc_ref[rows, :] / final[r][1] * gate).astype(o_ref.dtype)


def _fox_attn(qkvo, ck, slack, b_sz, s_len, blk, n_sub):
    h_cols = D_MODEL // FOX_HEAD_DIM
    n_blk = s_len // blk
    n_kt, tk = ck.shape[2:]
    ends = ck[:, :, :, tk - 1].reshape(-1)
    firsts = ck.reshape(b_sz, FOX_HEADS, n_blk, blk)[:, :, :, 0].reshape(-1)
    grid_spec = pltpu.PrefetchScalarGridSpec(
        num_scalar_prefetch=3,
        grid=(b_sz, FOX_HEADS, n_blk),
        in_specs=[
            pl.BlockSpec((1, blk, FOX_HEAD_DIM), lambda b, h, i, *_: (b, i, h)),
            pl.BlockSpec((1, s_len, FOX_HEAD_DIM), lambda b, h, i, *_: (b, 0, h_cols + h)),
            pl.BlockSpec((1, s_len, FOX_HEAD_DIM), lambda b, h, i, *_: (b, 0, 2 * h_cols + h)),
            pl.BlockSpec((1, blk, FOX_HEAD_DIM), lambda b, h, i, *_: (b, i, 3 * h_cols + h)),
            pl.BlockSpec((1, 1, n_kt, tk), lambda b, h, i, *_: (b, h, 0, 0)),
        ],
        out_specs=pl.BlockSpec((1, blk, FOX_HEAD_DIM), lambda b, h, i, *_: (b, i, h)),
        scratch_shapes=[pltpu.VMEM((blk, FOX_HEAD_DIM), F32)],
    )
    return pl.pallas_call(
        functools.partial(_fox_attn_kernel, blk=blk, n_sub=n_sub, tk=tk),
        grid_spec=grid_spec,
        out_shape=jax.ShapeDtypeStruct((b_sz, s_len, D_MODEL), BF16),
        compiler_params=_params("parallel", "parallel", "arbitrary"),
        name="fox_attn",
    )(ends, firsts, slack, qkvo, qkvo, qkvo, qkvo, ck)


def _pad_rows(w, n):
    return jnp.pad(w, ((0, n - w.shape[0]), (0, 0)))


def kernel(x, norm_mix, norm_ffn, gla_w_in, gla_w_g2, gla_b_g2, gla_o_gain, gla_w_o,
           fox_w_in, fox_b_f, fox_q_gain, fox_k_gain, fox_w_o,
           ffn_w_gate, ffn_w_up, ffn_w_down):
    b_sz, s_len, d = x.shape
    assert d == D_MODEL and s_len % GLA_CHUNK == 0
    m = b_sz * s_len
    tm = min(1024, m)
    tn = 1024
    tm_res = min(512, m)
    tm_ffn = min(1024, m)
    tf = 256
    gla_t = min(512, s_len)
    attn_blk = min(1024, s_len)
    attn_tk = min(512, s_len // 2)
    attn_sub = attn_tk
    cum_t = min(256, s_len)

    x2 = x.reshape(m, d)

    def ffn_layer(xin, layer):
        return _ffn(xin, norm_ffn[layer][None, :], ffn_w_gate, ffn_w_up, ffn_w_down, layer, tm_ffn, tf)

    gla_wt = jnp.swapaxes(gla_w_in, 1, 2)
    qkvr, la = _gla_proj(
        x2, norm_mix[0][None, :],
        gla_wt,
        _pad_rows(gla_wt[0, GLA_MAIN:, :], LANES),
        _pad_rows(gla_w_g2[0], LANES),
        gla_b_g2[0][None, :], tm, tn)
    og = _gla_mix(qkvr.reshape(b_sz, s_len, GLA_MAIN), la.reshape(b_sz, s_len, GLA_KEY_DIM),
                  gla_o_gain[0][None, :], b_sz, s_len, gla_t, 4)
    x2 = _residual_matmul(x2, og.reshape(m, GLA_VAL_DIM), gla_w_o, tm_res)
    x2 = ffn_layer(x2, 0)

    qk_gain = jnp.concatenate([jnp.tile(fox_q_gain[0], FOX_HEADS), jnp.tile(fox_k_gain[0], FOX_HEADS)])
    f_lo, f_hi = 3 * D_MODEL, 3 * D_MODEL + FOX_HEADS
    fox_wt = jnp.swapaxes(fox_w_in, 1, 2)
    qkvo, lf = _fox_proj(
        x2, norm_mix[1][None, :],
        _cast_weight_t(fox_wt, 0, f_lo),
        _cast_weight_t(fox_wt[:, f_hi:, :], 0, D_MODEL),
        _pad_rows(fox_wt[0, f_lo:f_hi, :], LANES),
        jnp.pad(fox_b_f[0], (0, LANES - FOX_HEADS))[None, :],
        qk_gain[None, :], tm, tn)
    c = _seq_cumsum(lf.reshape(b_sz, s_len, LANES), cum_t)
    c_hs = c[:, :, :FOX_HEADS].transpose(0, 2, 1)
    ck = c_hs.reshape(b_sz, FOX_HEADS, s_len // attn_tk, attn_tk)
    qk_max = (FOX_HEAD_DIM * FOX_SCALE * LOG2E * BF16_SLOP
              * jnp.max(jnp.abs(fox_q_gain[0])) * jnp.max(jnp.abs(fox_k_gain[0])))
    slack = (2.0 * qk_max + F32_EXP2_FLOOR).reshape(1)
    o = _fox_attn(qkvo.reshape(b_sz, s_len, FOX_MAIN), ck, slack, b_sz, s_len, attn_blk,
                  attn_blk // attn_sub)
    x2 = _residual_matmul(x2, o.reshape(m, D_MODEL), fox_w_o, tm_res)
    x2 = ffn_layer(x2, 1)
    return x2.reshape(b_sz, s_len, d)
```

```python
import functools

import jax
import jax.numpy as jnp
from jax import lax
from jax.experimental import pallas as pl
from jax.experimental.pallas import tpu as pltpu

F32 = jnp.float32
BF16 = jnp.bfloat16

D_MODEL = 2048
RMS_EPS = 1e-6

GLA_HEADS = 4
GLA_KEY_DIM = D_MODEL // 2
GLA_VAL_DIM = D_MODEL
GLA_HEAD_K = GLA_KEY_DIM // GLA_HEADS
GLA_HEAD_V = GLA_VAL_DIM // GLA_HEADS
GLA_GATE_RANK = 16
GLA_GATE_TEMP = 16.0
GLA_CHUNK = 64
GLA_MAIN = 2 * GLA_KEY_DIM + 2 * GLA_VAL_DIM
GLA_COARSE_LEVELS = (64, 32)
GLA_FINE_LEVELS = (16, 8)
GLA_DIAG = 4
GLA_CHUNKS_PER_ITER = 2

FOX_HEAD_DIM = 128
FOX_HEADS = D_MODEL // FOX_HEAD_DIM
FOX_MAIN = 4 * D_MODEL
FOX_SCALE = FOX_HEAD_DIM ** -0.5
LOG2E = 1.4426950408889634
BF16_SLOP = 1.02
F32_EXP2_FLOOR = 160.0

D_FF = ((8 * D_MODEL + 2) // 3 + 255) // 256 * 256

LANES = 128
SUBLANES = 8
NORM_SLAB = 256
CAST_COLS = 512
VMEM_LIMIT = 58 * 1024 * 1024

_DONE = object()


def _lockstep(chains):
    pending = list(chains)
    while pending:
        pending = [g for g in pending if next(g, _DONE) is not _DONE]


NT_DIMS = (((1,), (1,)), ((), ()))
TN_DIMS = (((0,), (0,)), ((), ()))


def _params(*sem, flags=None):
    return pltpu.CompilerParams(dimension_semantics=sem, vmem_limit_bytes=VMEM_LIMIT, flags=flags)


def _rmsnorm_rows(x, g):
    ms = jnp.mean(x * x, axis=-1, keepdims=True)
    return x * lax.rsqrt(ms + RMS_EPS) * g


def _log_sigmoid(z):
    return jnp.minimum(z, 0.0) - jnp.log(1.0 + jnp.exp(-jnp.abs(z)))


def _norm_to_scratch(x_ref, g_ref, h_ref):
    rows = x_ref.shape[0]
    slab = min(NORM_SLAB, rows)
    g = g_ref[...]
    for r0 in range(0, rows, slab):
        h_ref[r0:r0 + slab, :] = _rmsnorm_rows(x_ref[r0:r0 + slab, :], g).astype(BF16)


def _split3(a):
    hi = a.astype(BF16)
    r1 = a - hi.astype(F32)
    mid = r1.astype(BF16)
    lo = (r1 - mid.astype(F32)).astype(BF16)
    return hi, mid, lo


def _cumsum_rows(tril, a):
    hi, mid, lo = _split3(a)
    out = jnp.dot(tril, hi, preferred_element_type=F32)
    out += jnp.dot(tril, mid, preferred_element_type=F32)
    out += jnp.dot(tril, lo, preferred_element_type=F32)
    return out


def _cast_t_kernel(wt_ref, o_ref):
    o_ref[...] = wt_ref[0].T.astype(BF16)


def _cast_weight_t(wt3, layer, n_cols):
    _, _, k_dim = wt3.shape
    tc = CAST_COLS
    assert n_cols % tc == 0
    return pl.pallas_call(
        _cast_t_kernel,
        grid=(n_cols // tc,),
        in_specs=[pl.BlockSpec((1, tc, k_dim), lambda i: (layer, i, 0))],
        out_specs=pl.BlockSpec((k_dim, tc), lambda i: (0, i)),
        out_shape=jax.ShapeDtypeStruct((k_dim, n_cols), BF16),
        compiler_params=_params("parallel"),
        name="cast_weight_t",
    )(wt3)


def _gla_proj_kernel(x_ref, g_ref, w_ref, wg1_ref, wg2_ref, bg_ref, o_ref, la_ref, h_ref):
    @pl.when(pl.program_id(1) == 0)
    def _():
        _norm_to_scratch(x_ref, g_ref, h_ref)
        h = h_ref[...]
        g1 = lax.dot_general(h, wg1_ref[...].astype(BF16), NT_DIMS, preferred_element_type=F32)
        z = jnp.dot(g1.astype(BF16), wg2_ref[...].astype(BF16),
                    preferred_element_type=F32) + bg_ref[...]
        la_ref[...] = _log_sigmoid(z) * (LOG2E / GLA_GATE_TEMP)

    o_ref[...] = lax.dot_general(h_ref[...], w_ref[0].astype(BF16), NT_DIMS,
                                 preferred_element_type=F32).astype(o_ref.dtype)


def _gla_proj(x2, g, wt3, wg1, wg2, bg, tm, tn):
    m = x2.shape[0]
    return pl.pallas_call(
        _gla_proj_kernel,
        grid=(m // tm, GLA_MAIN // tn),
        in_specs=[
            pl.BlockSpec((tm, D_MODEL), lambda i, j: (i, 0)),
            pl.BlockSpec((1, D_MODEL), lambda i, j: (0, 0)),
            pl.BlockSpec((1, tn, D_MODEL), lambda i, j: (0, j, 0)),
            pl.BlockSpec((LANES, D_MODEL), lambda i, j: (0, 0)),
            pl.BlockSpec((LANES, GLA_KEY_DIM), lambda i, j: (0, 0)),
            pl.BlockSpec((1, GLA_KEY_DIM), lambda i, j: (0, 0)),
        ],
        out_specs=[
            pl.BlockSpec((tm, tn), lambda i, j: (i, j)),
            pl.BlockSpec((tm, GLA_KEY_DIM), lambda i, j: (i, 0)),
        ],
        out_shape=[
            jax.ShapeDtypeStruct((m, GLA_MAIN), BF16),
            jax.ShapeDtypeStruct((m, GLA_KEY_DIM), F32),
        ],
        scratch_shapes=[pltpu.VMEM((tm, D_MODEL), BF16)],
        compiler_params=_params("parallel", "arbitrary"),
        name="gla_proj",
    )(x2, g, wt3, wg1, wg2, bg)


def _gla_kernel(q_ref, k_ref, v_ref, r_ref, la_ref, gain_ref, o_ref, st_ref, *, n_chunks, heads):
    c_len = GLA_CHUNK

    @pl.when(pl.program_id(2) == 0)
    def _():
        st_ref[...] = jnp.zeros_like(st_ref)

    row = lax.broadcasted_iota(jnp.int32, (c_len, c_len), 0)
    col = lax.broadcasted_iota(jnp.int32, (c_len, c_len), 1)
    tril = (col <= row).astype(BF16)
    delta = row - col
    band = jnp.where((delta >= 0) & (delta <= row % GLA_DIAG), delta, -1)
    fine_masks = [
        (blk, (row // blk == col // blk) & (row % blk >= blk // 2) & (col % blk < blk // 2))
        for blk in GLA_FINE_LEVELS]
    half_rows = c_len // 2
    hrow = lax.broadcasted_iota(jnp.int32, (half_rows, half_rows), 0)
    hcol = lax.broadcasted_iota(jnp.int32, (half_rows, half_rows), 1)
    gain = gain_ref[...]
    scale = GLA_HEAD_K ** -0.5

    def gather_rows(a, blk, second_half):
        half = blk // 2
        off = half if second_half else 0
        return jnp.concatenate([a[s + off:s + off + half] for s in range(0, c_len, blk)], axis=0)

    def head_chunk(hd, sl):
        ksl = slice(hd * GLA_HEAD_K, (hd + 1) * GLA_HEAD_K)
        vsl = slice(hd * GLA_HEAD_V, (hd + 1) * GLA_HEAD_V)
        la = la_ref[0, sl, ksl]
        b = _cumsum_rows(tril, la)
        yield
        b_last = b[c_len - 1:c_len, :]
        qf = q_ref[0, sl, ksl].astype(F32)
        kf = k_ref[0, sl, ksl].astype(F32)
        v = v_ref[0, sl, vsl]

        st = st_ref[hd]
        qi = (qf * jnp.exp2(b)).astype(BF16)
        o = lax.dot_general(qi, st.astype(BF16), NT_DIMS, preferred_element_type=F32)
        kd = (kf * jnp.exp2(b_last - b)).astype(BF16)
        st_ref[hd] = st * jnp.exp2(b_last) + lax.dot_general(
            v, kd, TN_DIMS, preferred_element_type=F32)
        yield

        parts = []
        for blk in GLA_COARSE_LEVELS:
            half = blk // 2
            ref = jnp.concatenate(
                [jnp.broadcast_to(b[s + half - 1:s + half, :], (half, GLA_HEAD_K))
                 for s in range(0, c_len, blk)], axis=0)
            ql = (gather_rows(qf, blk, True) * jnp.exp2(gather_rows(b, blk, True) - ref)).astype(BF16)
            kl = (gather_rows(kf, blk, False) * jnp.exp2(ref - gather_rows(b, blk, False))).astype(BF16)
            a_l = lax.dot_general(ql, kl, NT_DIMS, preferred_element_type=F32)
            yield
            if blk < c_len:
                a_l = jnp.where(hrow // half == hcol // half, a_l, 0.0)
            parts.append((blk, jnp.dot(a_l.astype(BF16), gather_rows(v, blk, False),
                                       preferred_element_type=F32)))
            yield

        attn = jnp.zeros((c_len, c_len), F32)
        for blk, mk in fine_masks:
            half = blk // 2
            ref = jnp.concatenate(
                [jnp.broadcast_to(b[s + half - 1:s + half, :], (blk, GLA_HEAD_K))
                 for s in range(0, c_len, blk)], axis=0)
            ql = (qf * jnp.exp2(b - ref)).astype(BF16)
            kl = (kf * jnp.exp2(ref - b)).astype(BF16)
            a_l = lax.dot_general(ql, kl, NT_DIMS, preferred_element_type=F32)
            yield
            attn = jnp.where(mk, a_l, attn)
        for d in range(GLA_DIAG):
            if d:
                prod = qf * pltpu.roll(kf, d, 0) * jnp.exp2(b - pltpu.roll(b, d, 0))
            else:
                prod = qf * kf
            attn = jnp.where(band == d, jnp.sum(prod, axis=-1, keepdims=True), attn)
        o += jnp.dot(attn.astype(BF16), v, preferred_element_type=F32)
        yield

        groups = [o[g:g + SUBLANES] for g in range(0, c_len, SUBLANES)]
        for blk, part in parts:
            half = blk // 2
            src = 0
            for s in range(0, c_len, blk):
                for g in range((s + half) // SUBLANES, (s + blk) // SUBLANES):
                    groups[g] = groups[g] + part[src:src + SUBLANES]
                    src += SUBLANES
        o = jnp.concatenate(groups, axis=0)

        ms = jnp.mean(o * o, axis=-1, keepdims=True)
        factor = scale * lax.rsqrt(scale * scale * ms + RMS_EPS)
        r = r_ref[0, sl, vsl].astype(F32)
        o_ref[0, sl, vsl] = (o * factor * gain * (r * jax.nn.sigmoid(r))).astype(o_ref.dtype)

    def chunk_group(c, carry):
        chains = []
        for u in range(GLA_CHUNKS_PER_ITER):
            sl = pl.ds(pl.multiple_of((c * GLA_CHUNKS_PER_ITER + u) * c_len, c_len), c_len)
            chains += [head_chunk(hd, sl) for hd in range(heads)]
        _lockstep(chains)
        return carry

    assert n_chunks % GLA_CHUNKS_PER_ITER == 0
    lax.fori_loop(0, n_chunks // GLA_CHUNKS_PER_ITER, chunk_group, 0)


def _gla_mix(qkvr, la, gain, b_sz, s_len, t_len, heads):
    n_chunks = t_len // GLA_CHUNK
    wk = heads * GLA_HEAD_K
    wv = heads * GLA_HEAD_V
    kq = GLA_KEY_DIM // wk
    kv = 2 * GLA_KEY_DIM // wv
    kr = kv + GLA_VAL_DIM // wv
    return pl.pallas_call(
        functools.partial(_gla_kernel, n_chunks=n_chunks, heads=heads),
        grid=(b_sz, GLA_HEADS // heads, s_len // t_len),
        in_specs=[
            pl.BlockSpec((1, t_len, wk), lambda b, h, t: (b, t, h)),
            pl.BlockSpec((1, t_len, wk), lambda b, h, t: (b, t, kq + h)),
            pl.BlockSpec((1, t_len, wv), lambda b, h, t: (b, t, kv + h)),
            pl.BlockSpec((1, t_len, wv), lambda b, h, t: (b, t, kr + h)),
            pl.BlockSpec((1, t_len, wk), lambda b, h, t: (b, t, h)),
            pl.BlockSpec((1, GLA_HEAD_V), lambda b, h, t: (0, 0)),
        ],
        out_specs=pl.BlockSpec((1, t_len, wv), lambda b, h, t: (b, t, h)),
        out_shape=jax.ShapeDtypeStruct((b_sz, s_len, GLA_VAL_DIM), BF16),
        scratch_shapes=[pltpu.VMEM((heads, GLA_HEAD_V, GLA_HEAD_K), F32)],
        compiler_params=_params("parallel", "parallel", "arbitrary"),
        name="gla_mix",
    )(qkvr, qkvr, qkvr, qkvr, la, gain)


def _residual_matmul_kernel(x_ref, a_ref, w_ref, o_ref, wb_ref):
    @pl.when(pl.program_id(0) == 0)
    def _():
        for r0 in range(0, w_ref.shape[1], NORM_SLAB):
            wb_ref[r0:r0 + NORM_SLAB, :] = w_ref[0, r0:r0 + NORM_SLAB, :].astype(BF16)

    o_ref[...] = x_ref[...] + jnp.dot(a_ref[...], wb_ref[...], preferred_element_type=F32)


def _residual_matmul(x2, a, w3, tm):
    m, k_dim = a.shape
    n = w3.shape[2]
    return pl.pallas_call(
        _residual_matmul_kernel,
        grid=(m // tm,),
        in_specs=[
            pl.BlockSpec((tm, n), lambda i: (i, 0)),
            pl.BlockSpec((tm, k_dim), lambda i: (i, 0)),
            pl.BlockSpec((1, k_dim, n), lambda i: (0, 0, 0), pipeline_mode=pl.Buffered(1)),
        ],
        out_specs=pl.BlockSpec((tm, n), lambda i: (i, 0)),
        out_shape=jax.ShapeDtypeStruct((m, n), F32),
        scratch_shapes=[pltpu.VMEM((k_dim, n), BF16)],
        compiler_params=_params("arbitrary"),
        name="residual_matmul",
    )(x2, a, w3)


def _ffn_kernel(x_ref, g_ref, wg_hbm, wu_hbm, wd_hbm, o_ref, h_ref, wg_buf, wu_buf, wd_buf, sem,
                *, layer, tf):
    i = pl.program_id(0)
    n_f = D_FF // tf
    assert n_f % 2 == 0

    def tile_copies(f, slot):
        cols = pl.ds(pl.multiple_of(f * tf, tf), tf)
        return (
            pltpu.make_async_copy(wg_hbm.at[layer, :, cols], wg_buf.at[slot], sem.at[0, slot]),
            pltpu.make_async_copy(wu_hbm.at[layer, :, cols], wu_buf.at[slot], sem.at[1, slot]),
            pltpu.make_async_copy(wd_hbm.at[layer, cols, :], wd_buf.at[slot], sem.at[2, slot]),
        )

    @pl.when(i == 0)
    def _():
        for cp in tile_copies(0, 0):
            cp.start()

    _norm_to_scratch(x_ref, g_ref, h_ref)

    def use_tile(f, slot, acc_ref):
        for cp in tile_copies(f, slot):
            cp.wait()
        nxt = jnp.where(f + 1 == n_f, 0, f + 1)
        for cp in tile_copies(nxt, 1 - slot):
            cp.start()
        h = h_ref[...]
        gate = jnp.dot(h, wg_buf[slot].astype(BF16), preferred_element_type=F32)
        up = jnp.dot(h, wu_buf[slot].astype(BF16), preferred_element_type=F32)
        act = (gate * jax.nn.sigmoid(gate) * up).astype(BF16)
        o_ref[...] = acc_ref[...] + jnp.dot(act, wd_buf[slot].astype(BF16),
                                            preferred_element_type=F32)

    def pair(p, carry):
        use_tile(2 * p, 0, o_ref)
        use_tile(2 * p + 1, 1, o_ref)
        return carry

    use_tile(0, 0, x_ref)
    use_tile(1, 1, o_ref)
    lax.fori_loop(1, n_f // 2, pair, 0)

    @pl.when(i == pl.num_programs(0) - 1)
    def _():
        for cp in tile_copies(0, 0):
            cp.wait()


def _ffn(x2, g, wg, wu, wd, layer, tm, tf):
    m = x2.shape[0]
    return pl.pallas_call(
        functools.partial(_ffn_kernel, layer=layer, tf=tf),
        grid=(m // tm,),
        in_specs=[
            pl.BlockSpec((tm, D_MODEL), lambda i: (i, 0)),
            pl.BlockSpec((1, D_MODEL), lambda i: (0, 0)),
            pl.BlockSpec(memory_space=pl.ANY),
            pl.BlockSpec(memory_space=pl.ANY),
            pl.BlockSpec(memory_space=pl.ANY),
        ],
        out_specs=pl.BlockSpec((tm, D_MODEL), lambda i: (i, 0)),
        out_shape=jax.ShapeDtypeStruct((m, D_MODEL), F32),
        scratch_shapes=[
            pltpu.VMEM((tm, D_MODEL), BF16),
            pltpu.VMEM((2, D_MODEL, tf), F32),
            pltpu.VMEM((2, D_MODEL, tf), F32),
            pltpu.VMEM((2, tf, D_MODEL), F32),
            pltpu.SemaphoreType.DMA((3, 2)),
        ],
        compiler_params=_params("arbitrary"),
        name="ffn",
    )(x2, g, wg, wu, wd)


def _fox_proj_kernel(x_ref, g_ref, wa_ref, wb_ref, wf_ref, bf_ref, qkg_ref, o_ref, lf_ref, h_ref,
                     *, q_tiles, qk_tiles, a_tiles):
    j = pl.program_id(1)

    @pl.when(j == 0)
    def _():
        _norm_to_scratch(x_ref, g_ref, h_ref)
        h = h_ref[...]
        z = lax.dot_general(h, wf_ref[...].astype(BF16), NT_DIMS,
                            preferred_element_type=F32) + bf_ref[...]
        lf_ref[...] = _log_sigmoid(z)

    @pl.when(j < qk_tiles)
    def _():
        acc = jnp.dot(h_ref[...], wa_ref[...], preferred_element_type=F32)
        mult = jnp.where(j < q_tiles, FOX_SCALE * LOG2E, 1.0).astype(F32)
        for c in range(acc.shape[1] // FOX_HEAD_DIM):
            cs = slice(c * FOX_HEAD_DIM, (c + 1) * FOX_HEAD_DIM)
            xs = acc[:, cs]
            ms = jnp.mean(xs * xs, axis=-1, keepdims=True)
            o_ref[:, cs] = (xs * (lax.rsqrt(ms + RMS_EPS) * mult) * qkg_ref[:, cs]).astype(o_ref.dtype)

    @pl.when((j >= qk_tiles) & (j < a_tiles))
    def _():
        o_ref[...] = jnp.dot(h_ref[...], wa_ref[...], preferred_element_type=F32).astype(o_ref.dtype)

    @pl.when(j >= a_tiles)
    def _():
        o_ref[...] = jnp.dot(h_ref[...], wb_ref[...], preferred_element_type=F32).astype(o_ref.dtype)


def _fox_proj(x2, g, w_qkv, w_og, wf, bf, qk_gain, tm, tn):
    m = x2.shape[0]
    q_tiles = D_MODEL // tn
    qk_tiles = 2 * q_tiles
    a_tiles = w_qkv.shape[1] // tn
    return pl.pallas_call(
        functools.partial(_fox_proj_kernel, q_tiles=q_tiles, qk_tiles=qk_tiles, a_tiles=a_tiles),
        grid=(m // tm, FOX_MAIN // tn),
        in_specs=[
            pl.BlockSpec((tm, D_MODEL), lambda i, j: (i, 0)),
            pl.BlockSpec((1, D_MODEL), lambda i, j: (0, 0)),
            pl.BlockSpec((D_MODEL, tn), lambda i, j: (0, jnp.minimum(j, a_tiles - 1))),
            pl.BlockSpec((D_MODEL, tn), lambda i, j: (0, jnp.maximum(j - a_tiles, 0))),
            pl.BlockSpec((LANES, D_MODEL), lambda i, j: (0, 0)),
            pl.BlockSpec((1, LANES), lambda i, j: (0, 0)),
            pl.BlockSpec((1, tn), lambda i, j: (0, jnp.minimum(j, qk_tiles - 1))),
        ],
        out_specs=[
            pl.BlockSpec((tm, tn), lambda i, j: (i, j)),
            pl.BlockSpec((tm, LANES), lambda i, j: (i, 0)),
        ],
        out_shape=[
            jax.ShapeDtypeStruct((m, FOX_MAIN), BF16),
            jax.ShapeDtypeStruct((m, LANES), F32),
        ],
        scratch_shapes=[pltpu.VMEM((tm, D_MODEL), BF16)],
        compiler_params=_params("parallel", "arbitrary"),
        name="fox_proj",
    )(x2, g, w_qkv, w_og, wf, bf, qk_gain)


def _cumsum_kernel(lf_ref, c_ref, *, t_len):
    row = lax.broadcasted_iota(jnp.int32, (t_len, t_len), 0)
    col = lax.broadcasted_iota(jnp.int32, (t_len, t_len), 1)
    tril = (col <= row).astype(BF16)

    def block(t, carry):
        sl = pl.ds(pl.multiple_of(t * t_len, t_len), t_len)
        c = _cumsum_rows(tril, lf_ref[0, sl, :]) + carry
        c_ref[0, sl, :] = c * LOG2E
        return c[t_len - 1:t_len, :]

    lax.fori_loop(0, lf_ref.shape[1] // t_len, block, jnp.zeros((1, LANES), F32))


def _seq_cumsum(lf, t_len):
    b_sz, s_len, _ = lf.shape
    return pl.pallas_call(
        functools.partial(_cumsum_kernel, t_len=t_len),
        grid=(b_sz,),
        in_specs=[pl.BlockSpec((1, s_len, LANES), lambda b: (b, 0, 0))],
        out_specs=pl.BlockSpec((1, s_len, LANES), lambda b: (b, 0, 0)),
        out_shape=jax.ShapeDtypeStruct(lf.shape, F32),
        compiler_params=_params("parallel"),
        name="fox_cumsum",
    )(lf)


def _fox_attn_kernel(ends_ref, firsts_ref, slack_ref, q_ref, k_ref, v_ref, og_ref, ck_ref, o_ref,
                     acc_ref, va_ref, *, blk, n_sub, tk):
    i = pl.program_id(2)
    sub = blk // n_sub
    assert sub == tk
    n_kt = ck_ref.shape[2]
    bh = pl.program_id(0) * pl.num_programs(1) + pl.program_id(1)
    n_loop = jnp.maximum(i * n_sub - 1, 0)
    limit = firsts_ref[bh * pl.num_programs(2) + i] + slack_ref[0]

    def first_needed(j, lo):
        return jnp.minimum(lo, jnp.where(ends_ref[bh * n_kt + j] > limit, n_loop, j))

    j_start = lax.fori_loop(0, n_loop, first_needed, n_loop)

    c0 = ck_ref[0, 0, pl.ds(i * n_sub, 1), :][:, 0:1]
    acc_ref[...] = jnp.zeros_like(acc_ref)

    @pl.when(i == 0)
    def _():
        for r0 in range(0, va_ref.shape[0], blk):
            va_ref[r0:r0 + blk, :FOX_HEAD_DIM] = v_ref[0, r0:r0 + blk, :]
            va_ref[r0:r0 + blk, FOX_HEAD_DIM:] = jnp.ones((blk, FOX_HEAD_DIM), BF16)

    def update(r, out, m_old, ks, va, bias, mask_offset):
        rows = slice(r * sub, (r + 1) * sub)
        s = lax.dot_general(q_ref[0, rows, :], ks, NT_DIMS, preferred_element_type=F32) + bias
        yield
        if mask_offset is not None:
            row = lax.broadcasted_iota(jnp.int32, s.shape, 0)
            col = lax.broadcasted_iota(jnp.int32, s.shape, 1)
            s = jnp.where(col <= row + mask_offset, s, -jnp.inf)
        m_new = jnp.maximum(m_old, jnp.max(s, axis=-1, keepdims=True))
        p = jnp.exp2(s - m_new).astype(BF16)
        alpha = jnp.exp2(m_old - m_new)
        acc_ref[rows, :] = alpha * acc_ref[rows, :] + jnp.dot(p, va, preferred_element_type=F32)
        out[r] = m_new

    def body(j, carry):
        sl = pl.ds(pl.multiple_of(j * tk, tk), tk)
        ks = k_ref[0, sl, :]
        va = va_ref[sl, :]
        bias = c0 - ck_ref[0, 0, pl.ds(j, 1), :]
        out = [None] * n_sub
        _lockstep([update(r, out, carry[r], ks, va, bias, None) for r in range(n_sub)])
        return tuple(out)

    init = tuple(jnp.full((sub, 1), -jnp.inf, F32) for _ in range(n_sub))
    carry = lax.fori_loop(j_start, n_loop, body, init)

    carry = list(carry)
    for r in range(1, n_sub):
        t_prev = i * n_sub + r - 1
        limit_r = ends_ref[bh * n_kt + t_prev] + slack_ref[0]
        for u in range(r):
            g = i * n_sub - 1 + u

            def visit(g=g, r=r):
                out = [None] * n_sub
                sl = pl.ds(pl.multiple_of(g * tk, tk), tk)
                bias = c0 - ck_ref[0, 0, pl.ds(g, 1), :]
                _lockstep([update(r, out, carry[r], k_ref[0, sl, :], va_ref[sl, :], bias, None)])
                return out[r]

            needed = (i > 0) & (ends_ref[bh * n_kt + jnp.maximum(g, 0)] <= limit_r)
            carry[r] = lax.cond(needed, visit, lambda r=r: carry[r])

    final = [None] * n_sub
    steps = []
    for r in range(n_sub):
        t_own = i * n_sub + r
        t_lo = jnp.maximum(t_own - 1, 0)
        win = pl.ds(pl.multiple_of(t_lo * tk, tk), 2 * tk)
        ck_win = jnp.concatenate(
            [ck_ref[0, 0, pl.ds(t_lo, 1), :], ck_ref[0, 0, pl.ds(t_lo + 1, 1), :]], axis=1)
        steps.append(update(r, final, carry[r], k_ref[0, win, :], va_ref[win, :],
                            c0 - ck_win, (t_own - t_lo) * tk))
    _lockstep(steps)
    for r in range(n_sub):
        rows = slice(r * sub, (r + 1) * sub)
        acc = acc_ref[rows, :]
        gate = jax.nn.sigmoid(og_ref[0, rows, :].astype(F32))
        o_ref[0, rows, :] = (acc[:, :FOX_HEAD_DIM] / acc[:, FOX_HEAD_DIM:FOX_HEAD_DIM + 1]
                             * gate).astype(o_ref.dtype)


def _fox_attn(qkvo, ck, slack, b_sz, s_len, blk, n_sub):
    h_cols = D_MODEL // FOX_HEAD_DIM
    n_blk = s_len // blk
    n_kt, tk = ck.shape[2:]
    ends = ck[:, :, :, tk - 1].reshape(-1)
    firsts = ck.reshape(b_sz, FOX_HEADS, n_blk, blk)[:, :, :, 0].reshape(-1)
    grid_spec = pltpu.PrefetchScalarGridSpec(
        num_scalar_prefetch=3,
        grid=(b_sz, FOX_HEADS, n_blk),
        in_specs=[
            pl.BlockSpec((1, blk, FOX_HEAD_DIM), lambda b, h, i, *_: (b, i, h)),
            pl.BlockSpec((1, s_len, FOX_HEAD_DIM), lambda b, h, i, *_: (b, 0, h_cols + h)),
            pl.BlockSpec((1, s_len, FOX_HEAD_DIM), lambda b, h, i, *_: (b, 0, 2 * h_cols + h)),
            pl.BlockSpec((1, blk, FOX_HEAD_DIM), lambda b, h, i, *_: (b, i, 3 * h_cols + h)),
            pl.BlockSpec((1, 1, n_kt, tk), lambda b, h, i, *_: (b, h, 0, 0)),
        ],
        out_specs=pl.BlockSpec((1, blk, FOX_HEAD_DIM), lambda b, h, i, *_: (b, i, h)),
        scratch_shapes=[
            pltpu.VMEM((blk, 2 * FOX_HEAD_DIM), F32),
            pltpu.VMEM((s_len, 2 * FOX_HEAD_DIM), BF16),
        ],
    )
    return pl.pallas_call(
        functools.partial(_fox_attn_kernel, blk=blk, n_sub=n_sub, tk=tk),
        grid_spec=grid_spec,
        out_shape=jax.ShapeDtypeStruct((b_sz, s_len, D_MODEL), BF16),
        compiler_params=_params("parallel", "parallel", "arbitrary"),
        name="fox_attn",
    )(ends, firsts, slack, qkvo, qkvo, qkvo, qkvo, ck)


def _pad_rows(w, n):
    return jnp.pad(w, ((0, n - w.shape[0]), (0, 0)))


def kernel(x, norm_mix, norm_ffn, gla_w_in, gla_w_g2, gla_b_g2, gla_o_gain, gla_w_o,
           fox_w_in, fox_b_f, fox_q_gain, fox_k_gain, fox_w_o,
           ffn_w_gate, ffn_w_up, ffn_w_down):
    b_sz, s_len, d = x.shape
    assert d == D_MODEL and s_len % GLA_CHUNK == 0
    m = b_sz * s_len
    tm = min(1024, m)
    tn = 1024
    tm_res = min(512, m)
    tm_ffn = min(1024, m)
    tf = 256
    gla_t = min(512, s_len)
    attn_blk = min(1024, s_len)
    attn_tk = min(512, s_len // 2)
    attn_sub = attn_tk
    cum_t = min(256, s_len)

    x2 = x.reshape(m, d)

    def ffn_layer(xin, layer):
        return _ffn(xin, norm_ffn[layer][None, :], ffn_w_gate, ffn_w_up, ffn_w_down, layer, tm_ffn, tf)

    gla_wt = jnp.swapaxes(gla_w_in, 1, 2)
    qkvr, la = _gla_proj(
        x2, norm_mix[0][None, :],
        gla_wt,
        _pad_rows(gla_wt[0, GLA_MAIN:, :], LANES),
        _pad_rows(gla_w_g2[0], LANES),
        gla_b_g2[0][None, :], tm, tn)
    og = _gla_mix(qkvr.reshape(b_sz, s_len, GLA_MAIN), la.reshape(b_sz, s_len, GLA_KEY_DIM),
                  gla_o_gain[0][None, :], b_sz, s_len, gla_t, 4)
    x2 = _residual_matmul(x2, og.reshape(m, GLA_VAL_DIM), gla_w_o, tm_res)
    x2 = ffn_layer(x2, 0)

    qk_gain = jnp.concatenate([jnp.tile(fox_q_gain[0], FOX_HEADS), jnp.tile(fox_k_gain[0], FOX_HEADS)])
    f_lo, f_hi = 3 * D_MODEL, 3 * D_MODEL + FOX_HEADS
    fox_wt = jnp.swapaxes(fox_w_in, 1, 2)
    qkvo, lf = _fox_proj(
        x2, norm_mix[1][None, :],
        _cast_weight_t(fox_wt, 0, f_lo),
        _cast_weight_t(fox_wt[:, f_hi:, :], 0, D_MODEL),
        _pad_rows(fox_wt[0, f_lo:f_hi, :], LANES),
        jnp.pad(fox_b_f[0], (0, LANES - FOX_HEADS))[None, :],
        qk_gain[None, :], tm, tn)
    c = _seq_cumsum(lf.reshape(b_sz, s_len, LANES), cum_t)
    c_hs = c[:, :, :FOX_HEADS].transpose(0, 2, 1)
    ck = c_hs.reshape(b_sz, FOX_HEADS, s_len // attn_tk, attn_tk)
    qk_max = (FOX_HEAD_DIM * FOX_SCALE * LOG2E * BF16_SLOP
              * jnp.max(jnp.abs(fox_q_gain[0])) * jnp.max(jnp.abs(fox_k_gain[0])))
    slack = (2.0 * qk_max + F32_EXP2_FLOOR).reshape(1)
    o = _fox_attn(qkvo.reshape(b_sz, s_len, FOX_MAIN), ck, slack, b_sz, s_len, attn_blk,
                  attn_blk // attn_sub)
    x2 = _residual_matmul(x2, o.reshape(m, D_MODEL), fox_w_o, tm_res)
    x2 = ffn_layer(x2, 1)
    return x2.reshape(b_sz, s_len, d)
```

```python
import functools

import jax
import jax.numpy as jnp
from jax import lax
from jax.experimental import pallas as pl
from jax.experimental.pallas import tpu as pltpu

F32 = jnp.float32
BF16 = jnp.bfloat16

D_MODEL = 2048
RMS_EPS = 1e-6

GLA_HEADS = 4
GLA_KEY_DIM = D_MODEL // 2
GLA_VAL_DIM = D_MODEL
GLA_HEAD_K = GLA_KEY_DIM // GLA_HEADS
GLA_HEAD_V = GLA_VAL_DIM // GLA_HEADS
GLA_GATE_RANK = 16
GLA_GATE_TEMP = 16.0
GLA_CHUNK = 64
GLA_MAIN = 2 * GLA_KEY_DIM + 2 * GLA_VAL_DIM
GLA_COARSE_LEVELS = (64, 32)
GLA_FINE_LEVELS = (16, 8)
GLA_DIAG = 4
GLA_CHUNKS_PER_ITER = 2

FOX_HEAD_DIM = 128
FOX_HEADS = D_MODEL // FOX_HEAD_DIM
FOX_MAIN = 4 * D_MODEL
FOX_SCALE = FOX_HEAD_DIM ** -0.5
LOG2E = 1.4426950408889634
BF16_SLOP = 1.02
F32_EXP2_FLOOR = 160.0

D_FF = ((8 * D_MODEL + 2) // 3 + 255) // 256 * 256

LANES = 128
SUBLANES = 8
NORM_SLAB = 256
CAST_COLS = 512
VMEM_LIMIT = 58 * 1024 * 1024

_DONE = object()


def _lockstep(chains):
    pending = list(chains)
    while pending:
        pending = [g for g in pending if next(g, _DONE) is not _DONE]


NT_DIMS = (((1,), (1,)), ((), ()))
TN_DIMS = (((0,), (0,)), ((), ()))


def _params(*sem, flags=None):
    return pltpu.CompilerParams(dimension_semantics=sem, vmem_limit_bytes=VMEM_LIMIT, flags=flags)


def _rmsnorm_rows(x, g):
    ms = jnp.mean(x * x, axis=-1, keepdims=True)
    return x * lax.rsqrt(ms + RMS_EPS) * g


def _log_sigmoid(z):
    return jnp.minimum(z, 0.0) - jnp.log(1.0 + jnp.exp(-jnp.abs(z)))


def _norm_to_scratch(x_ref, g_ref, h_ref):
    rows = x_ref.shape[0]
    slab = min(NORM_SLAB, rows)
    g = g_ref[...]
    for r0 in range(0, rows, slab):
        h_ref[r0:r0 + slab, :] = _rmsnorm_rows(x_ref[r0:r0 + slab, :], g).astype(BF16)


def _split3(a):
    hi = a.astype(BF16)
    r1 = a - hi.astype(F32)
    mid = r1.astype(BF16)
    lo = (r1 - mid.astype(F32)).astype(BF16)
    return hi, mid, lo


def _cumsum_rows(tril, a, terms=3):
    parts = _split3(a)[:terms]
    out = jnp.dot(tril, parts[0], preferred_element_type=F32)
    for part in parts[1:]:
        out += jnp.dot(tril, part, preferred_element_type=F32)
    return out


def _cast_t_kernel(wt_ref, o_ref):
    o_ref[...] = wt_ref[0].T.astype(BF16)


def _cast_weight_t(wt3, layer, n_cols, gap_at, gap):
    _, _, k_dim = wt3.shape
    tc = CAST_COLS
    assert n_cols % tc == 0 and gap_at % tc == 0 and gap % SUBLANES == 0
    first_after = gap_at // tc
    return pl.pallas_call(
        _cast_t_kernel,
        grid=(n_cols // tc,),
        in_specs=[pl.BlockSpec(
            (pl.Element(1), pl.Element(tc), pl.Element(k_dim)),
            lambda i: (layer, pl.multiple_of(i * tc + jnp.where(i >= first_after, gap, 0),
                                             SUBLANES), 0))],
        out_specs=pl.BlockSpec((k_dim, tc), lambda i: (0, i)),
        out_shape=jax.ShapeDtypeStruct((k_dim, n_cols), BF16),
        compiler_params=_params("parallel"),
        name="cast_weight_t",
    )(wt3)


def _gla_proj_kernel(x_ref, g_ref, w_ref, wg1_ref, wg2_ref, bg_ref, o_ref, la_ref, h_ref):
    @pl.when(pl.program_id(1) == 0)
    def _():
        _norm_to_scratch(x_ref, g_ref, h_ref)
        h = h_ref[...]
        g1 = lax.dot_general(h, wg1_ref[...].astype(BF16), NT_DIMS, preferred_element_type=F32)
        z = jnp.dot(g1.astype(BF16), wg2_ref[...].astype(BF16),
                    preferred_element_type=F32) + bg_ref[...]
        la_ref[...] = _log_sigmoid(z) * (LOG2E / GLA_GATE_TEMP)

    o_ref[...] = lax.dot_general(h_ref[...], w_ref[0].astype(BF16), NT_DIMS,
                                 preferred_element_type=F32).astype(o_ref.dtype)


def _gla_proj(x2, g, wt3, wg1, wg2, bg, tm, tn):
    m = x2.shape[0]
    return pl.pallas_call(
        _gla_proj_kernel,
        grid=(m // tm, GLA_MAIN // tn),
        in_specs=[
            pl.BlockSpec((tm, D_MODEL), lambda i, j: (i, 0)),
            pl.BlockSpec((1, D_MODEL), lambda i, j: (0, 0)),
            pl.BlockSpec((1, tn, D_MODEL), lambda i, j: (0, j, 0)),
            pl.BlockSpec((LANES, D_MODEL), lambda i, j: (0, 0)),
            pl.BlockSpec((LANES, GLA_KEY_DIM), lambda i, j: (0, 0)),
            pl.BlockSpec((1, GLA_KEY_DIM), lambda i, j: (0, 0)),
        ],
        out_specs=[
            pl.BlockSpec((tm, tn), lambda i, j: (i, j)),
            pl.BlockSpec((tm, GLA_KEY_DIM), lambda i, j: (i, 0)),
        ],
        out_shape=[
            jax.ShapeDtypeStruct((m, GLA_MAIN), BF16),
            jax.ShapeDtypeStruct((m, GLA_KEY_DIM), F32),
        ],
        scratch_shapes=[pltpu.VMEM((tm, D_MODEL), BF16)],
        compiler_params=_params("parallel", "arbitrary"),
        name="gla_proj",
    )(x2, g, wt3, wg1, wg2, bg)


def _gla_kernel(q_ref, k_ref, v_ref, r_ref, la_ref, gain_ref, o_ref, st_ref, *, n_chunks, heads):
    c_len = GLA_CHUNK

    @pl.when(pl.program_id(2) == 0)
    def _():
        st_ref[...] = jnp.zeros_like(st_ref)

    row = lax.broadcasted_iota(jnp.int32, (c_len, c_len), 0)
    col = lax.broadcasted_iota(jnp.int32, (c_len, c_len), 1)
    tril = (col <= row).astype(BF16)
    delta = row - col
    band = jnp.where((delta >= 0) & (delta <= row % GLA_DIAG), delta, -1)
    fine_masks = [
        (blk, (row // blk == col // blk) & (row % blk >= blk // 2) & (col % blk < blk // 2))
        for blk in GLA_FINE_LEVELS]
    half_rows = c_len // 2
    hrow = lax.broadcasted_iota(jnp.int32, (half_rows, half_rows), 0)
    hcol = lax.broadcasted_iota(jnp.int32, (half_rows, half_rows), 1)
    gain = gain_ref[...]
    scale = GLA_HEAD_K ** -0.5

    def gather_rows(a, blk, second_half):
        half = blk // 2
        off = half if second_half else 0
        return jnp.concatenate([a[s + off:s + off + half] for s in range(0, c_len, blk)], axis=0)

    def head_chunk(hd, sl):
        ksl = slice(hd * GLA_HEAD_K, (hd + 1) * GLA_HEAD_K)
        vsl = slice(hd * GLA_HEAD_V, (hd + 1) * GLA_HEAD_V)
        la = la_ref[0, sl, ksl]
        b = _cumsum_rows(tril, la, terms=2)
        yield
        b_last = b[c_len - 1:c_len, :]
        qf = q_ref[0, sl, ksl].astype(F32)
        kf = k_ref[0, sl, ksl].astype(F32)
        v = v_ref[0, sl, vsl]

        st = st_ref[hd]
        qi = (qf * jnp.exp2(b)).astype(BF16)
        o = lax.dot_general(qi, st.astype(BF16), NT_DIMS, preferred_element_type=F32)
        kd = (kf * jnp.exp2(b_last - b)).astype(BF16)
        st_ref[hd] = st * jnp.exp2(b_last) + lax.dot_general(
            v, kd, TN_DIMS, preferred_element_type=F32)
        yield

        parts = []
        for blk in GLA_COARSE_LEVELS:
            half = blk // 2
            ref = jnp.concatenate(
                [jnp.broadcast_to(b[s + half - 1:s + half, :], (half, GLA_HEAD_K))
                 for s in range(0, c_len, blk)], axis=0)
            ql = (gather_rows(qf, blk, True) * jnp.exp2(gather_rows(b, blk, True) - ref)).astype(BF16)
            kl = (gather_rows(kf, blk, False) * jnp.exp2(ref - gather_rows(b, blk, False))).astype(BF16)
            a_l = lax.dot_general(ql, kl, NT_DIMS, preferred_element_type=F32)
            yield
            if blk < c_len:
                a_l = jnp.where(hrow // half == hcol // half, a_l, 0.0)
            parts.append((blk, jnp.dot(a_l.astype(BF16), gather_rows(v, blk, False),
                                       preferred_element_type=F32)))
            yield

        attn = jnp.zeros((c_len, c_len), F32)
        for blk, mk in fine_masks:
            half = blk // 2
            ref = jnp.concatenate(
                [jnp.broadcast_to(b[s + half - 1:s + half, :], (blk, GLA_HEAD_K))
                 for s in range(0, c_len, blk)], axis=0)
            ql = (qf * jnp.exp2(b - ref)).astype(BF16)
            kl = (kf * jnp.exp2(ref - b)).astype(BF16)
            a_l = lax.dot_general(ql, kl, NT_DIMS, preferred_element_type=F32)
            yield
            attn = jnp.where(mk, a_l, attn)
        for d in range(GLA_DIAG):
            if d:
                prod = qf * pltpu.roll(kf, d, 0) * jnp.exp2(b - pltpu.roll(b, d, 0))
            else:
                prod = qf * kf
            attn = jnp.where(band == d, jnp.sum(prod, axis=-1, keepdims=True), attn)
        o += jnp.dot(attn.astype(BF16), v, preferred_element_type=F32)
        yield

        groups = [o[g:g + SUBLANES] for g in range(0, c_len, SUBLANES)]
        for blk, part in parts:
            half = blk // 2
            src = 0
            for s in range(0, c_len, blk):
                for g in range((s + half) // SUBLANES, (s + blk) // SUBLANES):
                    groups[g] = groups[g] + part[src:src + SUBLANES]
                    src += SUBLANES
        o = jnp.concatenate(groups, axis=0)

        ms = jnp.mean(o * o, axis=-1, keepdims=True)
        factor = scale * lax.rsqrt(scale * scale * ms + RMS_EPS)
        r = r_ref[0, sl, vsl].astype(F32)
        o_ref[0, sl, vsl] = (o * factor * gain * (r * jax.nn.sigmoid(r))).astype(o_ref.dtype)

    def chunk_group(c, carry):
        chains = []
        for u in range(GLA_CHUNKS_PER_ITER):
            sl = pl.ds(pl.multiple_of((c * GLA_CHUNKS_PER_ITER + u) * c_len, c_len), c_len)
            chains += [head_chunk(hd, sl) for hd in range(heads)]
        _lockstep(chains)
        return carry

    assert n_chunks % GLA_CHUNKS_PER_ITER == 0
    lax.fori_loop(0, n_chunks // GLA_CHUNKS_PER_ITER, chunk_group, 0)


def _gla_mix(qkvr, la, gain, b_sz, s_len, t_len, heads):
    n_chunks = t_len // GLA_CHUNK
    wk = heads * GLA_HEAD_K
    wv = heads * GLA_HEAD_V
    kq = GLA_KEY_DIM // wk
    kv = 2 * GLA_KEY_DIM // wv
    kr = kv + GLA_VAL_DIM // wv
    return pl.pallas_call(
        functools.partial(_gla_kernel, n_chunks=n_chunks, heads=heads),
        grid=(b_sz, GLA_HEADS // heads, s_len // t_len),
        in_specs=[
            pl.BlockSpec((1, t_len, wk), lambda b, h, t: (b, t, h)),
            pl.BlockSpec((1, t_len, wk), lambda b, h, t: (b, t, kq + h)),
            pl.BlockSpec((1, t_len, wv), lambda b, h, t: (b, t, kv + h)),
            pl.BlockSpec((1, t_len, wv), lambda b, h, t: (b, t, kr + h)),
            pl.BlockSpec((1, t_len, wk), lambda b, h, t: (b, t, h)),
            pl.BlockSpec((1, GLA_HEAD_V), lambda b, h, t: (0, 0)),
        ],
        out_specs=pl.BlockSpec((1, t_len, wv), lambda b, h, t: (b, t, h)),
        out_shape=jax.ShapeDtypeStruct((b_sz, s_len, GLA_VAL_DIM), BF16),
        scratch_shapes=[pltpu.VMEM((heads, GLA_HEAD_V, GLA_HEAD_K), F32)],
        compiler_params=_params("parallel", "parallel", "arbitrary"),
        name="gla_mix",
    )(qkvr, qkvr, qkvr, qkvr, la, gain)


def _residual_matmul_kernel(x_ref, a_ref, w_ref, o_ref, wb_ref):
    @pl.when(pl.program_id(0) == 0)
    def _():
        for r0 in range(0, w_ref.shape[1], NORM_SLAB):
            wb_ref[r0:r0 + NORM_SLAB, :] = w_ref[0, r0:r0 + NORM_SLAB, :].astype(BF16)

    o_ref[...] = x_ref[...] + jnp.dot(a_ref[...], wb_ref[...], preferred_element_type=F32)


def _residual_matmul(x2, a, w3, tm):
    m, k_dim = a.shape
    n = w3.shape[2]
    return pl.pallas_call(
        _residual_matmul_kernel,
        grid=(m // tm,),
        in_specs=[
            pl.BlockSpec((tm, n), lambda i: (i, 0)),
            pl.BlockSpec((tm, k_dim), lambda i: (i, 0)),
            pl.BlockSpec((1, k_dim, n), lambda i: (0, 0, 0), pipeline_mode=pl.Buffered(1)),
        ],
        out_specs=pl.BlockSpec((tm, n), lambda i: (i, 0)),
        out_shape=jax.ShapeDtypeStruct((m, n), F32),
        scratch_shapes=[pltpu.VMEM((k_dim, n), BF16)],
        compiler_params=_params("arbitrary"),
        name="residual_matmul",
    )(x2, a, w3)


def _ffn_kernel(x_ref, g_ref, wg_hbm, wu_hbm, wd_hbm, o_ref, h_ref, wg_buf, wu_buf, wd_buf, sem,
                *, layer, tf):
    i = pl.program_id(0)
    n_f = D_FF // tf
    assert n_f % 2 == 0

    def tile_copies(f, slot):
        cols = pl.ds(pl.multiple_of(f * tf, tf), tf)
        return (
            pltpu.make_async_copy(wg_hbm.at[layer, :, cols], wg_buf.at[slot], sem.at[0, slot]),
            pltpu.make_async_copy(wu_hbm.at[layer, :, cols], wu_buf.at[slot], sem.at[1, slot]),
            pltpu.make_async_copy(wd_hbm.at[layer, cols, :], wd_buf.at[slot], sem.at[2, slot]),
        )

    @pl.when(i == 0)
    def _():
        for cp in tile_copies(0, 0):
            cp.start()

    _norm_to_scratch(x_ref, g_ref, h_ref)

    def use_tile(f, slot, acc_ref):
        for cp in tile_copies(f, slot):
            cp.wait()
        nxt = jnp.where(f + 1 == n_f, 0, f + 1)
        for cp in tile_copies(nxt, 1 - slot):
            cp.start()
        h = h_ref[...]
        gate = jnp.dot(h, wg_buf[slot].astype(BF16), preferred_element_type=F32)
        up = jnp.dot(h, wu_buf[slot].astype(BF16), preferred_element_type=F32)
        act = (gate * jax.nn.sigmoid(gate) * up).astype(BF16)
        o_ref[...] = acc_ref[...] + jnp.dot(act, wd_buf[slot].astype(BF16),
                                            preferred_element_type=F32)

    def pair(p, carry):
        use_tile(2 * p, 0, o_ref)
        use_tile(2 * p + 1, 1, o_ref)
        return carry

    use_tile(0, 0, x_ref)
    use_tile(1, 1, o_ref)
    lax.fori_loop(1, n_f // 2, pair, 0)

    @pl.when(i == pl.num_programs(0) - 1)
    def _():
        for cp in tile_copies(0, 0):
            cp.wait()


def _ffn(x2, g, wg, wu, wd, layer, tm, tf):
    m = x2.shape[0]
    return pl.pallas_call(
        functools.partial(_ffn_kernel, layer=layer, tf=tf),
        grid=(m // tm,),
        in_specs=[
            pl.BlockSpec((tm, D_MODEL), lambda i: (i, 0)),
            pl.BlockSpec((1, D_MODEL), lambda i: (0, 0)),
            pl.BlockSpec(memory_space=pl.ANY),
            pl.BlockSpec(memory_space=pl.ANY),
            pl.BlockSpec(memory_space=pl.ANY),
        ],
        out_specs=pl.BlockSpec((tm, D_MODEL), lambda i: (i, 0)),
        out_shape=jax.ShapeDtypeStruct((m, D_MODEL), F32),
        scratch_shapes=[
            pltpu.VMEM((tm, D_MODEL), BF16),
            pltpu.VMEM((2, D_MODEL, tf), F32),
            pltpu.VMEM((2, D_MODEL, tf), F32),
            pltpu.VMEM((2, tf, D_MODEL), F32),
            pltpu.SemaphoreType.DMA((3, 2)),
        ],
        compiler_params=_params("arbitrary"),
        name="ffn",
    )(x2, g, wg, wu, wd)


def _fox_proj_kernel(x_ref, g_ref, w_ref, wf_ref, bf_ref, qkg_ref, o_ref, lf_ref, h_ref,
                     *, q_tiles, qk_tiles):
    j = pl.program_id(1)

    @pl.when(j == 0)
    def _():
        _norm_to_scratch(x_ref, g_ref, h_ref)
        h = h_ref[...]
        z = lax.dot_general(h, wf_ref[...].astype(BF16), NT_DIMS,
                            preferred_element_type=F32) + bf_ref[...]
        lf_ref[...] = _log_sigmoid(z)

    @pl.when(j < qk_tiles)
    def _():
        acc = jnp.dot(h_ref[...], w_ref[...], preferred_element_type=F32)
        mult = jnp.where(j < q_tiles, FOX_SCALE * LOG2E, 1.0).astype(F32)
        for c in range(acc.shape[1] // FOX_HEAD_DIM):
            cs = slice(c * FOX_HEAD_DIM, (c + 1) * FOX_HEAD_DIM)
            xs = acc[:, cs]
            ms = jnp.mean(xs * xs, axis=-1, keepdims=True)
            o_ref[:, cs] = (xs * (lax.rsqrt(ms + RMS_EPS) * mult) * qkg_ref[:, cs]).astype(o_ref.dtype)

    @pl.when(j >= qk_tiles)
    def _():
        o_ref[...] = jnp.dot(h_ref[...], w_ref[...], preferred_element_type=F32).astype(o_ref.dtype)


def _fox_proj(x2, g, w_main, wf, bf, qk_gain, tm, tn):
    m = x2.shape[0]
    q_tiles = D_MODEL // tn
    qk_tiles = 2 * q_tiles
    return pl.pallas_call(
        functools.partial(_fox_proj_kernel, q_tiles=q_tiles, qk_tiles=qk_tiles),
        grid=(m // tm, FOX_MAIN // tn),
        in_specs=[
            pl.BlockSpec((tm, D_MODEL), lambda i, j: (i, 0)),
            pl.BlockSpec((1, D_MODEL), lambda i, j: (0, 0)),
            pl.BlockSpec((D_MODEL, tn), lambda i, j: (0, j)),
            pl.BlockSpec((LANES, D_MODEL), lambda i, j: (0, 0)),
            pl.BlockSpec((1, LANES), lambda i, j: (0, 0)),
            pl.BlockSpec((1, tn), lambda i, j: (0, jnp.minimum(j, qk_tiles - 1))),
        ],
        out_specs=[
            pl.BlockSpec((tm, tn), lambda i, j: (i, j)),
            pl.BlockSpec((tm, LANES), lambda i, j: (i, 0)),
        ],
        out_shape=[
            jax.ShapeDtypeStruct((m, FOX_MAIN), BF16),
            jax.ShapeDtypeStruct((m, LANES), F32),
        ],
        scratch_shapes=[pltpu.VMEM((tm, D_MODEL), BF16)],
        compiler_params=_params("parallel", "arbitrary"),
        name="fox_proj",
    )(x2, g, w_main, wf, bf, qk_gain)


def _cumsum_kernel(lf_ref, c_ref, *, t_len):
    row = lax.broadcasted_iota(jnp.int32, (t_len, t_len), 0)
    col = lax.broadcasted_iota(jnp.int32, (t_len, t_len), 1)
    tril = (col <= row).astype(BF16)

    def block(t, carry):
        sl = pl.ds(pl.multiple_of(t * t_len, t_len), t_len)
        c = _cumsum_rows(tril, lf_ref[0, sl, :]) + carry
        c_ref[0, sl, :] = c * LOG2E
        return c[t_len - 1:t_len, :]

    lax.fori_loop(0, lf_ref.shape[1] // t_len, block, jnp.zeros((1, LANES), F32))


def _seq_cumsum(lf, t_len):
    b_sz, s_len, _ = lf.shape
    return pl.pallas_call(
        functools.partial(_cumsum_kernel, t_len=t_len),
        grid=(b_sz,),
        in_specs=[pl.BlockSpec((1, s_len, LANES), lambda b: (b, 0, 0))],
        out_specs=pl.BlockSpec((1, s_len, LANES), lambda b: (b, 0, 0)),
        out_shape=jax.ShapeDtypeStruct(lf.shape, F32),
        compiler_params=_params("parallel"),
        name="fox_cumsum",
    )(lf)


def _fox_attn_kernel(ends_ref, firsts_ref, slack_ref, q_ref, k_ref, v_ref, og_ref, ck_ref, o_ref,
                     acc_ref, va_ref, *, blk, n_sub, tk):
    i = pl.program_id(2)
    sub = blk // n_sub
    assert sub == tk
    n_kt = ck_ref.shape[2]
    bh = pl.program_id(0) * pl.num_programs(1) + pl.program_id(1)
    n_loop = jnp.maximum(i * n_sub - 1, 0)
    limit = firsts_ref[bh * pl.num_programs(2) + i] + slack_ref[0]

    def first_needed(j, lo):
        return jnp.minimum(lo, jnp.where(ends_ref[bh * n_kt + j] > limit, n_loop, j))

    j_start = lax.fori_loop(0, n_loop, first_needed, n_loop)

    c0 = ck_ref[0, 0, pl.ds(i * n_sub, 1), :][:, 0:1]
    acc_ref[...] = jnp.zeros_like(acc_ref)

    @pl.when(i == 0)
    def _():
        for r0 in range(0, va_ref.shape[0], blk):
            va_ref[r0:r0 + blk, :FOX_HEAD_DIM] = v_ref[0, r0:r0 + blk, :]
            va_ref[r0:r0 + blk, FOX_HEAD_DIM:] = jnp.ones((blk, FOX_HEAD_DIM), BF16)

    def update(r, out, m_old, ks, va, bias, mask_offset):
        rows = slice(r * sub, (r + 1) * sub)
        s = lax.dot_general(q_ref[0, rows, :], ks, NT_DIMS, preferred_element_type=F32) + bias
        yield
        if mask_offset is not None:
            row = lax.broadcasted_iota(jnp.int32, s.shape, 0)
            col = lax.broadcasted_iota(jnp.int32, s.shape, 1)
            s = jnp.where(col <= row + mask_offset, s, -jnp.inf)
        m_new = jnp.maximum(m_old, jnp.max(s, axis=-1, keepdims=True))
        p = jnp.exp2(s - m_new).astype(BF16)
        alpha = jnp.exp2(m_old - m_new)
        acc_ref[rows, :] = alpha * acc_ref[rows, :] + jnp.dot(p, va, preferred_element_type=F32)
        out[r] = m_new

    def body(j, carry):
        sl = pl.ds(pl.multiple_of(j * tk, tk), tk)
        ks = k_ref[0, sl, :]
        va = va_ref[sl, :]
        bias = c0 - ck_ref[0, 0, pl.ds(j, 1), :]
        out = [None] * n_sub
        _lockstep([update(r, out, carry[r], ks, va, bias, None) for r in range(n_sub)])
        return tuple(out)

    init = tuple(jnp.full((sub, 1), -jnp.inf, F32) for _ in range(n_sub))
    carry = lax.fori_loop(j_start, n_loop, body, init)

    carry = list(carry)
    for r in range(1, n_sub):
        t_prev = i * n_sub + r - 1
        limit_r = ends_ref[bh * n_kt + t_prev] + slack_ref[0]
        for u in range(r):
            g = i * n_sub - 1 + u

            def visit(g=g, r=r):
                out = [None] * n_sub
                sl = pl.ds(pl.multiple_of(g * tk, tk), tk)
                bias = c0 - ck_ref[0, 0, pl.ds(g, 1), :]
                _lockstep([update(r, out, carry[r], k_ref[0, sl, :], va_ref[sl, :], bias, None)])
                return out[r]

            needed = (i > 0) & (ends_ref[bh * n_kt + jnp.maximum(g, 0)] <= limit_r)
            carry[r] = lax.cond(needed, visit, lambda r=r: carry[r])

    final = [None] * n_sub
    steps = []
    for r in range(n_sub):
        t_own = i * n_sub + r
        t_lo = jnp.maximum(t_own - 1, 0)
        win = pl.ds(pl.multiple_of(t_lo * tk, tk), 2 * tk)
        ck_win = jnp.concatenate(
            [ck_ref[0, 0, pl.ds(t_lo, 1), :], ck_ref[0, 0, pl.ds(t_lo + 1, 1), :]], axis=1)
        steps.append(update(r, final, carry[r], k_ref[0, win, :], va_ref[win, :],
                            c0 - ck_win, (t_own - t_lo) * tk))
    _lockstep(steps)
    for r in range(n_sub):
        rows = slice(r * sub, (r + 1) * sub)
        acc = acc_ref[rows, :]
        gate = jax.nn.sigmoid(og_ref[0, rows, :].astype(F32))
        o_ref[0, rows, :] = (acc[:, :FOX_HEAD_DIM] / acc[:, FOX_HEAD_DIM:FOX_HEAD_DIM + 1]
                             * gate).astype(o_ref.dtype)


def _fox_attn(qkvo, ck, slack, b_sz, s_len, blk, n_sub):
    h_cols = D_MODEL // FOX_HEAD_DIM
    n_blk = s_len // blk
    n_kt, tk = ck.shape[2:]
    ends = ck[:, :, :, tk - 1].reshape(-1)
    firsts = ck.reshape(b_sz, FOX_HEADS, n_blk, blk)[:, :, :, 0].reshape(-1)
    grid_spec = pltpu.PrefetchScalarGridSpec(
        num_scalar_prefetch=3,
        grid=(b_sz, FOX_HEADS, n_blk),
        in_specs=[
            pl.BlockSpec((1, blk, FOX_HEAD_DIM), lambda b, h, i, *_: (b, i, h)),
            pl.BlockSpec((1, s_len, FOX_HEAD_DIM), lambda b, h, i, *_: (b, 0, h_cols + h)),
            pl.BlockSpec((1, s_len, FOX_HEAD_DIM), lambda b, h, i, *_: (b, 0, 2 * h_cols + h)),
            pl.BlockSpec((1, blk, FOX_HEAD_DIM), lambda b, h, i, *_: (b, i, 3 * h_cols + h)),
            pl.BlockSpec((1, 1, n_kt, tk), lambda b, h, i, *_: (b, h, 0, 0)),
        ],
        out_specs=pl.BlockSpec((1, blk, FOX_HEAD_DIM), lambda b, h, i, *_: (b, i, h)),
        scratch_shapes=[
            pltpu.VMEM((blk, 2 * FOX_HEAD_DIM), F32),
            pltpu.VMEM((s_len, 2 * FOX_HEAD_DIM), BF16),
        ],
    )
    return pl.pallas_call(
        functools.partial(_fox_attn_kernel, blk=blk, n_sub=n_sub, tk=tk),
        grid_spec=grid_spec,
        out_shape=jax.ShapeDtypeStruct((b_sz, s_len, D_MODEL), BF16),
        compiler_params=_params("parallel", "parallel", "arbitrary"),
        name="fox_attn",
    )(ends, firsts, slack, qkvo, qkvo, qkvo, qkvo, ck)


def _pad_rows(w, n):
    return jnp.pad(w, ((0, n - w.shape[0]), (0, 0)))


def kernel(x, norm_mix, norm_ffn, gla_w_in, gla_w_g2, gla_b_g2, gla_o_gain, gla_w_o,
           fox_w_in, fox_b_f, fox_q_gain, fox_k_gain, fox_w_o,
           ffn_w_gate, ffn_w_up, ffn_w_down):
    b_sz, s_len, d = x.shape
    assert d == D_MODEL and s_len % GLA_CHUNK == 0
    m = b_sz * s_len
    tm = min(1024, m)
    tn = 1024
    tm_res = min(512, m)
    tm_ffn = min(1024, m)
    tf = 256
    gla_t = min(512, s_len)
    attn_blk = min(1024, s_len)
    attn_tk = min(512, s_len // 2)
    attn_sub = attn_tk
    cum_t = min(256, s_len)

    x2 = x.reshape(m, d)

    def ffn_layer(xin, layer):
        return _ffn(xin, norm_ffn[layer][None, :], ffn_w_gate, ffn_w_up, ffn_w_down, layer, tm_ffn, tf)

    gla_wt = jnp.swapaxes(gla_w_in, 1, 2)
    qkvr, la = _gla_proj(
        x2, norm_mix[0][None, :],
        gla_wt,
        _pad_rows(gla_wt[0, GLA_MAIN:, :], LANES),
        _pad_rows(gla_w_g2[0], LANES),
        gla_b_g2[0][None, :], tm, tn)
    og = _gla_mix(qkvr.reshape(b_sz, s_len, GLA_MAIN), la.reshape(b_sz, s_len, GLA_KEY_DIM),
                  gla_o_gain[0][None, :], b_sz, s_len, gla_t, 4)
    x2 = _residual_matmul(x2, og.reshape(m, GLA_VAL_DIM), gla_w_o, tm_res)
    x2 = ffn_layer(x2, 0)

    qk_gain = jnp.concatenate([jnp.tile(fox_q_gain[0], FOX_HEADS), jnp.tile(fox_k_gain[0], FOX_HEADS)])
    f_lo, f_hi = 3 * D_MODEL, 3 * D_MODEL + FOX_HEADS
    fox_wt = jnp.swapaxes(fox_w_in, 1, 2)
    qkvo, lf = _fox_proj(
        x2, norm_mix[1][None, :],
        _cast_weight_t(fox_wt, 0, FOX_MAIN, f_lo, FOX_HEADS),
        _pad_rows(fox_wt[0, f_lo:f_hi, :], LANES),
        jnp.pad(fox_b_f[0], (0, LANES - FOX_HEADS))[None, :],
        qk_gain[None, :], tm, tn)
    c = _seq_cumsum(lf.reshape(b_sz, s_len, LANES), cum_t)
    c_hs = c[:, :, :FOX_HEADS].transpose(0, 2, 1)
    ck = c_hs.reshape(b_sz, FOX_HEADS, s_len // attn_tk, attn_tk)
    qk_max = (FOX_HEAD_DIM * FOX_SCALE * LOG2E * BF16_SLOP
              * jnp.max(jnp.abs(fox_q_gain[0])) * jnp.max(jnp.abs(fox_k_gain[0])))
    slack = (2.0 * qk_max + F32_EXP2_FLOOR).reshape(1)
    o = _fox_attn(qkvo.reshape(b_sz, s_len, FOX_MAIN), ck, slack, b_sz, s_len, attn_blk,
                  attn_blk // attn_sub)
    x2 = _residual_matmul(x2, o.reshape(m, D_MODEL), fox_w_o, tm_res)
    x2 = ffn_layer(x2, 1)
    return x2.reshape(b_sz, s_len, d)
```

```python
import functools

import jax
import jax.numpy as jnp
from jax import lax
from jax.experimental import pallas as pl
from jax.experimental.pallas import tpu as pltpu

F32 = jnp.float32
BF16 = jnp.bfloat16

D_MODEL = 2048
RMS_EPS = 1e-6

GLA_HEADS = 4
GLA_KEY_DIM = D_MODEL // 2
GLA_VAL_DIM = D_MODEL
GLA_HEAD_K = GLA_KEY_DIM // GLA_HEADS
GLA_HEAD_V = GLA_VAL_DIM // GLA_HEADS
GLA_GATE_RANK = 16
GLA_GATE_TEMP = 16.0
GLA_CHUNK = 64
GLA_MAIN = 2 * GLA_KEY_DIM + 2 * GLA_VAL_DIM
GLA_COARSE_LEVELS = (64, 32)
GLA_FINE_LEVELS = (16, 8)
GLA_DIAG = 4
GLA_CHUNKS_PER_ITER = 2

FOX_HEAD_DIM = 128
FOX_HEADS = D_MODEL // FOX_HEAD_DIM
FOX_MAIN = 4 * D_MODEL
FOX_SCALE = FOX_HEAD_DIM ** -0.5
LOG2E = 1.4426950408889634
BF16_SLOP = 1.02
F32_EXP2_FLOOR = 160.0

D_FF = ((8 * D_MODEL + 2) // 3 + 255) // 256 * 256

LANES = 128
SUBLANES = 8
NORM_SLAB = 256
VMEM_LIMIT = 58 * 1024 * 1024

_DONE = object()


def _lockstep(chains):
    pending = list(chains)
    while pending:
        pending = [g for g in pending if next(g, _DONE) is not _DONE]


NT_DIMS = (((1,), (1,)), ((), ()))
TN_DIMS = (((0,), (0,)), ((), ()))


def _params(*sem, flags=None):
    return pltpu.CompilerParams(dimension_semantics=sem, vmem_limit_bytes=VMEM_LIMIT, flags=flags)


def _rmsnorm_rows(x, g):
    ms = jnp.mean(x * x, axis=-1, keepdims=True)
    return x * lax.rsqrt(ms + RMS_EPS) * g


def _log_sigmoid(z):
    return jnp.minimum(z, 0.0) - jnp.log(1.0 + jnp.exp(-jnp.abs(z)))


def _norm_to_scratch(x_ref, g_ref, h_ref):
    rows = x_ref.shape[0]
    slab = min(NORM_SLAB, rows)
    g = g_ref[...]
    for r0 in range(0, rows, slab):
        h_ref[r0:r0 + slab, :] = _rmsnorm_rows(x_ref[r0:r0 + slab, :], g).astype(BF16)


def _split3(a):
    hi = a.astype(BF16)
    r1 = a - hi.astype(F32)
    mid = r1.astype(BF16)
    lo = (r1 - mid.astype(F32)).astype(BF16)
    return hi, mid, lo


def _cumsum_rows(tril, a, terms=3):
    parts = _split3(a)[:terms]
    out = jnp.dot(tril, parts[0], preferred_element_type=F32)
    for part in parts[1:]:
        out += jnp.dot(tril, part, preferred_element_type=F32)
    return out


def _gla_proj_kernel(x_ref, g_ref, w_ref, wg1_ref, wg2_ref, bg_ref, o_ref, la_ref, h_ref):
    @pl.when(pl.program_id(1) == 0)
    def _():
        _norm_to_scratch(x_ref, g_ref, h_ref)
        h = h_ref[...]
        g1 = lax.dot_general(h, wg1_ref[...].astype(BF16), NT_DIMS, preferred_element_type=F32)
        z = jnp.dot(g1.astype(BF16), wg2_ref[...].astype(BF16),
                    preferred_element_type=F32) + bg_ref[...]
        la_ref[...] = _log_sigmoid(z) * (LOG2E / GLA_GATE_TEMP)

    o_ref[...] = lax.dot_general(h_ref[...], w_ref[0].astype(BF16), NT_DIMS,
                                 preferred_element_type=F32).astype(o_ref.dtype)


def _gla_proj(x2, g, wt3, wg1, wg2, bg, tm, tn):
    m = x2.shape[0]
    return pl.pallas_call(
        _gla_proj_kernel,
        grid=(m // tm, GLA_MAIN // tn),
        in_specs=[
            pl.BlockSpec((tm, D_MODEL), lambda i, j: (i, 0)),
            pl.BlockSpec((1, D_MODEL), lambda i, j: (0, 0)),
            pl.BlockSpec((1, tn, D_MODEL), lambda i, j: (0, j, 0)),
            pl.BlockSpec((LANES, D_MODEL), lambda i, j: (0, 0)),
            pl.BlockSpec((LANES, GLA_KEY_DIM), lambda i, j: (0, 0)),
            pl.BlockSpec((1, GLA_KEY_DIM), lambda i, j: (0, 0)),
        ],
        out_specs=[
            pl.BlockSpec((tm, tn), lambda i, j: (i, j)),
            pl.BlockSpec((tm, GLA_KEY_DIM), lambda i, j: (i, 0)),
        ],
        out_shape=[
            jax.ShapeDtypeStruct((m, GLA_MAIN), BF16),
            jax.ShapeDtypeStruct((m, GLA_KEY_DIM), F32),
        ],
        scratch_shapes=[pltpu.VMEM((tm, D_MODEL), BF16)],
        compiler_params=_params("parallel", "arbitrary"),
        name="gla_proj",
    )(x2, g, wt3, wg1, wg2, bg)


def _gla_kernel(q_ref, k_ref, v_ref, r_ref, la_ref, gain_ref, o_ref, st_ref, *, n_chunks, heads):
    c_len = GLA_CHUNK

    @pl.when(pl.program_id(2) == 0)
    def _():
        st_ref[...] = jnp.zeros_like(st_ref)

    row = lax.broadcasted_iota(jnp.int32, (c_len, c_len), 0)
    col = lax.broadcasted_iota(jnp.int32, (c_len, c_len), 1)
    tril = (col <= row).astype(BF16)
    delta = row - col
    band = jnp.where((delta >= 0) & (delta <= row % GLA_DIAG), delta, -1)
    fine_masks = [
        (blk, (row // blk == col // blk) & (row % blk >= blk // 2) & (col % blk < blk // 2))
        for blk in GLA_FINE_LEVELS]
    half_rows = c_len // 2
    hrow = lax.broadcasted_iota(jnp.int32, (half_rows, half_rows), 0)
    hcol = lax.broadcasted_iota(jnp.int32, (half_rows, half_rows), 1)
    gain = gain_ref[...]
    scale = GLA_HEAD_K ** -0.5

    def gather_rows(a, blk, second_half):
        half = blk // 2
        off = half if second_half else 0
        return jnp.concatenate([a[s + off:s + off + half] for s in range(0, c_len, blk)], axis=0)

    def head_chunk(hd, sl):
        ksl = slice(hd * GLA_HEAD_K, (hd + 1) * GLA_HEAD_K)
        vsl = slice(hd * GLA_HEAD_V, (hd + 1) * GLA_HEAD_V)
        la = la_ref[0, sl, ksl]
        b = _cumsum_rows(tril, la, terms=2)
        yield
        b_last = b[c_len - 1:c_len, :]
        qf = q_ref[0, sl, ksl].astype(F32)
        kf = k_ref[0, sl, ksl].astype(F32)
        v = v_ref[0, sl, vsl]

        st = st_ref[hd]
        qi = (qf * jnp.exp2(b)).astype(BF16)
        o = lax.dot_general(qi, st.astype(BF16), NT_DIMS, preferred_element_type=F32)
        kd = (kf * jnp.exp2(b_last - b)).astype(BF16)
        st_ref[hd] = st * jnp.exp2(b_last) + lax.dot_general(
            v, kd, TN_DIMS, preferred_element_type=F32)
        yield

        parts = []
        for blk in GLA_COARSE_LEVELS:
            half = blk // 2
            ref = jnp.concatenate(
                [jnp.broadcast_to(b[s + half - 1:s + half, :], (half, GLA_HEAD_K))
                 for s in range(0, c_len, blk)], axis=0)
            ql = (gather_rows(qf, blk, True) * jnp.exp2(gather_rows(b, blk, True) - ref)).astype(BF16)
            kl = (gather_rows(kf, blk, False) * jnp.exp2(ref - gather_rows(b, blk, False))).astype(BF16)
            a_l = lax.dot_general(ql, kl, NT_DIMS, preferred_element_type=F32)
            yield
            if blk < c_len:
                a_l = jnp.where(hrow // half == hcol // half, a_l, 0.0)
            parts.append((blk, jnp.dot(a_l.astype(BF16), gather_rows(v, blk, False),
                                       preferred_element_type=F32)))
            yield

        attn = jnp.zeros((c_len, c_len), F32)
        for blk, mk in fine_masks:
            half = blk // 2
            ref = jnp.concatenate(
                [jnp.broadcast_to(b[s + half - 1:s + half, :], (blk, GLA_HEAD_K))
                 for s in range(0, c_len, blk)], axis=0)
            ql = (qf * jnp.exp2(b - ref)).astype(BF16)
            kl = (kf * jnp.exp2(ref - b)).astype(BF16)
            a_l = lax.dot_general(ql, kl, NT_DIMS, preferred_element_type=F32)
            yield
            attn = jnp.where(mk, a_l, attn)
        for d in range(GLA_DIAG):
            if d:
                prod = qf * pltpu.roll(kf, d, 0) * jnp.exp2(b - pltpu.roll(b, d, 0))
            else:
                prod = qf * kf
            attn = jnp.where(band == d, jnp.sum(prod, axis=-1, keepdims=True), attn)
        o += jnp.dot(attn.astype(BF16), v, preferred_element_type=F32)
        yield

        groups = [o[g:g + SUBLANES] for g in range(0, c_len, SUBLANES)]
        for blk, part in parts:
            half = blk // 2
            src = 0
            for s in range(0, c_len, blk):
                for g in range((s + half) // SUBLANES, (s + blk) // SUBLANES):
                    groups[g] = groups[g] + part[src:src + SUBLANES]
                    src += SUBLANES
        o = jnp.concatenate(groups, axis=0)

        ms = jnp.mean(o * o, axis=-1, keepdims=True)
        factor = scale * lax.rsqrt(scale * scale * ms + RMS_EPS)
        r = r_ref[0, sl, vsl].astype(F32)
        o_ref[0, sl, vsl] = (o * factor * gain * (r * jax.nn.sigmoid(r))).astype(o_ref.dtype)

    def chunk_group(c, carry):
        chains = []
        for u in range(GLA_CHUNKS_PER_ITER):
            sl = pl.ds(pl.multiple_of((c * GLA_CHUNKS_PER_ITER + u) * c_len, c_len), c_len)
            chains += [head_chunk(hd, sl) for hd in range(heads)]
        _lockstep(chains)
        return carry

    assert n_chunks % GLA_CHUNKS_PER_ITER == 0
    lax.fori_loop(0, n_chunks // GLA_CHUNKS_PER_ITER, chunk_group, 0)


def _gla_mix(qkvr, la, gain, b_sz, s_len, t_len, heads):
    n_chunks = t_len // GLA_CHUNK
    wk = heads * GLA_HEAD_K
    wv = heads * GLA_HEAD_V
    kq = GLA_KEY_DIM // wk
    kv = 2 * GLA_KEY_DIM // wv
    kr = kv + GLA_VAL_DIM // wv
    return pl.pallas_call(
        functools.partial(_gla_kernel, n_chunks=n_chunks, heads=heads),
        grid=(b_sz, GLA_HEADS // heads, s_len // t_len),
        in_specs=[
            pl.BlockSpec((1, t_len, wk), lambda b, h, t: (b, t, h)),
            pl.BlockSpec((1, t_len, wk), lambda b, h, t: (b, t, kq + h)),
            pl.BlockSpec((1, t_len, wv), lambda b, h, t: (b, t, kv + h)),
            pl.BlockSpec((1, t_len, wv), lambda b, h, t: (b, t, kr + h)),
            pl.BlockSpec((1, t_len, wk), lambda b, h, t: (b, t, h)),
            pl.BlockSpec((1, GLA_HEAD_V), lambda b, h, t: (0, 0)),
        ],
        out_specs=pl.BlockSpec((1, t_len, wv), lambda b, h, t: (b, t, h)),
        out_shape=jax.ShapeDtypeStruct((b_sz, s_len, GLA_VAL_DIM), BF16),
        scratch_shapes=[pltpu.VMEM((heads, GLA_HEAD_V, GLA_HEAD_K), F32)],
        compiler_params=_params("parallel", "parallel", "arbitrary"),
        name="gla_mix",
    )(qkvr, qkvr, qkvr, qkvr, la, gain)


def _residual_matmul_kernel(x_ref, a_ref, w_ref, o_ref, wb_ref):
    @pl.when(pl.program_id(0) == 0)
    def _():
        for r0 in range(0, w_ref.shape[1], NORM_SLAB):
            wb_ref[r0:r0 + NORM_SLAB, :] = w_ref[0, r0:r0 + NORM_SLAB, :].astype(BF16)

    o_ref[...] = x_ref[...] + jnp.dot(a_ref[...], wb_ref[...], preferred_element_type=F32)


def _residual_matmul(x2, a, w3, tm):
    m, k_dim = a.shape
    n = w3.shape[2]
    return pl.pallas_call(
        _residual_matmul_kernel,
        grid=(m // tm,),
        in_specs=[
            pl.BlockSpec((tm, n), lambda i: (i, 0)),
            pl.BlockSpec((tm, k_dim), lambda i: (i, 0)),
            pl.BlockSpec((1, k_dim, n), lambda i: (0, 0, 0), pipeline_mode=pl.Buffered(1)),
        ],
        out_specs=pl.BlockSpec((tm, n), lambda i: (i, 0)),
        out_shape=jax.ShapeDtypeStruct((m, n), F32),
        scratch_shapes=[pltpu.VMEM((k_dim, n), BF16)],
        compiler_params=_params("arbitrary"),
        name="residual_matmul",
    )(x2, a, w3)


def _ffn_kernel(x_ref, g_ref, wg_hbm, wu_hbm, wd_hbm, o_ref, h_ref, wg_buf, wu_buf, wd_buf, sem,
                *, layer, tf):
    i = pl.program_id(0)
    n_f = D_FF // tf
    assert n_f % 2 == 0

    def tile_copies(f, slot):
        cols = pl.ds(pl.multiple_of(f * tf, tf), tf)
        return (
            pltpu.make_async_copy(wg_hbm.at[layer, :, cols], wg_buf.at[slot], sem.at[0, slot]),
            pltpu.make_async_copy(wu_hbm.at[layer, :, cols], wu_buf.at[slot], sem.at[1, slot]),
            pltpu.make_async_copy(wd_hbm.at[layer, cols, :], wd_buf.at[slot], sem.at[2, slot]),
        )

    @pl.when(i == 0)
    def _():
        for cp in tile_copies(0, 0):
            cp.start()

    _norm_to_scratch(x_ref, g_ref, h_ref)

    def use_tile(f, slot, acc_ref):
        for cp in tile_copies(f, slot):
            cp.wait()
        nxt = jnp.where(f + 1 == n_f, 0, f + 1)
        for cp in tile_copies(nxt, 1 - slot):
            cp.start()
        h = h_ref[...]
        gate = jnp.dot(h, wg_buf[slot].astype(BF16), preferred_element_type=F32)
        up = jnp.dot(h, wu_buf[slot].astype(BF16), preferred_element_type=F32)
        act = (gate * jax.nn.sigmoid(gate) * up).astype(BF16)
        o_ref[...] = acc_ref[...] + jnp.dot(act, wd_buf[slot].astype(BF16),
                                            preferred_element_type=F32)

    def pair(p, carry):
        use_tile(2 * p, 0, o_ref)
        use_tile(2 * p + 1, 1, o_ref)
        return carry

    use_tile(0, 0, x_ref)
    use_tile(1, 1, o_ref)
    lax.fori_loop(1, n_f // 2, pair, 0)

    @pl.when(i == pl.num_programs(0) - 1)
    def _():
        for cp in tile_copies(0, 0):
            cp.wait()


def _ffn(x2, g, wg, wu, wd, layer, tm, tf):
    m = x2.shape[0]
    return pl.pallas_call(
        functools.partial(_ffn_kernel, layer=layer, tf=tf),
        grid=(m // tm,),
        in_specs=[
            pl.BlockSpec((tm, D_MODEL), lambda i: (i, 0)),
            pl.BlockSpec((1, D_MODEL), lambda i: (0, 0)),
            pl.BlockSpec(memory_space=pl.ANY),
            pl.BlockSpec(memory_space=pl.ANY),
            pl.BlockSpec(memory_space=pl.ANY),
        ],
        out_specs=pl.BlockSpec((tm, D_MODEL), lambda i: (i, 0)),
        out_shape=jax.ShapeDtypeStruct((m, D_MODEL), F32),
        scratch_shapes=[
            pltpu.VMEM((tm, D_MODEL), BF16),
            pltpu.VMEM((2, D_MODEL, tf), F32),
            pltpu.VMEM((2, D_MODEL, tf), F32),
            pltpu.VMEM((2, tf, D_MODEL), F32),
            pltpu.SemaphoreType.DMA((3, 2)),
        ],
        compiler_params=_params("arbitrary"),
        name="ffn",
    )(x2, g, wg, wu, wd)


def _fox_proj_kernel(x_ref, g_ref, w_ref, wf_ref, bf_ref, qkg_ref, o_ref, lf_ref, h_ref,
                     *, q_tiles, qk_tiles):
    j = pl.program_id(1)

    @pl.when(j == 0)
    def _():
        _norm_to_scratch(x_ref, g_ref, h_ref)
        h = h_ref[...]
        z = lax.dot_general(h, wf_ref[...].astype(BF16), NT_DIMS,
                            preferred_element_type=F32) + bf_ref[...]
        lf_ref[...] = _log_sigmoid(z)

    def project():
        return lax.dot_general(h_ref[...], w_ref[0].astype(BF16), NT_DIMS,
                               preferred_element_type=F32)

    @pl.when(j < qk_tiles)
    def _():
        acc = project()
        mult = jnp.where(j < q_tiles, FOX_SCALE * LOG2E, 1.0).astype(F32)
        for c in range(acc.shape[1] // FOX_HEAD_DIM):
            cs = slice(c * FOX_HEAD_DIM, (c + 1) * FOX_HEAD_DIM)
            xs = acc[:, cs]
            ms = jnp.mean(xs * xs, axis=-1, keepdims=True)
            o_ref[:, cs] = (xs * (lax.rsqrt(ms + RMS_EPS) * mult) * qkg_ref[:, cs]).astype(o_ref.dtype)

    @pl.when(j >= qk_tiles)
    def _():
        o_ref[...] = project().astype(o_ref.dtype)


def _fox_proj(x2, g, wt3, wf, bf, qk_gain, tm, tn, gap_at, gap):
    m = x2.shape[0]
    q_tiles = D_MODEL // tn
    qk_tiles = 2 * q_tiles
    assert gap_at % tn == 0 and gap % SUBLANES == 0
    first_after = gap_at // tn
    return pl.pallas_call(
        functools.partial(_fox_proj_kernel, q_tiles=q_tiles, qk_tiles=qk_tiles),
        grid=(m // tm, FOX_MAIN // tn),
        in_specs=[
            pl.BlockSpec((tm, D_MODEL), lambda i, j: (i, 0)),
            pl.BlockSpec((1, D_MODEL), lambda i, j: (0, 0)),
            pl.BlockSpec(
                (pl.Element(1), pl.Element(tn), pl.Element(D_MODEL)),
                lambda i, j: (0, pl.multiple_of(j * tn + jnp.where(j >= first_after, gap, 0),
                                                SUBLANES), 0)),
            pl.BlockSpec((LANES, D_MODEL), lambda i, j: (0, 0)),
            pl.BlockSpec((1, LANES), lambda i, j: (0, 0)),
            pl.BlockSpec((1, tn), lambda i, j: (0, jnp.minimum(j, qk_tiles - 1))),
        ],
        out_specs=[
            pl.BlockSpec((tm, tn), lambda i, j: (i, j)),
            pl.BlockSpec((tm, LANES), lambda i, j: (i, 0)),
        ],
        out_shape=[
            jax.ShapeDtypeStruct((m, FOX_MAIN), BF16),
            jax.ShapeDtypeStruct((m, LANES), F32),
        ],
        scratch_shapes=[pltpu.VMEM((tm, D_MODEL), BF16)],
        compiler_params=_params("parallel", "arbitrary"),
        name="fox_proj",
    )(x2, g, wt3, wf, bf, qk_gain)


def _cumsum_kernel(lf_ref, c_ref, *, t_len):
    row = lax.broadcasted_iota(jnp.int32, (t_len, t_len), 0)
    col = lax.broadcasted_iota(jnp.int32, (t_len, t_len), 1)
    tril = (col <= row).astype(BF16)

    def block(t, carry):
        sl = pl.ds(pl.multiple_of(t * t_len, t_len), t_len)
        c = _cumsum_rows(tril, lf_ref[0, sl, :]) + carry
        c_ref[0, sl, :] = c * LOG2E
        return c[t_len - 1:t_len, :]

    lax.fori_loop(0, lf_ref.shape[1] // t_len, block, jnp.zeros((1, LANES), F32))


def _seq_cumsum(lf, t_len):
    b_sz, s_len, _ = lf.shape
    return pl.pallas_call(
        functools.partial(_cumsum_kernel, t_len=t_len),
        grid=(b_sz,),
        in_specs=[pl.BlockSpec((1, s_len, LANES), lambda b: (b, 0, 0))],
        out_specs=pl.BlockSpec((1, s_len, LANES), lambda b: (b, 0, 0)),
        out_shape=jax.ShapeDtypeStruct(lf.shape, F32),
        compiler_params=_params("parallel"),
        name="fox_cumsum",
    )(lf)


def _fox_attn_kernel(ends_ref, firsts_ref, slack_ref, q_ref, k_ref, v_ref, og_ref, ck_ref, o_ref,
                     acc_ref, va_ref, *, blk, n_sub, tk):
    i = pl.program_id(2)
    sub = blk // n_sub
    assert sub == tk
    n_kt = ck_ref.shape[2]
    bh = pl.program_id(0) * pl.num_programs(1) + pl.program_id(1)
    n_loop = jnp.maximum(i * n_sub - 1, 0)
    limit = firsts_ref[bh * pl.num_programs(2) + i] + slack_ref[0]

    def first_needed(j, lo):
        return jnp.minimum(lo, jnp.where(ends_ref[bh * n_kt + j] > limit, n_loop, j))

    j_start = lax.fori_loop(0, n_loop, first_needed, n_loop)

    c0 = ck_ref[0, 0, pl.ds(i * n_sub, 1), :][:, 0:1]
    acc_ref[...] = jnp.zeros_like(acc_ref)

    @pl.when(i == 0)
    def _():
        for r0 in range(0, va_ref.shape[0], blk):
            va_ref[r0:r0 + blk, :FOX_HEAD_DIM] = v_ref[0, r0:r0 + blk, :]
            va_ref[r0:r0 + blk, FOX_HEAD_DIM:] = jnp.ones((blk, FOX_HEAD_DIM), BF16)

    def update(r, out, m_old, ks, va, bias, mask_offset):
        rows = slice(r * sub, (r + 1) * sub)
        s = lax.dot_general(q_ref[0, rows, :], ks, NT_DIMS, preferred_element_type=F32) + bias
        yield
        if mask_offset is not None:
            row = lax.broadcasted_iota(jnp.int32, s.shape, 0)
            col = lax.broadcasted_iota(jnp.int32, s.shape, 1)
            s = jnp.where(col <= row + mask_offset, s, -jnp.inf)
        m_new = jnp.maximum(m_old, jnp.max(s, axis=-1, keepdims=True))
        p = jnp.exp2(s - m_new).astype(BF16)
        alpha = jnp.exp2(m_old - m_new)
        acc_ref[rows, :] = alpha * acc_ref[rows, :] + jnp.dot(p, va, preferred_element_type=F32)
        out[r] = m_new

    def body(j, carry):
        sl = pl.ds(pl.multiple_of(j * tk, tk), tk)
        ks = k_ref[0, sl, :]
        va = va_ref[sl, :]
        bias = c0 - ck_ref[0, 0, pl.ds(j, 1), :]
        out = [None] * n_sub
        _lockstep([update(r, out, carry[r], ks, va, bias, None) for r in range(n_sub)])
        return tuple(out)

    init = tuple(jnp.full((sub, 1), -jnp.inf, F32) for _ in range(n_sub))
    carry = lax.fori_loop(j_start, n_loop, body, init)

    carry = list(carry)
    for r in range(1, n_sub):
        t_prev = i * n_sub + r - 1
        limit_r = ends_ref[bh * n_kt + t_prev] + slack_ref[0]
        for u in range(r):
            g = i * n_sub - 1 + u

            def visit(g=g, r=r):
                out = [None] * n_sub
                sl = pl.ds(pl.multiple_of(g * tk, tk), tk)
                bias = c0 - ck_ref[0, 0, pl.ds(g, 1), :]
                _lockstep([update(r, out, carry[r], k_ref[0, sl, :], va_ref[sl, :], bias, None)])
                return out[r]

            needed = (i > 0) & (ends_ref[bh * n_kt + jnp.maximum(g, 0)] <= limit_r)
            carry[r] = lax.cond(needed, visit, lambda r=r: carry[r])

    final = [None] * n_sub
    steps = []
    for r in range(n_sub):
        t_own = i * n_sub + r
        t_lo = jnp.maximum(t_own - 1, 0)
        win = pl.ds(pl.multiple_of(t_lo * tk, tk), 2 * tk)
        ck_win = jnp.concatenate(
            [ck_ref[0, 0, pl.ds(t_lo, 1), :], ck_ref[0, 0, pl.ds(t_lo + 1, 1), :]], axis=1)
        steps.append(update(r, final, carry[r], k_ref[0, win, :], va_ref[win, :],
                            c0 - ck_win, (t_own - t_lo) * tk))
    _lockstep(steps)
    for r in range(n_sub):
        rows = slice(r * sub, (r + 1) * sub)
        acc = acc_ref[rows, :]
        gate = jax.nn.sigmoid(og_ref[0, rows, :].astype(F32))
        o_ref[0, rows, :] = (acc[:, :FOX_HEAD_DIM] / acc[:, FOX_HEAD_DIM:FOX_HEAD_DIM + 1]
                             * gate).astype(o_ref.dtype)


def _fox_attn(qkvo, ck, slack, b_sz, s_len, blk, n_sub):
    h_cols = D_MODEL // FOX_HEAD_DIM
    n_blk = s_len // blk
    n_kt, tk = ck.shape[2:]
    ends = ck[:, :, :, tk - 1].reshape(-1)
    firsts = ck.reshape(b_sz, FOX_HEADS, n_blk, blk)[:, :, :, 0].reshape(-1)
    grid_spec = pltpu.PrefetchScalarGridSpec(
        num_scalar_prefetch=3,
        grid=(b_sz, FOX_HEADS, n_blk),
        in_specs=[
            pl.BlockSpec((1, blk, FOX_HEAD_DIM), lambda b, h, i, *_: (b, i, h)),
            pl.BlockSpec((1, s_len, FOX_HEAD_DIM), lambda b, h, i, *_: (b, 0, h_cols + h)),
            pl.BlockSpec((1, s_len, FOX_HEAD_DIM), lambda b, h, i, *_: (b, 0, 2 * h_cols + h)),
            pl.BlockSpec((1, blk, FOX_HEAD_DIM), lambda b, h, i, *_: (b, i, 3 * h_cols + h)),
            pl.BlockSpec((1, 1, n_kt, tk), lambda b, h, i, *_: (b, h, 0, 0)),
        ],
        out_specs=pl.BlockSpec((1, blk, FOX_HEAD_DIM), lambda b, h, i, *_: (b, i, h)),
        scratch_shapes=[
            pltpu.VMEM((blk, 2 * FOX_HEAD_DIM), F32),
            pltpu.VMEM((s_len, 2 * FOX_HEAD_DIM), BF16),
        ],
    )
    return pl.pallas_call(
        functools.partial(_fox_attn_kernel, blk=blk, n_sub=n_sub, tk=tk),
        grid_spec=grid_spec,
        out_shape=jax.ShapeDtypeStruct((b_sz, s_len, D_MODEL), BF16),
        compiler_params=_params("parallel", "parallel", "arbitrary"),
        name="fox_attn",
    )(ends, firsts, slack, qkvo, qkvo, qkvo, qkvo, ck)


def _pad_rows(w, n):
    return jnp.pad(w, ((0, n - w.shape[0]), (0, 0)))


def kernel(x, norm_mix, norm_ffn, gla_w_in, gla_w_g2, gla_b_g2, gla_o_gain, gla_w_o,
           fox_w_in, fox_b_f, fox_q_gain, fox_k_gain, fox_w_o,
           ffn_w_gate, ffn_w_up, ffn_w_down):
    b_sz, s_len, d = x.shape
    assert d == D_MODEL and s_len % GLA_CHUNK == 0
    m = b_sz * s_len
    tm = min(1024, m)
    tn = 1024
    tm_res = min(512, m)
    tm_ffn = min(1024, m)
    tf = 256
    gla_t = min(512, s_len)
    attn_blk = min(1024, s_len)
    attn_tk = min(512, s_len // 2)
    attn_sub = attn_tk
    cum_t = min(256, s_len)

    x2 = x.reshape(m, d)

    def ffn_layer(xin, layer):
        return _ffn(xin, norm_ffn[layer][None, :], ffn_w_gate, ffn_w_up, ffn_w_down, layer, tm_ffn, tf)

    gla_wt = jnp.swapaxes(gla_w_in, 1, 2)
    qkvr, la = _gla_proj(
        x2, norm_mix[0][None, :],
        gla_wt,
        _pad_rows(gla_wt[0, GLA_MAIN:, :], LANES),
        _pad_rows(gla_w_g2[0], LANES),
        gla_b_g2[0][None, :], tm, tn)
    og = _gla_mix(qkvr.reshape(b_sz, s_len, GLA_MAIN), la.reshape(b_sz, s_len, GLA_KEY_DIM),
                  gla_o_gain[0][None, :], b_sz, s_len, gla_t, 4)
    x2 = _residual_matmul(x2, og.reshape(m, GLA_VAL_DIM), gla_w_o, tm_res)
    x2 = ffn_layer(x2, 0)

    qk_gain = jnp.concatenate([jnp.tile(fox_q_gain[0], FOX_HEADS), jnp.tile(fox_k_gain[0], FOX_HEADS)])
    f_lo, f_hi = 3 * D_MODEL, 3 * D_MODEL + FOX_HEADS
    fox_wt = jnp.swapaxes(fox_w_in, 1, 2)
    qkvo, lf = _fox_proj(
        x2, norm_mix[1][None, :],
        fox_wt,
        _pad_rows(fox_wt[0, f_lo:f_hi, :], LANES),
        jnp.pad(fox_b_f[0], (0, LANES - FOX_HEADS))[None, :],
        qk_gain[None, :], tm, tn, f_lo, FOX_HEADS)
    c = _seq_cumsum(lf.reshape(b_sz, s_len, LANES), cum_t)
    c_hs = c[:, :, :FOX_HEADS].transpose(0, 2, 1)
    ck = c_hs.reshape(b_sz, FOX_HEADS, s_len // attn_tk, attn_tk)
    qk_max = (FOX_HEAD_DIM * FOX_SCALE * LOG2E * BF16_SLOP
              * jnp.max(jnp.abs(fox_q_gain[0])) * jnp.max(jnp.abs(fox_k_gain[0])))
    slack = (2.0 * qk_max + F32_EXP2_FLOOR).reshape(1)
    o = _fox_attn(qkvo.reshape(b_sz, s_len, FOX_MAIN), ck, slack, b_sz, s_len, attn_blk,
                  attn_blk // attn_sub)
    x2 = _residual_matmul(x2, o.reshape(m, D_MODEL), fox_w_o, tm_res)
    x2 = ffn_layer(x2, 1)
    return x2.reshape(b_sz, s_len, d)
```

```python
import functools

import jax
import jax.numpy as jnp
from jax import lax
from jax.experimental import pallas as pl
from jax.experimental.pallas import tpu as pltpu

F32 = jnp.float32
BF16 = jnp.bfloat16

D_MODEL = 2048
RMS_EPS = 1e-6

GLA_HEADS = 4
GLA_KEY_DIM = D_MODEL // 2
GLA_VAL_DIM = D_MODEL
GLA_HEAD_K = GLA_KEY_DIM // GLA_HEADS
GLA_HEAD_V = GLA_VAL_DIM // GLA_HEADS
GLA_GATE_RANK = 16
GLA_GATE_TEMP = 16.0
GLA_CHUNK = 64
GLA_MAIN = 2 * GLA_KEY_DIM + 2 * GLA_VAL_DIM
GLA_COARSE_LEVELS = (64, 32)
GLA_FINE_LEVELS = (16, 8)
GLA_DIAG = 4
GLA_CHUNKS_PER_ITER = 2

FOX_HEAD_DIM = 128
FOX_HEADS = D_MODEL // FOX_HEAD_DIM
FOX_MAIN = 4 * D_MODEL
FOX_SCALE = FOX_HEAD_DIM ** -0.5
LOG2E = 1.4426950408889634
BF16_SLOP = 1.02
F32_EXP2_FLOOR = 160.0

D_FF = ((8 * D_MODEL + 2) // 3 + 255) // 256 * 256

LANES = 128
SUBLANES = 8
NORM_SLAB = 256
VMEM_LIMIT = 58 * 1024 * 1024

_DONE = object()


def _lockstep(chains):
    pending = list(chains)
    while pending:
        pending = [g for g in pending if next(g, _DONE) is not _DONE]


NT_DIMS = (((1,), (1,)), ((), ()))
TN_DIMS = (((0,), (0,)), ((), ()))


def _params(*sem, flags=None):
    return pltpu.CompilerParams(dimension_semantics=sem, vmem_limit_bytes=VMEM_LIMIT, flags=flags)


def _rmsnorm_rows(x, g):
    ms = jnp.mean(x * x, axis=-1, keepdims=True)
    return x * lax.rsqrt(ms + RMS_EPS) * g


def _log_sigmoid(z):
    return jnp.minimum(z, 0.0) - jnp.log(1.0 + jnp.exp(-jnp.abs(z)))


def _norm_to_scratch(x_ref, g_ref, h_ref):
    rows = x_ref.shape[0]
    slab = min(NORM_SLAB, rows)
    g = g_ref[...]
    for r0 in range(0, rows, slab):
        h_ref[r0:r0 + slab, :] = _rmsnorm_rows(x_ref[r0:r0 + slab, :], g).astype(BF16)


def _split3(a):
    hi = a.astype(BF16)
    r1 = a - hi.astype(F32)
    mid = r1.astype(BF16)
    lo = (r1 - mid.astype(F32)).astype(BF16)
    return hi, mid, lo


def _cumsum_rows(tril, a, terms=3):
    parts = _split3(a)[:terms]
    out = jnp.dot(tril, parts[0], preferred_element_type=F32)
    for part in parts[1:]:
        out += jnp.dot(tril, part, preferred_element_type=F32)
    return out


def _gla_norm_kernel(x_ref, g_ref, wg1_ref, wg2_ref, bg_ref, h_ref, la_ref):
    _norm_to_scratch(x_ref, g_ref, h_ref)
    h = h_ref[...]
    g1 = lax.dot_general(h, wg1_ref[...].astype(BF16), NT_DIMS, preferred_element_type=F32)
    z = jnp.dot(g1.astype(BF16), wg2_ref[...].astype(BF16),
                preferred_element_type=F32) + bg_ref[...]
    la_ref[...] = _log_sigmoid(z) * (LOG2E / GLA_GATE_TEMP)


def _gla_norm(x2, g, wg1, wg2, bg, tm):
    m = x2.shape[0]
    return pl.pallas_call(
        _gla_norm_kernel,
        grid=(m // tm,),
        in_specs=[
            pl.BlockSpec((tm, D_MODEL), lambda i: (i, 0)),
            pl.BlockSpec((1, D_MODEL), lambda i: (0, 0)),
            pl.BlockSpec((LANES, D_MODEL), lambda i: (0, 0)),
            pl.BlockSpec((LANES, GLA_KEY_DIM), lambda i: (0, 0)),
            pl.BlockSpec((1, GLA_KEY_DIM), lambda i: (0, 0)),
        ],
        out_specs=[
            pl.BlockSpec((tm, D_MODEL), lambda i: (i, 0)),
            pl.BlockSpec((tm, GLA_KEY_DIM), lambda i: (i, 0)),
        ],
        out_shape=[
            jax.ShapeDtypeStruct((m, D_MODEL), BF16),
            jax.ShapeDtypeStruct((m, GLA_KEY_DIM), F32),
        ],
        compiler_params=_params("parallel"),
        name="gla_norm",
    )(x2, g, wg1, wg2, bg)


def _project(h_ref, w_ref, wb_ref):
    @pl.when(pl.program_id(1) == 0)
    def _():
        for r0 in range(0, wb_ref.shape[0], NORM_SLAB):
            wb_ref[r0:r0 + NORM_SLAB, :] = w_ref[0, r0:r0 + NORM_SLAB, :].astype(BF16)

    return lax.dot_general(h_ref[...], wb_ref[...], NT_DIMS, preferred_element_type=F32)


def _gla_proj_kernel(h_ref, w_ref, o_ref, wb_ref):
    o_ref[...] = _project(h_ref, w_ref, wb_ref).astype(o_ref.dtype)


def _gla_proj(h, wt3, tm, tn):
    m = h.shape[0]
    return pl.pallas_call(
        _gla_proj_kernel,
        grid=(GLA_MAIN // tn, m // tm),
        in_specs=[
            pl.BlockSpec((tm, D_MODEL), lambda j, i: (i, 0)),
            pl.BlockSpec((1, tn, D_MODEL), lambda j, i: (0, j, 0)),
        ],
        out_specs=pl.BlockSpec((tm, tn), lambda j, i: (i, j)),
        out_shape=jax.ShapeDtypeStruct((m, GLA_MAIN), BF16),
        scratch_shapes=[pltpu.VMEM((tn, D_MODEL), BF16)],
        compiler_params=_params("arbitrary", "arbitrary"),
        name="gla_proj",
    )(h, wt3)


def _gla_kernel(q_ref, k_ref, v_ref, r_ref, la_ref, gain_ref, o_ref, st_ref, *, n_chunks, heads):
    c_len = GLA_CHUNK

    @pl.when(pl.program_id(2) == 0)
    def _():
        st_ref[...] = jnp.zeros_like(st_ref)

    row = lax.broadcasted_iota(jnp.int32, (c_len, c_len), 0)
    col = lax.broadcasted_iota(jnp.int32, (c_len, c_len), 1)
    tril = (col <= row).astype(BF16)
    delta = row - col
    band = jnp.where((delta >= 0) & (delta <= row % GLA_DIAG), delta, -1)
    fine_masks = [
        (blk, (row // blk == col // blk) & (row % blk >= blk // 2) & (col % blk < blk // 2))
        for blk in GLA_FINE_LEVELS]
    half_rows = c_len // 2
    hrow = lax.broadcasted_iota(jnp.int32, (half_rows, half_rows), 0)
    hcol = lax.broadcasted_iota(jnp.int32, (half_rows, half_rows), 1)
    gain = gain_ref[...]
    scale = GLA_HEAD_K ** -0.5

    def gather_rows(a, blk, second_half):
        half = blk // 2
        off = half if second_half else 0
        return jnp.concatenate([a[s + off:s + off + half] for s in range(0, c_len, blk)], axis=0)

    def head_chunk(hd, sl):
        ksl = slice(hd * GLA_HEAD_K, (hd + 1) * GLA_HEAD_K)
        vsl = slice(hd * GLA_HEAD_V, (hd + 1) * GLA_HEAD_V)
        la = la_ref[0, sl, ksl]
        b = _cumsum_rows(tril, la, terms=2)
        yield
        b_last = b[c_len - 1:c_len, :]
        qf = q_ref[0, sl, ksl].astype(F32)
        kf = k_ref[0, sl, ksl].astype(F32)
        v = v_ref[0, sl, vsl]

        st = st_ref[hd]
        qi = (qf * jnp.exp2(b)).astype(BF16)
        o = lax.dot_general(qi, st.astype(BF16), NT_DIMS, preferred_element_type=F32)
        kd = (kf * jnp.exp2(b_last - b)).astype(BF16)
        st_ref[hd] = st * jnp.exp2(b_last) + lax.dot_general(
            v, kd, TN_DIMS, preferred_element_type=F32)
        yield

        parts = []
        for blk in GLA_COARSE_LEVELS:
            half = blk // 2
            ref = jnp.concatenate(
                [jnp.broadcast_to(b[s + half - 1:s + half, :], (half, GLA_HEAD_K))
                 for s in range(0, c_len, blk)], axis=0)
            ql = (gather_rows(qf, blk, True) * jnp.exp2(gather_rows(b, blk, True) - ref)).astype(BF16)
            kl = (gather_rows(kf, blk, False) * jnp.exp2(ref - gather_rows(b, blk, False))).astype(BF16)
            a_l = lax.dot_general(ql, kl, NT_DIMS, preferred_element_type=F32)
            yield
            if blk < c_len:
                a_l = jnp.where(hrow // half == hcol // half, a_l, 0.0)
            parts.append((blk, jnp.dot(a_l.astype(BF16), gather_rows(v, blk, False),
                                       preferred_element_type=F32)))
            yield

        attn = jnp.zeros((c_len, c_len), F32)
        for blk, mk in fine_masks:
            half = blk // 2
            ref = jnp.concatenate(
                [jnp.broadcast_to(b[s + half - 1:s + half, :], (blk, GLA_HEAD_K))
                 for s in range(0, c_len, blk)], axis=0)
            ql = (qf * jnp.exp2(b - ref)).astype(BF16)
            kl = (kf * jnp.exp2(ref - b)).astype(BF16)
            a_l = lax.dot_general(ql, kl, NT_DIMS, preferred_element_type=F32)
            yield
            attn = jnp.where(mk, a_l, attn)
        for d in range(GLA_DIAG):
            if d:
                prod = qf * pltpu.roll(kf, d, 0) * jnp.exp2(b - pltpu.roll(b, d, 0))
            else:
                prod = qf * kf
            attn = jnp.where(band == d, jnp.sum(prod, axis=-1, keepdims=True), attn)
        o += jnp.dot(attn.astype(BF16), v, preferred_element_type=F32)
        yield

        groups = [o[g:g + SUBLANES] for g in range(0, c_len, SUBLANES)]
        for blk, part in parts:
            half = blk // 2
            src = 0
            for s in range(0, c_len, blk):
                for g in range((s + half) // SUBLANES, (s + blk) // SUBLANES):
                    groups[g] = groups[g] + part[src:src + SUBLANES]
                    src += SUBLANES
        o = jnp.concatenate(groups, axis=0)

        ms = jnp.mean(o * o, axis=-1, keepdims=True)
        factor = scale * lax.rsqrt(scale * scale * ms + RMS_EPS)
        r = r_ref[0, sl, vsl].astype(F32)
        o_ref[0, sl, vsl] = (o * factor * gain * (r * jax.nn.sigmoid(r))).astype(o_ref.dtype)

    def chunk_group(c, carry):
        chains = []
        for u in range(GLA_CHUNKS_PER_ITER):
            sl = pl.ds(pl.multiple_of((c * GLA_CHUNKS_PER_ITER + u) * c_len, c_len), c_len)
            chains += [head_chunk(hd, sl) for hd in range(heads)]
        _lockstep(chains)
        return carry

    assert n_chunks % GLA_CHUNKS_PER_ITER == 0
    lax.fori_loop(0, n_chunks // GLA_CHUNKS_PER_ITER, chunk_group, 0)


def _gla_mix(qkvr, la, gain, b_sz, s_len, t_len, heads):
    n_chunks = t_len // GLA_CHUNK
    wk = heads * GLA_HEAD_K
    wv = heads * GLA_HEAD_V
    kq = GLA_KEY_DIM // wk
    kv = 2 * GLA_KEY_DIM // wv
    kr = kv + GLA_VAL_DIM // wv
    return pl.pallas_call(
        functools.partial(_gla_kernel, n_chunks=n_chunks, heads=heads),
        grid=(b_sz, GLA_HEADS // heads, s_len // t_len),
        in_specs=[
            pl.BlockSpec((1, t_len, wk), lambda b, h, t: (b, t, h)),
            pl.BlockSpec((1, t_len, wk), lambda b, h, t: (b, t, kq + h)),
            pl.BlockSpec((1, t_len, wv), lambda b, h, t: (b, t, kv + h)),
            pl.BlockSpec((1, t_len, wv), lambda b, h, t: (b, t, kr + h)),
            pl.BlockSpec((1, t_len, wk), lambda b, h, t: (b, t, h)),
            pl.BlockSpec((1, GLA_HEAD_V), lambda b, h, t: (0, 0)),
        ],
        out_specs=pl.BlockSpec((1, t_len, wv), lambda b, h, t: (b, t, h)),
        out_shape=jax.ShapeDtypeStruct((b_sz, s_len, GLA_VAL_DIM), BF16),
        scratch_shapes=[pltpu.VMEM((heads, GLA_HEAD_V, GLA_HEAD_K), F32)],
        compiler_params=_params("parallel", "parallel", "arbitrary"),
        name="gla_mix",
    )(qkvr, qkvr, qkvr, qkvr, la, gain)


def _residual_matmul_kernel(x_ref, a_ref, w_ref, o_ref, wb_ref):
    @pl.when(pl.program_id(0) == 0)
    def _():
        for r0 in range(0, w_ref.shape[1], NORM_SLAB):
            wb_ref[r0:r0 + NORM_SLAB, :] = w_ref[0, r0:r0 + NORM_SLAB, :].astype(BF16)

    o_ref[...] = x_ref[...] + jnp.dot(a_ref[...], wb_ref[...], preferred_element_type=F32)


def _residual_matmul(x2, a, w3, tm):
    m, k_dim = a.shape
    n = w3.shape[2]
    return pl.pallas_call(
        _residual_matmul_kernel,
        grid=(m // tm,),
        in_specs=[
            pl.BlockSpec((tm, n), lambda i: (i, 0)),
            pl.BlockSpec((tm, k_dim), lambda i: (i, 0)),
            pl.BlockSpec((1, k_dim, n), lambda i: (0, 0, 0), pipeline_mode=pl.Buffered(1)),
        ],
        out_specs=pl.BlockSpec((tm, n), lambda i: (i, 0)),
        out_shape=jax.ShapeDtypeStruct((m, n), F32),
        scratch_shapes=[pltpu.VMEM((k_dim, n), BF16)],
        compiler_params=_params("arbitrary"),
        name="residual_matmul",
    )(x2, a, w3)


def _ffn_kernel(x_ref, g_ref, wg_hbm, wu_hbm, wd_hbm, o_ref, h_ref, wg_buf, wu_buf, wd_buf, sem,
                *, layer, tf):
    i = pl.program_id(0)
    n_f = D_FF // tf
    assert n_f % 2 == 0

    def tile_copies(f, slot):
        cols = pl.ds(pl.multiple_of(f * tf, tf), tf)
        return (
            pltpu.make_async_copy(wg_hbm.at[layer, :, cols], wg_buf.at[slot], sem.at[0, slot]),
            pltpu.make_async_copy(wu_hbm.at[layer, :, cols], wu_buf.at[slot], sem.at[1, slot]),
            pltpu.make_async_copy(wd_hbm.at[layer, cols, :], wd_buf.at[slot], sem.at[2, slot]),
        )

    @pl.when(i == 0)
    def _():
        for cp in tile_copies(0, 0):
            cp.start()

    _norm_to_scratch(x_ref, g_ref, h_ref)

    def use_tile(f, slot, acc_ref):
        for cp in tile_copies(f, slot):
            cp.wait()
        nxt = jnp.where(f + 1 == n_f, 0, f + 1)
        for cp in tile_copies(nxt, 1 - slot):
            cp.start()
        h = h_ref[...]
        gate = jnp.dot(h, wg_buf[slot].astype(BF16), preferred_element_type=F32)
        up = jnp.dot(h, wu_buf[slot].astype(BF16), preferred_element_type=F32)
        act = (gate * jax.nn.sigmoid(gate) * up).astype(BF16)
        o_ref[...] = acc_ref[...] + jnp.dot(act, wd_buf[slot].astype(BF16),
                                            preferred_element_type=F32)

    def pair(p, carry):
        use_tile(2 * p, 0, o_ref)
        use_tile(2 * p + 1, 1, o_ref)
        return carry

    use_tile(0, 0, x_ref)
    use_tile(1, 1, o_ref)
    lax.fori_loop(1, n_f // 2, pair, 0)

    @pl.when(i == pl.num_programs(0) - 1)
    def _():
        for cp in tile_copies(0, 0):
            cp.wait()


def _ffn(x2, g, wg, wu, wd, layer, tm, tf):
    m = x2.shape[0]
    return pl.pallas_call(
        functools.partial(_ffn_kernel, layer=layer, tf=tf),
        grid=(m // tm,),
        in_specs=[
            pl.BlockSpec((tm, D_MODEL), lambda i: (i, 0)),
            pl.BlockSpec((1, D_MODEL), lambda i: (0, 0)),
            pl.BlockSpec(memory_space=pl.ANY),
            pl.BlockSpec(memory_space=pl.ANY),
            pl.BlockSpec(memory_space=pl.ANY),
        ],
        out_specs=pl.BlockSpec((tm, D_MODEL), lambda i: (i, 0)),
        out_shape=jax.ShapeDtypeStruct((m, D_MODEL), F32),
        scratch_shapes=[
            pltpu.VMEM((tm, D_MODEL), BF16),
            pltpu.VMEM((2, D_MODEL, tf), F32),
            pltpu.VMEM((2, D_MODEL, tf), F32),
            pltpu.VMEM((2, tf, D_MODEL), F32),
            pltpu.SemaphoreType.DMA((3, 2)),
        ],
        compiler_params=_params("arbitrary"),
        name="ffn",
    )(x2, g, wg, wu, wd)


def _fox_norm_kernel(x_ref, g_ref, wf_ref, bf_ref, h_ref, lf_ref):
    _norm_to_scratch(x_ref, g_ref, h_ref)
    z = lax.dot_general(h_ref[...], wf_ref[...].astype(BF16), NT_DIMS,
                        preferred_element_type=F32) + bf_ref[...]
    lf_ref[...] = _log_sigmoid(z)


def _fox_norm(x2, g, wf, bf, tm):
    m = x2.shape[0]
    return pl.pallas_call(
        _fox_norm_kernel,
        grid=(m // tm,),
        in_specs=[
            pl.BlockSpec((tm, D_MODEL), lambda i: (i, 0)),
            pl.BlockSpec((1, D_MODEL), lambda i: (0, 0)),
            pl.BlockSpec((LANES, D_MODEL), lambda i: (0, 0)),
            pl.BlockSpec((1, LANES), lambda i: (0, 0)),
        ],
        out_specs=[
            pl.BlockSpec((tm, D_MODEL), lambda i: (i, 0)),
            pl.BlockSpec((tm, LANES), lambda i: (i, 0)),
        ],
        out_shape=[
            jax.ShapeDtypeStruct((m, D_MODEL), BF16),
            jax.ShapeDtypeStruct((m, LANES), F32),
        ],
        compiler_params=_params("parallel"),
        name="fox_norm",
    )(x2, g, wf, bf)


def _fox_proj_kernel(h_ref, w_ref, qkg_ref, o_ref, wb_ref, *, q_tiles, qk_tiles):
    j = pl.program_id(0)
    acc = _project(h_ref, w_ref, wb_ref)

    @pl.when(j < qk_tiles)
    def _():
        mult = jnp.where(j < q_tiles, FOX_SCALE * LOG2E, 1.0).astype(F32)
        for c in range(acc.shape[1] // FOX_HEAD_DIM):
            cs = slice(c * FOX_HEAD_DIM, (c + 1) * FOX_HEAD_DIM)
            xs = acc[:, cs]
            ms = jnp.mean(xs * xs, axis=-1, keepdims=True)
            o_ref[:, cs] = (xs * (lax.rsqrt(ms + RMS_EPS) * mult) * qkg_ref[:, cs]).astype(o_ref.dtype)

    @pl.when(j >= qk_tiles)
    def _():
        o_ref[...] = acc.astype(o_ref.dtype)


def _fox_proj(h, wt3, qk_gain, tm, tn, gap_at, gap):
    m = h.shape[0]
    q_tiles = D_MODEL // tn
    qk_tiles = 2 * q_tiles
    assert gap_at % tn == 0 and gap % SUBLANES == 0
    first_after = gap_at // tn
    return pl.pallas_call(
        functools.partial(_fox_proj_kernel, q_tiles=q_tiles, qk_tiles=qk_tiles),
        grid=(FOX_MAIN // tn, m // tm),
        in_specs=[
            pl.BlockSpec((tm, D_MODEL), lambda j, i: (i, 0)),
            pl.BlockSpec(
                (pl.Element(1), pl.Element(tn), pl.Element(D_MODEL)),
                lambda j, i: (0, pl.multiple_of(j * tn + jnp.where(j >= first_after, gap, 0),
                                                SUBLANES), 0)),
            pl.BlockSpec((1, tn), lambda j, i: (0, jnp.minimum(j, qk_tiles - 1))),
        ],
        out_specs=pl.BlockSpec((tm, tn), lambda j, i: (i, j)),
        out_shape=jax.ShapeDtypeStruct((m, FOX_MAIN), BF16),
        scratch_shapes=[pltpu.VMEM((tn, D_MODEL), BF16)],
        compiler_params=_params("arbitrary", "arbitrary"),
        name="fox_proj",
    )(h, wt3, qk_gain)


def _cumsum_kernel(lf_ref, c_ref, *, t_len):
    row = lax.broadcasted_iota(jnp.int32, (t_len, t_len), 0)
    col = lax.broadcasted_iota(jnp.int32, (t_len, t_len), 1)
    tril = (col <= row).astype(BF16)

    def block(t, carry):
        sl = pl.ds(pl.multiple_of(t * t_len, t_len), t_len)
        c = _cumsum_rows(tril, lf_ref[0, sl, :]) + carry
        c_ref[0, sl, :] = c * LOG2E
        return c[t_len - 1:t_len, :]

    lax.fori_loop(0, lf_ref.shape[1] // t_len, block, jnp.zeros((1, LANES), F32))


def _seq_cumsum(lf, t_len):
    b_sz, s_len, _ = lf.shape
    return pl.pallas_call(
        functools.partial(_cumsum_kernel, t_len=t_len),
        grid=(b_sz,),
        in_specs=[pl.BlockSpec((1, s_len, LANES), lambda b: (b, 0, 0))],
        out_specs=pl.BlockSpec((1, s_len, LANES), lambda b: (b, 0, 0)),
        out_shape=jax.ShapeDtypeStruct(lf.shape, F32),
        compiler_params=_params("parallel"),
        name="fox_cumsum",
    )(lf)


def _fox_attn_kernel(ends_ref, firsts_ref, slack_ref, q_ref, k_ref, v_ref, og_ref, ck_ref, o_ref,
                     acc_ref, va_ref, *, blk, n_sub, tk):
    i = pl.program_id(2)
    sub = blk // n_sub
    assert sub == tk
    n_kt = ck_ref.shape[2]
    bh = pl.program_id(0) * pl.num_programs(1) + pl.program_id(1)
    n_loop = jnp.maximum(i * n_sub - 1, 0)
    limit = firsts_ref[bh * pl.num_programs(2) + i] + slack_ref[0]

    def first_needed(j, lo):
        return jnp.minimum(lo, jnp.where(ends_ref[bh * n_kt + j] > limit, n_loop, j))

    j_start = lax.fori_loop(0, n_loop, first_needed, n_loop)

    c0 = ck_ref[0, 0, pl.ds(i * n_sub, 1), :][:, 0:1]
    acc_ref[...] = jnp.zeros_like(acc_ref)

    @pl.when(i == 0)
    def _():
        for r0 in range(0, va_ref.shape[0], blk):
            va_ref[r0:r0 + blk, :FOX_HEAD_DIM] = v_ref[0, r0:r0 + blk, :]
            va_ref[r0:r0 + blk, FOX_HEAD_DIM:] = jnp.ones((blk, FOX_HEAD_DIM), BF16)

    def update(r, out, m_old, ks, va, bias, mask_offset):
        rows = slice(r * sub, (r + 1) * sub)
        s = lax.dot_general(q_ref[0, rows, :], ks, NT_DIMS, preferred_element_type=F32) + bias
        yield
        if mask_offset is not None:
            row = lax.broadcasted_iota(jnp.int32, s.shape, 0)
            col = lax.broadcasted_iota(jnp.int32, s.shape, 1)
            s = jnp.where(col <= row + mask_offset, s, -jnp.inf)
        m_new = jnp.maximum(m_old, jnp.max(s, axis=-1, keepdims=True))
        p = jnp.exp2(s - m_new).astype(BF16)
        alpha = jnp.exp2(m_old - m_new)
        acc_ref[rows, :] = alpha * acc_ref[rows, :] + jnp.dot(p, va, preferred_element_type=F32)
        out[r] = m_new

    def body(j, carry):
        sl = pl.ds(pl.multiple_of(j * tk, tk), tk)
        ks = k_ref[0, sl, :]
        va = va_ref[sl, :]
        bias = c0 - ck_ref[0, 0, pl.ds(j, 1), :]
        out = [None] * n_sub
        _lockstep([update(r, out, carry[r], ks, va, bias, None) for r in range(n_sub)])
        return tuple(out)

    init = tuple(jnp.full((sub, 1), -jnp.inf, F32) for _ in range(n_sub))
    carry = lax.fori_loop(j_start, n_loop, body, init)

    carry = list(carry)
    for r in range(1, n_sub):
        t_prev = i * n_sub + r - 1
        limit_r = ends_ref[bh * n_kt + t_prev] + slack_ref[0]
        for u in range(r):
            g = i * n_sub - 1 + u

            def visit(g=g, r=r):
                out = [None] * n_sub
                sl = pl.ds(pl.multiple_of(g * tk, tk), tk)
                bias = c0 - ck_ref[0, 0, pl.ds(g, 1), :]
                _lockstep([update(r, out, carry[r], k_ref[0, sl, :], va_ref[sl, :], bias, None)])
                return out[r]

            needed = (i > 0) & (ends_ref[bh * n_kt + jnp.maximum(g, 0)] <= limit_r)
            carry[r] = lax.cond(needed, visit, lambda r=r: carry[r])

    final = [None] * n_sub
    steps = []
    for r in range(n_sub):
        t_own = i * n_sub + r
        t_lo = jnp.maximum(t_own - 1, 0)
        win = pl.ds(pl.multiple_of(t_lo * tk, tk), 2 * tk)
        ck_win = jnp.concatenate(
            [ck_ref[0, 0, pl.ds(t_lo, 1), :], ck_ref[0, 0, pl.ds(t_lo + 1, 1), :]], axis=1)
        steps.append(update(r, final, carry[r], k_ref[0, win, :], va_ref[win, :],
                            c0 - ck_win, (t_own - t_lo) * tk))
    _lockstep(steps)
    for r in range(n_sub):
        rows = slice(r * sub, (r + 1) * sub)
        acc = acc_ref[rows, :]
        gate = jax.nn.sigmoid(og_ref[0, rows, :].astype(F32))
        o_ref[0, rows, :] = (acc[:, :FOX_HEAD_DIM] / acc[:, FOX_HEAD_DIM:FOX_HEAD_DIM + 1]
                             * gate).astype(o_ref.dtype)


def _fox_attn(qkvo, ck, slack, b_sz, s_len, blk, n_sub):
    h_cols = D_MODEL // FOX_HEAD_DIM
    n_blk = s_len // blk
    n_kt, tk = ck.shape[2:]
    ends = ck[:, :, :, tk - 1].reshape(-1)
    firsts = ck.reshape(b_sz, FOX_HEADS, n_blk, blk)[:, :, :, 0].reshape(-1)
    grid_spec = pltpu.PrefetchScalarGridSpec(
        num_scalar_prefetch=3,
        grid=(b_sz, FOX_HEADS, n_blk),
        in_specs=[
            pl.BlockSpec((1, blk, FOX_HEAD_DIM), lambda b, h, i, *_: (b, i, h)),
            pl.BlockSpec((1, s_len, FOX_HEAD_DIM), lambda b, h, i, *_: (b, 0, h_cols + h)),
            pl.BlockSpec((1, s_len, FOX_HEAD_DIM), lambda b, h, i, *_: (b, 0, 2 * h_cols + h)),
            pl.BlockSpec((1, blk, FOX_HEAD_DIM), lambda b, h, i, *_: (b, i, 3 * h_cols + h)),
            pl.BlockSpec((1, 1, n_kt, tk), lambda b, h, i, *_: (b, h, 0, 0)),
        ],
        out_specs=pl.BlockSpec((1, blk, FOX_HEAD_DIM), lambda b, h, i, *_: (b, i, h)),
        scratch_shapes=[
            pltpu.VMEM((blk, 2 * FOX_HEAD_DIM), F32),
            pltpu.VMEM((s_len, 2 * FOX_HEAD_DIM), BF16),
        ],
    )
    return pl.pallas_call(
        functools.partial(_fox_attn_kernel, blk=blk, n_sub=n_sub, tk=tk),
        grid_spec=grid_spec,
        out_shape=jax.ShapeDtypeStruct((b_sz, s_len, D_MODEL), BF16),
        compiler_params=_params("parallel", "parallel", "arbitrary"),
        name="fox_attn",
    )(ends, firsts, slack, qkvo, qkvo, qkvo, qkvo, ck)


def _pad_rows(w, n):
    return jnp.pad(w, ((0, n - w.shape[0]), (0, 0)))


def kernel(x, norm_mix, norm_ffn, gla_w_in, gla_w_g2, gla_b_g2, gla_o_gain, gla_w_o,
           fox_w_in, fox_b_f, fox_q_gain, fox_k_gain, fox_w_o,
           ffn_w_gate, ffn_w_up, ffn_w_down):
    b_sz, s_len, d = x.shape
    assert d == D_MODEL and s_len % GLA_CHUNK == 0
    m = b_sz * s_len
    tm = min(1024, m)
    tn = 1024
    tm_res = min(512, m)
    tm_ffn = min(1024, m)
    tf = 256
    gla_t = min(512, s_len)
    attn_blk = min(1024, s_len)
    attn_tk = min(512, s_len // 2)
    attn_sub = attn_tk
    cum_t = min(256, s_len)

    x2 = x.reshape(m, d)

    def ffn_layer(xin, layer):
        return _ffn(xin, norm_ffn[layer][None, :], ffn_w_gate, ffn_w_up, ffn_w_down, layer, tm_ffn, tf)

    gla_wt = jnp.swapaxes(gla_w_in, 1, 2)
    h, la = _gla_norm(x2, norm_mix[0][None, :], _pad_rows(gla_wt[0, GLA_MAIN:, :], LANES),
                      _pad_rows(gla_w_g2[0], LANES), gla_b_g2[0][None, :], tm)
    qkvr = _gla_proj(h, gla_wt, tm, tn)
    og = _gla_mix(qkvr.reshape(b_sz, s_len, GLA_MAIN), la.reshape(b_sz, s_len, GLA_KEY_DIM),
                  gla_o_gain[0][None, :], b_sz, s_len, gla_t, 4)
    x2 = _residual_matmul(x2, og.reshape(m, GLA_VAL_DIM), gla_w_o, tm_res)
    x2 = ffn_layer(x2, 0)

    qk_gain = jnp.concatenate([jnp.tile(fox_q_gain[0], FOX_HEADS), jnp.tile(fox_k_gain[0], FOX_HEADS)])
    f_lo, f_hi = 3 * D_MODEL, 3 * D_MODEL + FOX_HEADS
    fox_wt = jnp.swapaxes(fox_w_in, 1, 2)
    h, lf = _fox_norm(x2, norm_mix[1][None, :], _pad_rows(fox_wt[0, f_lo:f_hi, :], LANES),
                      jnp.pad(fox_b_f[0], (0, LANES - FOX_HEADS))[None, :], tm)
    qkvo = _fox_proj(h, fox_wt, qk_gain[None, :], tm, tn, f_lo, FOX_HEADS)
    c = _seq_cumsum(lf.reshape(b_sz, s_len, LANES), cum_t)
    c_hs = c[:, :, :FOX_HEADS].transpose(0, 2, 1)
    ck = c_hs.reshape(b_sz, FOX_HEADS, s_len // attn_tk, attn_tk)
    qk_max = (FOX_HEAD_DIM * FOX_SCALE * LOG2E * BF16_SLOP
              * jnp.max(jnp.abs(fox_q_gain[0])) * jnp.max(jnp.abs(fox_k_gain[0])))
    slack = (2.0 * qk_max + F32_EXP2_FLOOR).reshape(1)
    o = _fox_attn(qkvo.reshape(b_sz, s_len, FOX_MAIN), ck, slack, b_sz, s_len, attn_blk,
                  attn_blk // attn_sub)
    x2 = _residual_matmul(x2, o.reshape(m, D_MODEL), fox_w_o, tm_res)
    x2 = ffn_layer(x2, 1)
    return x2.reshape(b_sz, s_len, d)
```

```python
import functools

import jax
import jax.numpy as jnp
from jax import lax
from jax.experimental import pallas as pl
from jax.experimental.pallas import tpu as pltpu

F32 = jnp.float32
BF16 = jnp.bfloat16

D_MODEL = 2048
RMS_EPS = 1e-6

GLA_HEADS = 4
GLA_KEY_DIM = D_MODEL // 2
GLA_VAL_DIM = D_MODEL
GLA_HEAD_K = GLA_KEY_DIM // GLA_HEADS
GLA_HEAD_V = GLA_VAL_DIM // GLA_HEADS
GLA_GATE_RANK = 16
GLA_GATE_TEMP = 16.0
GLA_CHUNK = 64
GLA_MAIN = 2 * GLA_KEY_DIM + 2 * GLA_VAL_DIM
GLA_COARSE_LEVELS = (64, 32)
GLA_FINE_LEVELS = (16, 8)
GLA_DIAG = 4
GLA_CHUNKS_PER_ITER = 2

FOX_HEAD_DIM = 128
FOX_HEADS = D_MODEL // FOX_HEAD_DIM
FOX_MAIN = 4 * D_MODEL
FOX_SCALE = FOX_HEAD_DIM ** -0.5
LOG2E = 1.4426950408889634
BF16_SLOP = 1.02
F32_EXP2_FLOOR = 160.0

D_FF = ((8 * D_MODEL + 2) // 3 + 255) // 256 * 256

LANES = 128
SUBLANES = 8
NORM_SLAB = 256
VMEM_LIMIT = 58 * 1024 * 1024

_DONE = object()


def _lockstep(chains):
    pending = list(chains)
    while pending:
        pending = [g for g in pending if next(g, _DONE) is not _DONE]


NT_DIMS = (((1,), (1,)), ((), ()))
TN_DIMS = (((0,), (0,)), ((), ()))


def _params(*sem):
    return pltpu.CompilerParams(dimension_semantics=sem, vmem_limit_bytes=VMEM_LIMIT)


def _rmsnorm_rows(x, g):
    ms = jnp.mean(x * x, axis=-1, keepdims=True)
    return x * lax.rsqrt(ms + RMS_EPS) * g


def _log_sigmoid(z):
    return jnp.minimum(z, 0.0) - jnp.log(1.0 + jnp.exp(-jnp.abs(z)))


def _norm_to_scratch(x_ref, g_ref, h_ref):
    rows = x_ref.shape[0]
    slab = min(NORM_SLAB, rows)
    g = g_ref[...]
    for r0 in range(0, rows, slab):
        h_ref[r0:r0 + slab, :] = _rmsnorm_rows(x_ref[r0:r0 + slab, :], g).astype(BF16)


def _split3(a):
    hi = a.astype(BF16)
    r1 = a - hi.astype(F32)
    mid = r1.astype(BF16)
    lo = (r1 - mid.astype(F32)).astype(BF16)
    return hi, mid, lo


def _cumsum_rows(tril, a, terms=3):
    parts = _split3(a)[:terms]
    out = jnp.dot(tril, parts[0], preferred_element_type=F32)
    for part in parts[1:]:
        out += jnp.dot(tril, part, preferred_element_type=F32)
    return out


def _gla_proj_kernel(x_ref, g_ref, w_ref, wg1_ref, wg2_ref, bg_ref, o_ref, la_ref, h_ref):
    @pl.when(pl.program_id(1) == 0)
    def _():
        _norm_to_scratch(x_ref, g_ref, h_ref)
        h = h_ref[...]
        g1 = lax.dot_general(h, wg1_ref[...].astype(BF16), NT_DIMS, preferred_element_type=F32)
        z = jnp.dot(g1.astype(BF16), wg2_ref[...].astype(BF16),
                    preferred_element_type=F32) + bg_ref[...]
        la_ref[...] = _log_sigmoid(z) * (LOG2E / GLA_GATE_TEMP)

    o_ref[...] = lax.dot_general(h_ref[...], w_ref[0].astype(BF16), NT_DIMS,
                                 preferred_element_type=F32).astype(o_ref.dtype)


def _gla_proj(x2, g, wt3, wg1, wg2, bg, tm, tn):
    m = x2.shape[0]
    return pl.pallas_call(
        _gla_proj_kernel,
        grid=(m // tm, GLA_MAIN // tn),
        in_specs=[
            pl.BlockSpec((tm, D_MODEL), lambda i, j: (i, 0)),
            pl.BlockSpec((1, D_MODEL), lambda i, j: (0, 0)),
            pl.BlockSpec((1, tn, D_MODEL), lambda i, j: (0, j, 0)),
            pl.BlockSpec((LANES, D_MODEL), lambda i, j: (0, 0)),
            pl.BlockSpec((LANES, GLA_KEY_DIM), lambda i, j: (0, 0)),
            pl.BlockSpec((1, GLA_KEY_DIM), lambda i, j: (0, 0)),
        ],
        out_specs=[
            pl.BlockSpec((tm, tn), lambda i, j: (i, j)),
            pl.BlockSpec((tm, GLA_KEY_DIM), lambda i, j: (i, 0)),
        ],
        out_shape=[
            jax.ShapeDtypeStruct((m, GLA_MAIN), BF16),
            jax.ShapeDtypeStruct((m, GLA_KEY_DIM), F32),
        ],
        scratch_shapes=[pltpu.VMEM((tm, D_MODEL), BF16)],
        compiler_params=_params("parallel", "arbitrary"),
        name="gla_proj",
    )(x2, g, wt3, wg1, wg2, bg)


def _gla_kernel(q_ref, k_ref, v_ref, r_ref, la_ref, gain_ref, o_ref, st_ref, *, n_chunks, heads):
    c_len = GLA_CHUNK

    @pl.when(pl.program_id(2) == 0)
    def _():
        st_ref[...] = jnp.zeros_like(st_ref)

    row = lax.broadcasted_iota(jnp.int32, (c_len, c_len), 0)
    col = lax.broadcasted_iota(jnp.int32, (c_len, c_len), 1)
    tril = (col <= row).astype(BF16)
    delta = row - col
    band = jnp.where((delta >= 0) & (delta <= row % GLA_DIAG), delta, -1)
    fine_masks = [
        (blk, (row // blk == col // blk) & (row % blk >= blk // 2) & (col % blk < blk // 2))
        for blk in GLA_FINE_LEVELS]
    half_rows = c_len // 2
    hrow = lax.broadcasted_iota(jnp.int32, (half_rows, half_rows), 0)
    hcol = lax.broadcasted_iota(jnp.int32, (half_rows, half_rows), 1)
    gain = gain_ref[...]
    scale = GLA_HEAD_K ** -0.5

    def gather_rows(a, blk, second_half):
        half = blk // 2
        off = half if second_half else 0
        return jnp.concatenate([a[s + off:s + off + half] for s in range(0, c_len, blk)], axis=0)

    def head_chunk(hd, sl):
        ksl = slice(hd * GLA_HEAD_K, (hd + 1) * GLA_HEAD_K)
        vsl = slice(hd * GLA_HEAD_V, (hd + 1) * GLA_HEAD_V)
        la = la_ref[0, sl, ksl]
        b = _cumsum_rows(tril, la, terms=2)
        yield
        b_last = b[c_len - 1:c_len, :]
        qf = q_ref[0, sl, ksl].astype(F32)
        kf = k_ref[0, sl, ksl].astype(F32)
        v = v_ref[0, sl, vsl]

        st = st_ref[hd]
        qi = (qf * jnp.exp2(b)).astype(BF16)
        o = lax.dot_general(qi, st.astype(BF16), NT_DIMS, preferred_element_type=F32)
        kd = (kf * jnp.exp2(b_last - b)).astype(BF16)
        st_ref[hd] = st * jnp.exp2(b_last) + lax.dot_general(
            v, kd, TN_DIMS, preferred_element_type=F32)
        yield

        parts = []
        for blk in GLA_COARSE_LEVELS:
            half = blk // 2
            ref = jnp.concatenate(
                [jnp.broadcast_to(b[s + half - 1:s + half, :], (half, GLA_HEAD_K))
                 for s in range(0, c_len, blk)], axis=0)
            ql = (gather_rows(qf, blk, True) * jnp.exp2(gather_rows(b, blk, True) - ref)).astype(BF16)
            kl = (gather_rows(kf, blk, False) * jnp.exp2(ref - gather_rows(b, blk, False))).astype(BF16)
            a_l = lax.dot_general(ql, kl, NT_DIMS, preferred_element_type=F32)
            yield
            if blk < c_len:
                a_l = jnp.where(hrow // half == hcol // half, a_l, 0.0)
            parts.append((blk, jnp.dot(a_l.astype(BF16), gather_rows(v, blk, False),
                                       preferred_element_type=F32)))
            yield

        attn = jnp.zeros((c_len, c_len), F32)
        for blk, mk in fine_masks:
            half = blk // 2
            ref = jnp.concatenate(
                [jnp.broadcast_to(b[s + half - 1:s + half, :], (blk, GLA_HEAD_K))
                 for s in range(0, c_len, blk)], axis=0)
            ql = (qf * jnp.exp2(b - ref)).astype(BF16)
            kl = (kf * jnp.exp2(ref - b)).astype(BF16)
            a_l = lax.dot_general(ql, kl, NT_DIMS, preferred_element_type=F32)
            yield
            attn = jnp.where(mk, a_l, attn)
        for d in range(GLA_DIAG):
            if d:
                prod = qf * pltpu.roll(kf, d, 0) * jnp.exp2(b - pltpu.roll(b, d, 0))
            else:
                prod = qf * kf
            attn = jnp.where(band == d, jnp.sum(prod, axis=-1, keepdims=True), attn)
        o += jnp.dot(attn.astype(BF16), v, preferred_element_type=F32)
        yield

        groups = [o[g:g + SUBLANES] for g in range(0, c_len, SUBLANES)]
        for blk, part in parts:
            half = blk // 2
            src = 0
            for s in range(0, c_len, blk):
                for g in range((s + half) // SUBLANES, (s + blk) // SUBLANES):
                    groups[g] = groups[g] + part[src:src + SUBLANES]
                    src += SUBLANES
        o = jnp.concatenate(groups, axis=0)

        ms = jnp.mean(o * o, axis=-1, keepdims=True)
        factor = scale * lax.rsqrt(scale * scale * ms + RMS_EPS)
        r = r_ref[0, sl, vsl].astype(F32)
        o_ref[0, sl, vsl] = (o * factor * gain * (r * jax.nn.sigmoid(r))).astype(o_ref.dtype)

    def chunk_group(c, carry):
        chains = []
        for u in range(GLA_CHUNKS_PER_ITER):
            sl = pl.ds(pl.multiple_of((c * GLA_CHUNKS_PER_ITER + u) * c_len, c_len), c_len)
            chains += [head_chunk(hd, sl) for hd in range(heads)]
        _lockstep(chains)
        return carry

    assert n_chunks % GLA_CHUNKS_PER_ITER == 0
    lax.fori_loop(0, n_chunks // GLA_CHUNKS_PER_ITER, chunk_group, 0)


def _gla_mix(qkvr, la, gain, b_sz, s_len, t_len, heads):
    n_chunks = t_len // GLA_CHUNK
    wk = heads * GLA_HEAD_K
    wv = heads * GLA_HEAD_V
    kq = GLA_KEY_DIM // wk
    kv = 2 * GLA_KEY_DIM // wv
    kr = kv + GLA_VAL_DIM // wv
    return pl.pallas_call(
        functools.partial(_gla_kernel, n_chunks=n_chunks, heads=heads),
        grid=(b_sz, GLA_HEADS // heads, s_len // t_len),
        in_specs=[
            pl.BlockSpec((1, t_len, wk), lambda b, h, t: (b, t, h)),
            pl.BlockSpec((1, t_len, wk), lambda b, h, t: (b, t, kq + h)),
            pl.BlockSpec((1, t_len, wv), lambda b, h, t: (b, t, kv + h)),
            pl.BlockSpec((1, t_len, wv), lambda b, h, t: (b, t, kr + h)),
            pl.BlockSpec((1, t_len, wk), lambda b, h, t: (b, t, h)),
            pl.BlockSpec((1, GLA_HEAD_V), lambda b, h, t: (0, 0)),
        ],
        out_specs=pl.BlockSpec((1, t_len, wv), lambda b, h, t: (b, t, h)),
        out_shape=jax.ShapeDtypeStruct((b_sz, s_len, GLA_VAL_DIM), BF16),
        scratch_shapes=[pltpu.VMEM((heads, GLA_HEAD_V, GLA_HEAD_K), F32)],
        compiler_params=_params("parallel", "parallel", "arbitrary"),
        name="gla_mix",
    )(qkvr, qkvr, qkvr, qkvr, la, gain)


def _residual_matmul_kernel(x_ref, a_ref, w_ref, o_ref, wb_ref):
    @pl.when(pl.program_id(0) == 0)
    def _():
        for r0 in range(0, w_ref.shape[1], NORM_SLAB):
            wb_ref[r0:r0 + NORM_SLAB, :] = w_ref[0, r0:r0 + NORM_SLAB, :].astype(BF16)

    o_ref[...] = x_ref[...] + jnp.dot(a_ref[...], wb_ref[...], preferred_element_type=F32)


def _residual_matmul(x2, a, w3, tm):
    m, k_dim = a.shape
    n = w3.shape[2]
    return pl.pallas_call(
        _residual_matmul_kernel,
        grid=(m // tm,),
        in_specs=[
            pl.BlockSpec((tm, n), lambda i: (i, 0)),
            pl.BlockSpec((tm, k_dim), lambda i: (i, 0)),
            pl.BlockSpec((1, k_dim, n), lambda i: (0, 0, 0), pipeline_mode=pl.Buffered(1)),
        ],
        out_specs=pl.BlockSpec((tm, n), lambda i: (i, 0)),
        out_shape=jax.ShapeDtypeStruct((m, n), F32),
        scratch_shapes=[pltpu.VMEM((k_dim, n), BF16)],
        compiler_params=_params("arbitrary"),
        name="residual_matmul",
    )(x2, a, w3)


def _ffn_kernel(x_ref, g_ref, wg_hbm, wu_hbm, wd_hbm, o_ref, h_ref, wg_buf, wu_buf, wd_buf, sem,
                *, layer, tf):
    i = pl.program_id(0)
    n_f = D_FF // tf
    assert n_f % 2 == 0

    def tile_copies(f, slot):
        cols = pl.ds(pl.multiple_of(f * tf, tf), tf)
        return (
            pltpu.make_async_copy(wg_hbm.at[layer, :, cols], wg_buf.at[slot], sem.at[0, slot]),
            pltpu.make_async_copy(wu_hbm.at[layer, :, cols], wu_buf.at[slot], sem.at[1, slot]),
            pltpu.make_async_copy(wd_hbm.at[layer, cols, :], wd_buf.at[slot], sem.at[2, slot]),
        )

    @pl.when(i == 0)
    def _():
        for cp in tile_copies(0, 0):
            cp.start()

    _norm_to_scratch(x_ref, g_ref, h_ref)

    def use_tile(f, slot, acc_ref):
        for cp in tile_copies(f, slot):
            cp.wait()
        nxt = jnp.where(f + 1 == n_f, 0, f + 1)
        for cp in tile_copies(nxt, 1 - slot):
            cp.start()
        h = h_ref[...]
        gate = jnp.dot(h, wg_buf[slot].astype(BF16), preferred_element_type=F32)
        up = jnp.dot(h, wu_buf[slot].astype(BF16), preferred_element_type=F32)
        act = (gate * jax.nn.sigmoid(gate) * up).astype(BF16)
        o_ref[...] = acc_ref[...] + jnp.dot(act, wd_buf[slot].astype(BF16),
                                            preferred_element_type=F32)

    def pair(p, carry):
        use_tile(2 * p, 0, o_ref)
        use_tile(2 * p + 1, 1, o_ref)
        return carry

    use_tile(0, 0, x_ref)
    use_tile(1, 1, o_ref)
    lax.fori_loop(1, n_f // 2, pair, 0)

    @pl.when(i == pl.num_programs(0) - 1)
    def _():
        for cp in tile_copies(0, 0):
            cp.wait()


def _ffn(x2, g, wg, wu, wd, layer, tm, tf):
    m = x2.shape[0]
    return pl.pallas_call(
        functools.partial(_ffn_kernel, layer=layer, tf=tf),
        grid=(m // tm,),
        in_specs=[
            pl.BlockSpec((tm, D_MODEL), lambda i: (i, 0)),
            pl.BlockSpec((1, D_MODEL), lambda i: (0, 0)),
            pl.BlockSpec(memory_space=pl.ANY),
            pl.BlockSpec(memory_space=pl.ANY),
            pl.BlockSpec(memory_space=pl.ANY),
        ],
        out_specs=pl.BlockSpec((tm, D_MODEL), lambda i: (i, 0)),
        out_shape=jax.ShapeDtypeStruct((m, D_MODEL), F32),
        scratch_shapes=[
            pltpu.VMEM((tm, D_MODEL), BF16),
            pltpu.VMEM((2, D_MODEL, tf), F32),
            pltpu.VMEM((2, D_MODEL, tf), F32),
            pltpu.VMEM((2, tf, D_MODEL), F32),
            pltpu.SemaphoreType.DMA((3, 2)),
        ],
        compiler_params=_params("arbitrary"),
        name="ffn",
    )(x2, g, wg, wu, wd)


def _fox_proj_kernel(x_ref, g_ref, w_ref, wf_ref, bf_ref, qkg_ref, o_ref, lf_ref, h_ref,
                     *, q_tiles, qk_tiles):
    j = pl.program_id(1)

    @pl.when(j == 0)
    def _():
        _norm_to_scratch(x_ref, g_ref, h_ref)
        h = h_ref[...]
        z = lax.dot_general(h, wf_ref[...].astype(BF16), NT_DIMS,
                            preferred_element_type=F32) + bf_ref[...]
        lf_ref[...] = _log_sigmoid(z)

    def project():
        return lax.dot_general(h_ref[...], w_ref[0].astype(BF16), NT_DIMS,
                               preferred_element_type=F32)

    @pl.when(j < qk_tiles)
    def _():
        acc = project()
        mult = jnp.where(j < q_tiles, FOX_SCALE * LOG2E, 1.0).astype(F32)
        for c in range(acc.shape[1] // FOX_HEAD_DIM):
            cs = slice(c * FOX_HEAD_DIM, (c + 1) * FOX_HEAD_DIM)
            xs = acc[:, cs]
            ms = jnp.mean(xs * xs, axis=-1, keepdims=True)
            o_ref[:, cs] = (xs * (lax.rsqrt(ms + RMS_EPS) * mult) * qkg_ref[:, cs]).astype(o_ref.dtype)

    @pl.when(j >= qk_tiles)
    def _():
        o_ref[...] = project().astype(o_ref.dtype)


def _fox_proj(x2, g, wt3, wf, bf, qk_gain, tm, tn, gap_at, gap):
    m = x2.shape[0]
    q_tiles = D_MODEL // tn
    qk_tiles = 2 * q_tiles
    assert gap_at % tn == 0 and gap % SUBLANES == 0
    first_after = gap_at // tn
    return pl.pallas_call(
        functools.partial(_fox_proj_kernel, q_tiles=q_tiles, qk_tiles=qk_tiles),
        grid=(m // tm, FOX_MAIN // tn),
        in_specs=[
            pl.BlockSpec((tm, D_MODEL), lambda i, j: (i, 0)),
            pl.BlockSpec((1, D_MODEL), lambda i, j: (0, 0)),
            pl.BlockSpec(
                (pl.Element(1), pl.Element(tn), pl.Element(D_MODEL)),
                lambda i, j: (0, pl.multiple_of(j * tn + jnp.where(j >= first_after, gap, 0),
                                                SUBLANES), 0)),
            pl.BlockSpec((LANES, D_MODEL), lambda i, j: (0, 0)),
            pl.BlockSpec((1, LANES), lambda i, j: (0, 0)),
            pl.BlockSpec((1, tn), lambda i, j: (0, jnp.minimum(j, qk_tiles - 1))),
        ],
        out_specs=[
            pl.BlockSpec((tm, tn), lambda i, j: (i, j)),
            pl.BlockSpec((tm, LANES), lambda i, j: (i, 0)),
        ],
        out_shape=[
            jax.ShapeDtypeStruct((m, FOX_MAIN), BF16),
            jax.ShapeDtypeStruct((m, LANES), F32),
        ],
        scratch_shapes=[pltpu.VMEM((tm, D_MODEL), BF16)],
        compiler_params=_params("parallel", "arbitrary"),
        name="fox_proj",
    )(x2, g, wt3, wf, bf, qk_gain)


def _cumsum_kernel(lf_ref, c_ref, *, t_len):
    row = lax.broadcasted_iota(jnp.int32, (t_len, t_len), 0)
    col = lax.broadcasted_iota(jnp.int32, (t_len, t_len), 1)
    tril = (col <= row).astype(BF16)

    def block(t, carry):
        sl = pl.ds(pl.multiple_of(t * t_len, t_len), t_len)
        c = _cumsum_rows(tril, lf_ref[0, sl, :]) + carry
        c_ref[0, sl, :] = c * LOG2E
        return c[t_len - 1:t_len, :]

    lax.fori_loop(0, lf_ref.shape[1] // t_len, block, jnp.zeros((1, LANES), F32))


def _seq_cumsum(lf, t_len):
    b_sz, s_len, _ = lf.shape
    return pl.pallas_call(
        functools.partial(_cumsum_kernel, t_len=t_len),
        grid=(b_sz,),
        in_specs=[pl.BlockSpec((1, s_len, LANES), lambda b: (b, 0, 0))],
        out_specs=pl.BlockSpec((1, s_len, LANES), lambda b: (b, 0, 0)),
        out_shape=jax.ShapeDtypeStruct(lf.shape, F32),
        compiler_params=_params("parallel"),
        name="fox_cumsum",
    )(lf)


def _fox_attn_kernel(ends_ref, firsts_ref, slack_ref, q_ref, k_ref, v_ref, og_ref, ck_ref, o_ref,
                     acc_ref, va_ref, *, blk, n_sub, tk):
    i = pl.program_id(2)
    sub = blk // n_sub
    assert sub == tk
    n_kt = ck_ref.shape[2]
    bh = pl.program_id(0) * pl.num_programs(1) + pl.program_id(1)
    n_loop = jnp.maximum(i * n_sub - 1, 0)
    limit = firsts_ref[bh * pl.num_programs(2) + i] + slack_ref[0]

    def first_needed(j, lo):
        return jnp.minimum(lo, jnp.where(ends_ref[bh * n_kt + j] > limit, n_loop, j))

    j_start = lax.fori_loop(0, n_loop, first_needed, n_loop)

    c0 = ck_ref[0, 0, pl.ds(i * n_sub, 1), :][:, 0:1]
    acc_ref[...] = jnp.zeros_like(acc_ref)

    @pl.when(i == 0)
    def _():
        for r0 in range(0, va_ref.shape[0], blk):
            va_ref[r0:r0 + blk, :FOX_HEAD_DIM] = v_ref[0, r0:r0 + blk, :]
            va_ref[r0:r0 + blk, FOX_HEAD_DIM:] = jnp.ones((blk, FOX_HEAD_DIM), BF16)

    def update(r, out, m_old, ks, va, bias, mask_offset):
        rows = slice(r * sub, (r + 1) * sub)
        s = lax.dot_general(q_ref[0, rows, :], ks, NT_DIMS, preferred_element_type=F32) + bias
        yield
        if mask_offset is not None:
            row = lax.broadcasted_iota(jnp.int32, s.shape, 0)
            col = lax.broadcasted_iota(jnp.int32, s.shape, 1)
            s = jnp.where(col <= row + mask_offset, s, -jnp.inf)
        m_new = jnp.maximum(m_old, jnp.max(s, axis=-1, keepdims=True))
        p = jnp.exp2(s - m_new).astype(BF16)
        alpha = jnp.exp2(m_old - m_new)
        acc_ref[rows, :] = alpha * acc_ref[rows, :] + jnp.dot(p, va, preferred_element_type=F32)
        out[r] = m_new

    def body(j, carry):
        sl = pl.ds(pl.multiple_of(j * tk, tk), tk)
        ks = k_ref[0, sl, :]
        va = va_ref[sl, :]
        bias = c0 - ck_ref[0, 0, pl.ds(j, 1), :]
        out = [None] * n_sub
        _lockstep([update(r, out, carry[r], ks, va, bias, None) for r in range(n_sub)])
        return tuple(out)

    init = tuple(jnp.full((sub, 1), -jnp.inf, F32) for _ in range(n_sub))
    carry = lax.fori_loop(j_start, n_loop, body, init)

    carry = list(carry)
    for r in range(1, n_sub):
        t_prev = i * n_sub + r - 1
        limit_r = ends_ref[bh * n_kt + t_prev] + slack_ref[0]
        for u in range(r):
            g = i * n_sub - 1 + u

            def visit(g=g, r=r):
                out = [None] * n_sub
                sl = pl.ds(pl.multiple_of(g * tk, tk), tk)
                bias = c0 - ck_ref[0, 0, pl.ds(g, 1), :]
                _lockstep([update(r, out, carry[r], k_ref[0, sl, :], va_ref[sl, :], bias, None)])
                return out[r]

            needed = (i > 0) & (ends_ref[bh * n_kt + jnp.maximum(g, 0)] <= limit_r)
            carry[r] = lax.cond(needed, visit, lambda r=r: carry[r])

    final = [None] * n_sub
    steps = []
    for r in range(n_sub):
        t_own = i * n_sub + r
        t_lo = jnp.maximum(t_own - 1, 0)
        win = pl.ds(pl.multiple_of(t_lo * tk, tk), 2 * tk)
        ck_win = jnp.concatenate(
            [ck_ref[0, 0, pl.ds(t_lo, 1), :], ck_ref[0, 0, pl.ds(t_lo + 1, 1), :]], axis=1)
        steps.append(update(r, final, carry[r], k_ref[0, win, :], va_ref[win, :],
                            c0 - ck_win, (t_own - t_lo) * tk))
    _lockstep(steps)
    for r in range(n_sub):
        rows = slice(r * sub, (r + 1) * sub)
        acc = acc_ref[rows, :]
        gate = jax.nn.sigmoid(og_ref[0, rows, :].astype(F32))
        o_ref[0, rows, :] = (acc[:, :FOX_HEAD_DIM] / acc[:, FOX_HEAD_DIM:FOX_HEAD_DIM + 1]
                             * gate).astype(o_ref.dtype)


def _fox_attn(qkvo, ck, slack, b_sz, s_len, blk, n_sub):
    h_cols = D_MODEL // FOX_HEAD_DIM
    n_blk = s_len // blk
    n_kt, tk = ck.shape[2:]
    ends = ck[:, :, :, tk - 1].reshape(-1)
    firsts = ck.reshape(b_sz, FOX_HEADS, n_blk, blk)[:, :, :, 0].reshape(-1)
    grid_spec = pltpu.PrefetchScalarGridSpec(
        num_scalar_prefetch=3,
        grid=(b_sz, FOX_HEADS, n_blk),
        in_specs=[
            pl.BlockSpec((1, blk, FOX_HEAD_DIM), lambda b, h, i, *_: (b, i, h)),
            pl.BlockSpec((1, s_len, FOX_HEAD_DIM), lambda b, h, i, *_: (b, 0, h_cols + h)),
            pl.BlockSpec((1, s_len, FOX_HEAD_DIM), lambda b, h, i, *_: (b, 0, 2 * h_cols + h)),
            pl.BlockSpec((1, blk, FOX_HEAD_DIM), lambda b, h, i, *_: (b, i, 3 * h_cols + h)),
            pl.BlockSpec((1, 1, n_kt, tk), lambda b, h, i, *_: (b, h, 0, 0)),
        ],
        out_specs=pl.BlockSpec((1, blk, FOX_HEAD_DIM), lambda b, h, i, *_: (b, i, h)),
        scratch_shapes=[
            pltpu.VMEM((blk, 2 * FOX_HEAD_DIM), F32),
            pltpu.VMEM((s_len, 2 * FOX_HEAD_DIM), BF16),
        ],
    )
    return pl.pallas_call(
        functools.partial(_fox_attn_kernel, blk=blk, n_sub=n_sub, tk=tk),
        grid_spec=grid_spec,
        out_shape=jax.ShapeDtypeStruct((b_sz, s_len, D_MODEL), BF16),
        compiler_params=_params("parallel", "parallel", "arbitrary"),
        name="fox_attn",
    )(ends, firsts, slack, qkvo, qkvo, qkvo, qkvo, ck)


def _pad_rows(w, n):
    return jnp.pad(w, ((0, n - w.shape[0]), (0, 0)))


def kernel(x, norm_mix, norm_ffn, gla_w_in, gla_w_g2, gla_b_g2, gla_o_gain, gla_w_o,
           fox_w_in, fox_b_f, fox_q_gain, fox_k_gain, fox_w_o,
           ffn_w_gate, ffn_w_up, ffn_w_down):
    b_sz, s_len, d = x.shape
    assert d == D_MODEL and s_len % GLA_CHUNK == 0
    m = b_sz * s_len
    tm = min(1024, m)
    tn = 1024
    tm_res = min(512, m)
    tm_ffn = min(1024, m)
    tf = 256
    gla_t = min(512, s_len)
    attn_blk = min(1024, s_len)
    attn_tk = min(512, s_len // 2)
    attn_sub = attn_tk
    cum_t = min(256, s_len)

    x2 = x.reshape(m, d)

    def ffn_layer(xin, layer):
        return _ffn(xin, norm_ffn[layer][None, :], ffn_w_gate, ffn_w_up, ffn_w_down, layer, tm_ffn, tf)

    gla_wt = jnp.swapaxes(gla_w_in, 1, 2)
    qkvr, la = _gla_proj(
        x2, norm_mix[0][None, :],
        gla_wt,
        _pad_rows(gla_wt[0, GLA_MAIN:, :], LANES),
        _pad_rows(gla_w_g2[0], LANES),
        gla_b_g2[0][None, :], tm, tn)
    og = _gla_mix(qkvr.reshape(b_sz, s_len, GLA_MAIN), la.reshape(b_sz, s_len, GLA_KEY_DIM),
                  gla_o_gain[0][None, :], b_sz, s_len, gla_t, 4)
    x2 = _residual_matmul(x2, og.reshape(m, GLA_VAL_DIM), gla_w_o, tm_res)
    x2 = ffn_layer(x2, 0)

    qk_gain = jnp.concatenate([jnp.tile(fox_q_gain[0], FOX_HEADS), jnp.tile(fox_k_gain[0], FOX_HEADS)])
    f_lo, f_hi = 3 * D_MODEL, 3 * D_MODEL + FOX_HEADS
    fox_wt = jnp.swapaxes(fox_w_in, 1, 2)
    qkvo, lf = _fox_proj(
        x2, norm_mix[1][None, :],
        fox_wt,
        _pad_rows(fox_wt[0, f_lo:f_hi, :], LANES),
        jnp.pad(fox_b_f[0], (0, LANES - FOX_HEADS))[None, :],
        qk_gain[None, :], tm, tn, f_lo, FOX_HEADS)
    c = _seq_cumsum(lf.reshape(b_sz, s_len, LANES), cum_t)
    c_hs = c[:, :, :FOX_HEADS].transpose(0, 2, 1)
    ck = c_hs.reshape(b_sz, FOX_HEADS, s_len // attn_tk, attn_tk)
    qk_max = (FOX_HEAD_DIM * FOX_SCALE * LOG2E * BF16_SLOP
              * jnp.max(jnp.abs(fox_q_gain[0])) * jnp.max(jnp.abs(fox_k_gain[0])))
    slack = (2.0 * qk_max + F32_EXP2_FLOOR).reshape(1)
    o = _fox_attn(qkvo.reshape(b_sz, s_len, FOX_MAIN), ck, slack, b_sz, s_len, attn_blk,
                  attn_blk // attn_sub)
    x2 = _residual_matmul(x2, o.reshape(m, D_MODEL), fox_w_o, tm_res)
    x2 = ffn_layer(x2, 1)
    return x2.reshape(b_sz, s_len, d)
```

```python
import functools

import jax
import jax.numpy as jnp
from jax import lax
from jax.experimental import pallas as pl
from jax.experimental.pallas import tpu as pltpu

F32 = jnp.float32
BF16 = jnp.bfloat16

D_MODEL = 2048
RMS_EPS = 1e-6

GLA_HEADS = 4
GLA_KEY_DIM = D_MODEL // 2
GLA_VAL_DIM = D_MODEL
GLA_HEAD_K = GLA_KEY_DIM // GLA_HEADS
GLA_HEAD_V = GLA_VAL_DIM // GLA_HEADS
GLA_GATE_RANK = 16
GLA_GATE_TEMP = 16.0
GLA_CHUNK = 64
GLA_MAIN = 2 * GLA_KEY_DIM + 2 * GLA_VAL_DIM
GLA_COARSE_LEVELS = (64, 32)
GLA_FINE_LEVELS = (16, 8)
GLA_DIAG = 4
GLA_CHUNKS_PER_ITER = 2

FOX_HEAD_DIM = 128
FOX_HEADS = D_MODEL // FOX_HEAD_DIM
FOX_MAIN = 4 * D_MODEL
FOX_SCALE = FOX_HEAD_DIM ** -0.5
LOG2E = 1.4426950408889634
BF16_SLOP = 1.02
F32_EXP2_FLOOR = 160.0

D_FF = ((8 * D_MODEL + 2) // 3 + 255) // 256 * 256

LANES = 128
SUBLANES = 8
NORM_SLAB = 256
VMEM_LIMIT = 58 * 1024 * 1024

_DONE = object()


def _lockstep(chains):
    pending = list(chains)
    while pending:
        pending = [g for g in pending if next(g, _DONE) is not _DONE]


NT_DIMS = (((1,), (1,)), ((), ()))
TN_DIMS = (((0,), (0,)), ((), ()))


def _params(*sem):
    return pltpu.CompilerParams(dimension_semantics=sem, vmem_limit_bytes=VMEM_LIMIT)


def _rmsnorm_rows(x, g):
    ms = jnp.mean(x * x, axis=-1, keepdims=True)
    return x * lax.rsqrt(ms + RMS_EPS) * g


def _log_sigmoid(z):
    return jnp.minimum(z, 0.0) - jnp.log(1.0 + jnp.exp(-jnp.abs(z)))


def _norm_to_scratch(x_ref, g_ref, h_ref):
    rows = x_ref.shape[0]
    slab = min(NORM_SLAB, rows)
    g = g_ref[...]
    for r0 in range(0, rows, slab):
        h_ref[r0:r0 + slab, :] = _rmsnorm_rows(x_ref[r0:r0 + slab, :], g).astype(BF16)


def _split3(a):
    hi = a.astype(BF16)
    r1 = a - hi.astype(F32)
    mid = r1.astype(BF16)
    lo = (r1 - mid.astype(F32)).astype(BF16)
    return hi, mid, lo


def _cumsum_rows(tril, a, terms=3):
    parts = _split3(a)[:terms]
    out = jnp.dot(tril, parts[0], preferred_element_type=F32)
    for part in parts[1:]:
        out += jnp.dot(tril, part, preferred_element_type=F32)
    return out


def _gla_norm_kernel(x_ref, g_ref, wg1_ref, wg2_ref, bg_ref, h_ref, la_ref):
    _norm_to_scratch(x_ref, g_ref, h_ref)
    h = h_ref[...]
    g1 = lax.dot_general(h, wg1_ref[...].astype(BF16), NT_DIMS, preferred_element_type=F32)
    z = jnp.dot(g1.astype(BF16), wg2_ref[...].astype(BF16),
                preferred_element_type=F32) + bg_ref[...]
    la_ref[...] = _log_sigmoid(z) * (LOG2E / GLA_GATE_TEMP)


def _gla_norm(x2, g, wg1, wg2, bg, tm):
    m = x2.shape[0]
    return pl.pallas_call(
        _gla_norm_kernel,
        grid=(m // tm,),
        in_specs=[
            pl.BlockSpec((tm, D_MODEL), lambda i: (i, 0)),
            pl.BlockSpec((1, D_MODEL), lambda i: (0, 0)),
            pl.BlockSpec((LANES, D_MODEL), lambda i: (0, 0)),
            pl.BlockSpec((LANES, GLA_KEY_DIM), lambda i: (0, 0)),
            pl.BlockSpec((1, GLA_KEY_DIM), lambda i: (0, 0)),
        ],
        out_specs=[
            pl.BlockSpec((tm, D_MODEL), lambda i: (i, 0)),
            pl.BlockSpec((tm, GLA_KEY_DIM), lambda i: (i, 0)),
        ],
        out_shape=[
            jax.ShapeDtypeStruct((m, D_MODEL), BF16),
            jax.ShapeDtypeStruct((m, GLA_KEY_DIM), F32),
        ],
        compiler_params=_params("parallel"),
        name="gla_norm",
    )(x2, g, wg1, wg2, bg)


def _gla_proj_kernel(h_ref, w_ref, o_ref, wb_ref):
    @pl.when(pl.program_id(1) == 0)
    def _():
        for r0 in range(0, wb_ref.shape[0], NORM_SLAB):
            wb_ref[r0:r0 + NORM_SLAB, :] = w_ref[0, r0:r0 + NORM_SLAB, :].astype(BF16)

    o_ref[...] = lax.dot_general(h_ref[...], wb_ref[...], NT_DIMS,
                                 preferred_element_type=F32).astype(o_ref.dtype)


def _gla_proj(h, wt3, tm, tn):
    m = h.shape[0]
    return pl.pallas_call(
        _gla_proj_kernel,
        grid=(GLA_MAIN // tn, m // tm),
        in_specs=[
            pl.BlockSpec((tm, D_MODEL), lambda j, i: (i, 0)),
            pl.BlockSpec((1, tn, D_MODEL), lambda j, i: (0, j, 0)),
        ],
        out_specs=pl.BlockSpec((tm, tn), lambda j, i: (i, j)),
        out_shape=jax.ShapeDtypeStruct((m, GLA_MAIN), BF16),
        scratch_shapes=[pltpu.VMEM((tn, D_MODEL), BF16)],
        compiler_params=_params("arbitrary", "arbitrary"),
        name="gla_proj",
    )(h, wt3)


def _gla_kernel(q_ref, k_ref, v_ref, r_ref, la_ref, gain_ref, o_ref, st_ref, *, n_chunks, heads):
    c_len = GLA_CHUNK

    @pl.when(pl.program_id(2) == 0)
    def _():
        st_ref[...] = jnp.zeros_like(st_ref)

    row = lax.broadcasted_iota(jnp.int32, (c_len, c_len), 0)
    col = lax.broadcasted_iota(jnp.int32, (c_len, c_len), 1)
    tril = (col <= row).astype(BF16)
    delta = row - col
    band = jnp.where((delta >= 0) & (delta <= row % GLA_DIAG), delta, -1)
    fine_masks = [
        (blk, (row // blk == col // blk) & (row % blk >= blk // 2) & (col % blk < blk // 2))
        for blk in GLA_FINE_LEVELS]
    half_rows = c_len // 2
    hrow = lax.broadcasted_iota(jnp.int32, (half_rows, half_rows), 0)
    hcol = lax.broadcasted_iota(jnp.int32, (half_rows, half_rows), 1)
    gain = gain_ref[...]
    scale = GLA_HEAD_K ** -0.5

    def gather_rows(a, blk, second_half):
        half = blk // 2
        off = half if second_half else 0
        return jnp.concatenate([a[s + off:s + off + half] for s in range(0, c_len, blk)], axis=0)

    def head_chunk(hd, sl):
        ksl = slice(hd * GLA_HEAD_K, (hd + 1) * GLA_HEAD_K)
        vsl = slice(hd * GLA_HEAD_V, (hd + 1) * GLA_HEAD_V)
        la = la_ref[0, sl, ksl]
        b = _cumsum_rows(tril, la, terms=2)
        yield
        b_last = b[c_len - 1:c_len, :]
        qf = q_ref[0, sl, ksl].astype(F32)
        kf = k_ref[0, sl, ksl].astype(F32)
        v = v_ref[0, sl, vsl]

        st = st_ref[hd]
        qi = (qf * jnp.exp2(b)).astype(BF16)
        o = lax.dot_general(qi, st.astype(BF16), NT_DIMS, preferred_element_type=F32)
        kd = (kf * jnp.exp2(b_last - b)).astype(BF16)
        st_ref[hd] = st * jnp.exp2(b_last) + lax.dot_general(
            v, kd, TN_DIMS, preferred_element_type=F32)
        yield

        parts = []
        for blk in GLA_COARSE_LEVELS:
            half = blk // 2
            ref = jnp.concatenate(
                [jnp.broadcast_to(b[s + half - 1:s + half, :], (half, GLA_HEAD_K))
                 for s in range(0, c_len, blk)], axis=0)
            ql = (gather_rows(qf, blk, True) * jnp.exp2(gather_rows(b, blk, True) - ref)).astype(BF16)
            kl = (gather_rows(kf, blk, False) * jnp.exp2(ref - gather_rows(b, blk, False))).astype(BF16)
            a_l = lax.dot_general(ql, kl, NT_DIMS, preferred_element_type=F32)
            yield
            if blk < c_len:
                a_l = jnp.where(hrow // half == hcol // half, a_l, 0.0)
            parts.append((blk, jnp.dot(a_l.astype(BF16), gather_rows(v, blk, False),
                                       preferred_element_type=F32)))
            yield

        attn = jnp.zeros((c_len, c_len), F32)
        for blk, mk in fine_masks:
            half = blk // 2
            ref = jnp.concatenate(
                [jnp.broadcast_to(b[s + half - 1:s + half, :], (blk, GLA_HEAD_K))
                 for s in range(0, c_len, blk)], axis=0)
            ql = (qf * jnp.exp2(b - ref)).astype(BF16)
            kl = (kf * jnp.exp2(ref - b)).astype(BF16)
            a_l = lax.dot_general(ql, kl, NT_DIMS, preferred_element_type=F32)
            yield
            attn = jnp.where(mk, a_l, attn)
        for d in range(GLA_DIAG):
            if d:
                prod = qf * pltpu.roll(kf, d, 0) * jnp.exp2(b - pltpu.roll(b, d, 0))
            else:
                prod = qf * kf
            attn = jnp.where(band == d, jnp.sum(prod, axis=-1, keepdims=True), attn)
        o += jnp.dot(attn.astype(BF16), v, preferred_element_type=F32)
        yield

        groups = [o[g:g + SUBLANES] for g in range(0, c_len, SUBLANES)]
        for blk, part in parts:
            half = blk // 2
            src = 0
            for s in range(0, c_len, blk):
                for g in range((s + half) // SUBLANES, (s + blk) // SUBLANES):
                    groups[g] = groups[g] + part[src:src + SUBLANES]
                    src += SUBLANES
        o = jnp.concatenate(groups, axis=0)

        ms = jnp.mean(o * o, axis=-1, keepdims=True)
        factor = scale * lax.rsqrt(scale * scale * ms + RMS_EPS)
        r = r_ref[0, sl, vsl].astype(F32)
        o_ref[0, sl, vsl] = (o * factor * gain * (r * jax.nn.sigmoid(r))).astype(o_ref.dtype)

    def chunk_group(c, carry):
        chains = []
        for u in range(GLA_CHUNKS_PER_ITER):
            sl = pl.ds(pl.multiple_of((c * GLA_CHUNKS_PER_ITER + u) * c_len, c_len), c_len)
            chains += [head_chunk(hd, sl) for hd in range(heads)]
        _lockstep(chains)
        return carry

    assert n_chunks % GLA_CHUNKS_PER_ITER == 0
    lax.fori_loop(0, n_chunks // GLA_CHUNKS_PER_ITER, chunk_group, 0)


def _gla_mix(qkvr, la, gain, b_sz, s_len, t_len, heads):
    n_chunks = t_len // GLA_CHUNK
    wk = heads * GLA_HEAD_K
    wv = heads * GLA_HEAD_V
    kq = GLA_KEY_DIM // wk
    kv = 2 * GLA_KEY_DIM // wv
    kr = kv + GLA_VAL_DIM // wv
    return pl.pallas_call(
        functools.partial(_gla_kernel, n_chunks=n_chunks, heads=heads),
        grid=(b_sz, GLA_HEADS // heads, s_len // t_len),
        in_specs=[
            pl.BlockSpec((1, t_len, wk), lambda b, h, t: (b, t, h)),
            pl.BlockSpec((1, t_len, wk), lambda b, h, t: (b, t, kq + h)),
            pl.BlockSpec((1, t_len, wv), lambda b, h, t: (b, t, kv + h)),
            pl.BlockSpec((1, t_len, wv), lambda b, h, t: (b, t, kr + h)),
            pl.BlockSpec((1, t_len, wk), lambda b, h, t: (b, t, h)),
            pl.BlockSpec((1, GLA_HEAD_V), lambda b, h, t: (0, 0)),
        ],
        out_specs=pl.BlockSpec((1, t_len, wv), lambda b, h, t: (b, t, h)),
        out_shape=jax.ShapeDtypeStruct((b_sz, s_len, GLA_VAL_DIM), BF16),
        scratch_shapes=[pltpu.VMEM((heads, GLA_HEAD_V, GLA_HEAD_K), F32)],
        compiler_params=_params("parallel", "parallel", "arbitrary"),
        name="gla_mix",
    )(qkvr, qkvr, qkvr, qkvr, la, gain)


def _residual_matmul_kernel(x_ref, a_ref, w_ref, o_ref, wb_ref):
    @pl.when(pl.program_id(0) == 0)
    def _():
        for r0 in range(0, w_ref.shape[1], NORM_SLAB):
            wb_ref[r0:r0 + NORM_SLAB, :] = w_ref[0, r0:r0 + NORM_SLAB, :].astype(BF16)

    o_ref[...] = x_ref[...] + jnp.dot(a_ref[...], wb_ref[...], preferred_element_type=F32)


def _residual_matmul(x2, a, w3, tm):
    m, k_dim = a.shape
    n = w3.shape[2]
    return pl.pallas_call(
        _residual_matmul_kernel,
        grid=(m // tm,),
        in_specs=[
            pl.BlockSpec((tm, n), lambda i: (i, 0)),
            pl.BlockSpec((tm, k_dim), lambda i: (i, 0)),
            pl.BlockSpec((1, k_dim, n), lambda i: (0, 0, 0), pipeline_mode=pl.Buffered(1)),
        ],
        out_specs=pl.BlockSpec((tm, n), lambda i: (i, 0)),
        out_shape=jax.ShapeDtypeStruct((m, n), F32),
        scratch_shapes=[pltpu.VMEM((k_dim, n), BF16)],
        compiler_params=_params("arbitrary"),
        name="residual_matmul",
    )(x2, a, w3)


def _ffn_kernel(x_ref, g_ref, wg_hbm, wu_hbm, wd_hbm, o_ref, h_ref, wg_buf, wu_buf, wd_buf, sem,
                *, layer, tf):
    i = pl.program_id(0)
    n_f = D_FF // tf
    assert n_f % 2 == 0

    def tile_copies(f, slot):
        cols = pl.ds(pl.multiple_of(f * tf, tf), tf)
        return (
            pltpu.make_async_copy(wg_hbm.at[layer, :, cols], wg_buf.at[slot], sem.at[0, slot]),
            pltpu.make_async_copy(wu_hbm.at[layer, :, cols], wu_buf.at[slot], sem.at[1, slot]),
            pltpu.make_async_copy(wd_hbm.at[layer, cols, :], wd_buf.at[slot], sem.at[2, slot]),
        )

    @pl.when(i == 0)
    def _():
        for cp in tile_copies(0, 0):
            cp.start()

    _norm_to_scratch(x_ref, g_ref, h_ref)

    def use_tile(f, slot, acc_ref):
        for cp in tile_copies(f, slot):
            cp.wait()
        nxt = jnp.where(f + 1 == n_f, 0, f + 1)
        for cp in tile_copies(nxt, 1 - slot):
            cp.start()
        h = h_ref[...]
        gate = jnp.dot(h, wg_buf[slot].astype(BF16), preferred_element_type=F32)
        up = jnp.dot(h, wu_buf[slot].astype(BF16), preferred_element_type=F32)
        act = (gate * jax.nn.sigmoid(gate) * up).astype(BF16)
        o_ref[...] = acc_ref[...] + jnp.dot(act, wd_buf[slot].astype(BF16),
                                            preferred_element_type=F32)

    def pair(p, carry):
        use_tile(2 * p, 0, o_ref)
        use_tile(2 * p + 1, 1, o_ref)
        return carry

    use_tile(0, 0, x_ref)
    use_tile(1, 1, o_ref)
    lax.fori_loop(1, n_f // 2, pair, 0)

    @pl.when(i == pl.num_programs(0) - 1)
    def _():
        for cp in tile_copies(0, 0):
            cp.wait()


def _ffn(x2, g, wg, wu, wd, layer, tm, tf):
    m = x2.shape[0]
    return pl.pallas_call(
        functools.partial(_ffn_kernel, layer=layer, tf=tf),
        grid=(m // tm,),
        in_specs=[
            pl.BlockSpec((tm, D_MODEL), lambda i: (i, 0)),
            pl.BlockSpec((1, D_MODEL), lambda i: (0, 0)),
            pl.BlockSpec(memory_space=pl.ANY),
            pl.BlockSpec(memory_space=pl.ANY),
            pl.BlockSpec(memory_space=pl.ANY),
        ],
        out_specs=pl.BlockSpec((tm, D_MODEL), lambda i: (i, 0)),
        out_shape=jax.ShapeDtypeStruct((m, D_MODEL), F32),
        scratch_shapes=[
            pltpu.VMEM((tm, D_MODEL), BF16),
            pltpu.VMEM((2, D_MODEL, tf), F32),
            pltpu.VMEM((2, D_MODEL, tf), F32),
            pltpu.VMEM((2, tf, D_MODEL), F32),
            pltpu.SemaphoreType.DMA((3, 2)),
        ],
        compiler_params=_params("arbitrary"),
        name="ffn",
    )(x2, g, wg, wu, wd)


def _fox_proj_kernel(x_ref, g_ref, w_ref, wf_ref, bf_ref, qkg_ref, o_ref, lf_ref, h_ref,
                     *, q_tiles, qk_tiles):
    j = pl.program_id(1)

    @pl.when(j == 0)
    def _():
        _norm_to_scratch(x_ref, g_ref, h_ref)
        h = h_ref[...]
        z = lax.dot_general(h, wf_ref[...].astype(BF16), NT_DIMS,
                            preferred_element_type=F32) + bf_ref[...]
        lf_ref[...] = _log_sigmoid(z)

    def project():
        return lax.dot_general(h_ref[...], w_ref[0].astype(BF16), NT_DIMS,
                               preferred_element_type=F32)

    @pl.when(j < qk_tiles)
    def _():
        acc = project()
        mult = jnp.where(j < q_tiles, FOX_SCALE * LOG2E, 1.0).astype(F32)
        for c in range(acc.shape[1] // FOX_HEAD_DIM):
            cs = slice(c * FOX_HEAD_DIM, (c + 1) * FOX_HEAD_DIM)
            xs = acc[:, cs]
            ms = jnp.mean(xs * xs, axis=-1, keepdims=True)
            o_ref[:, cs] = (xs * (lax.rsqrt(ms + RMS_EPS) * mult) * qkg_ref[:, cs]).astype(o_ref.dtype)

    @pl.when(j >= qk_tiles)
    def _():
        o_ref[...] = project().astype(o_ref.dtype)


def _fox_proj(x2, g, wt3, wf, bf, qk_gain, tm, tn, gap_at, gap):
    m = x2.shape[0]
    q_tiles = D_MODEL // tn
    qk_tiles = 2 * q_tiles
    assert gap_at % tn == 0 and gap % SUBLANES == 0
    first_after = gap_at // tn
    return pl.pallas_call(
        functools.partial(_fox_proj_kernel, q_tiles=q_tiles, qk_tiles=qk_tiles),
        grid=(m // tm, FOX_MAIN // tn),
        in_specs=[
            pl.BlockSpec((tm, D_MODEL), lambda i, j: (i, 0)),
            pl.BlockSpec((1, D_MODEL), lambda i, j: (0, 0)),
            pl.BlockSpec(
                (pl.Element(1), pl.Element(tn), pl.Element(D_MODEL)),
                lambda i, j: (0, pl.multiple_of(j * tn + jnp.where(j >= first_after, gap, 0),
                                                SUBLANES), 0)),
            pl.BlockSpec((LANES, D_MODEL), lambda i, j: (0, 0)),
            pl.BlockSpec((1, LANES), lambda i, j: (0, 0)),
            pl.BlockSpec((1, tn), lambda i, j: (0, jnp.minimum(j, qk_tiles - 1))),
        ],
        out_specs=[
            pl.BlockSpec((tm, tn), lambda i, j: (i, j)),
            pl.BlockSpec((tm, LANES), lambda i, j: (i, 0)),
        ],
        out_shape=[
            jax.ShapeDtypeStruct((m, FOX_MAIN), BF16),
            jax.ShapeDtypeStruct((m, LANES), F32),
        ],
        scratch_shapes=[pltpu.VMEM((tm, D_MODEL), BF16)],
        compiler_params=_params("parallel", "arbitrary"),
        name="fox_proj",
    )(x2, g, wt3, wf, bf, qk_gain)


def _cumsum_kernel(lf_ref, c_ref, *, t_len):
    row = lax.broadcasted_iota(jnp.int32, (t_len, t_len), 0)
    col = lax.broadcasted_iota(jnp.int32, (t_len, t_len), 1)
    tril = (col <= row).astype(BF16)

    def block(t, carry):
        sl = pl.ds(pl.multiple_of(t * t_len, t_len), t_len)
        c = _cumsum_rows(tril, lf_ref[0, sl, :]) + carry
        c_ref[0, sl, :] = c * LOG2E
        return c[t_len - 1:t_len, :]

    lax.fori_loop(0, lf_ref.shape[1] // t_len, block, jnp.zeros((1, LANES), F32))


def _seq_cumsum(lf, t_len):
    b_sz, s_len, _ = lf.shape
    return pl.pallas_call(
        functools.partial(_cumsum_kernel, t_len=t_len),
        grid=(b_sz,),
        in_specs=[pl.BlockSpec((1, s_len, LANES), lambda b: (b, 0, 0))],
        out_specs=pl.BlockSpec((1, s_len, LANES), lambda b: (b, 0, 0)),
        out_shape=jax.ShapeDtypeStruct(lf.shape, F32),
        compiler_params=_params("parallel"),
        name="fox_cumsum",
    )(lf)


def _fox_attn_kernel(ends_ref, firsts_ref, slack_ref, q_ref, k_ref, v_ref, og_ref, ck_ref, o_ref,
                     acc_ref, va_ref, *, blk, n_sub, tk):
    i = pl.program_id(2)
    sub = blk // n_sub
    assert sub == tk
    n_kt = ck_ref.shape[2]
    bh = pl.program_id(0) * pl.num_programs(1) + pl.program_id(1)
    n_loop = jnp.maximum(i * n_sub - 1, 0)
    limit = firsts_ref[bh * pl.num_programs(2) + i] + slack_ref[0]

    def first_needed(j, lo):
        return jnp.minimum(lo, jnp.where(ends_ref[bh * n_kt + j] > limit, n_loop, j))

    j_start = lax.fori_loop(0, n_loop, first_needed, n_loop)

    c0 = ck_ref[0, 0, pl.ds(i * n_sub, 1), :][:, 0:1]
    acc_ref[...] = jnp.zeros_like(acc_ref)

    @pl.when(i == 0)
    def _():
        for r0 in range(0, va_ref.shape[0], blk):
            va_ref[r0:r0 + blk, :FOX_HEAD_DIM] = v_ref[0, r0:r0 + blk, :]
            va_ref[r0:r0 + blk, FOX_HEAD_DIM:] = jnp.ones((blk, FOX_HEAD_DIM), BF16)

    def update(r, out, m_old, ks, va, bias, mask_offset):
        rows = slice(r * sub, (r + 1) * sub)
        s = lax.dot_general(q_ref[0, rows, :], ks, NT_DIMS, preferred_element_type=F32) + bias
        yield
        if mask_offset is not None:
            row = lax.broadcasted_iota(jnp.int32, s.shape, 0)
            col = lax.broadcasted_iota(jnp.int32, s.shape, 1)
            s = jnp.where(col <= row + mask_offset, s, -jnp.inf)
        m_new = jnp.maximum(m_old, jnp.max(s, axis=-1, keepdims=True))
        p = jnp.exp2(s - m_new).astype(BF16)
        alpha = jnp.exp2(m_old - m_new)
        acc_ref[rows, :] = alpha * acc_ref[rows, :] + jnp.dot(p, va, preferred_element_type=F32)
        out[r] = m_new

    def body(j, carry):
        sl = pl.ds(pl.multiple_of(j * tk, tk), tk)
        ks = k_ref[0, sl, :]
        va = va_ref[sl, :]
        bias = c0 - ck_ref[0, 0, pl.ds(j, 1), :]
        out = [None] * n_sub
        _lockstep([update(r, out, carry[r], ks, va, bias, None) for r in range(n_sub)])
        return tuple(out)

    init = tuple(jnp.full((sub, 1), -jnp.inf, F32) for _ in range(n_sub))
    carry = lax.fori_loop(j_start, n_loop, body, init)

    carry = list(carry)
    for r in range(1, n_sub):
        t_prev = i * n_sub + r - 1
        limit_r = ends_ref[bh * n_kt + t_prev] + slack_ref[0]
        for u in range(r):
            g = i * n_sub - 1 + u

            def visit(g=g, r=r):
                out = [None] * n_sub
                sl = pl.ds(pl.multiple_of(g * tk, tk), tk)
                bias = c0 - ck_ref[0, 0, pl.ds(g, 1), :]
                _lockstep([update(r, out, carry[r], k_ref[0, sl, :], va_ref[sl, :], bias, None)])
                return out[r]

            needed = (i > 0) & (ends_ref[bh * n_kt + jnp.maximum(g, 0)] <= limit_r)
            carry[r] = lax.cond(needed, visit, lambda r=r: carry[r])

    final = [None] * n_sub
    steps = []
    for r in range(n_sub):
        t_own = i * n_sub + r
        t_lo = jnp.maximum(t_own - 1, 0)
        win = pl.ds(pl.multiple_of(t_lo * tk, tk), 2 * tk)
        ck_win = jnp.concatenate(
            [ck_ref[0, 0, pl.ds(t_lo, 1), :], ck_ref[0, 0, pl.ds(t_lo + 1, 1), :]], axis=1)
        steps.append(update(r, final, carry[r], k_ref[0, win, :], va_ref[win, :],
                            c0 - ck_win, (t_own - t_lo) * tk))
    _lockstep(steps)
    for r in range(n_sub):
        rows = slice(r * sub, (r + 1) * sub)
        acc = acc_ref[rows, :]
        gate = jax.nn.sigmoid(og_ref[0, rows, :].astype(F32))
        o_ref[0, rows, :] = (acc[:, :FOX_HEAD_DIM] / acc[:, FOX_HEAD_DIM:FOX_HEAD_DIM + 1]
                             * gate).astype(o_ref.dtype)


def _fox_attn(qkvo, ck, slack, b_sz, s_len, blk, n_sub):
    h_cols = D_MODEL // FOX_HEAD_DIM
    n_blk = s_len // blk
    n_kt, tk = ck.shape[2:]
    ends = ck[:, :, :, tk - 1].reshape(-1)
    firsts = ck.reshape(b_sz, FOX_HEADS, n_blk, blk)[:, :, :, 0].reshape(-1)
    grid_spec = pltpu.PrefetchScalarGridSpec(
        num_scalar_prefetch=3,
        grid=(b_sz, FOX_HEADS, n_blk),
        in_specs=[
            pl.BlockSpec((1, blk, FOX_HEAD_DIM), lambda b, h, i, *_: (b, i, h)),
            pl.BlockSpec((1, s_len, FOX_HEAD_DIM), lambda b, h, i, *_: (b, 0, h_cols + h)),
            pl.BlockSpec((1, s_len, FOX_HEAD_DIM), lambda b, h, i, *_: (b, 0, 2 * h_cols + h)),
            pl.BlockSpec((1, blk, FOX_HEAD_DIM), lambda b, h, i, *_: (b, i, 3 * h_cols + h)),
            pl.BlockSpec((1, 1, n_kt, tk), lambda b, h, i, *_: (b, h, 0, 0)),
        ],
        out_specs=pl.BlockSpec((1, blk, FOX_HEAD_DIM), lambda b, h, i, *_: (b, i, h)),
        scratch_shapes=[
            pltpu.VMEM((blk, 2 * FOX_HEAD_DIM), F32),
            pltpu.VMEM((s_len, 2 * FOX_HEAD_DIM), BF16),
        ],
    )
    return pl.pallas_call(
        functools.partial(_fox_attn_kernel, blk=blk, n_sub=n_sub, tk=tk),
        grid_spec=grid_spec,
        out_shape=jax.ShapeDtypeStruct((b_sz, s_len, D_MODEL), BF16),
        compiler_params=_params("parallel", "parallel", "arbitrary"),
        name="fox_attn",
    )(ends, firsts, slack, qkvo, qkvo, qkvo, qkvo, ck)


def _pad_rows(w, n):
    return jnp.pad(w, ((0, n - w.shape[0]), (0, 0)))


def kernel(x, norm_mix, norm_ffn, gla_w_in, gla_w_g2, gla_b_g2, gla_o_gain, gla_w_o,
           fox_w_in, fox_b_f, fox_q_gain, fox_k_gain, fox_w_o,
           ffn_w_gate, ffn_w_up, ffn_w_down):
    b_sz, s_len, d = x.shape
    assert d == D_MODEL and s_len % GLA_CHUNK == 0
    m = b_sz * s_len
    tm = min(1024, m)
    tn = 1024
    tm_gla = min(2048, m)
    tm_res = min(512, m)
    tm_ffn = min(1024, m)
    tf = 256
    gla_t = min(1024, s_len)
    attn_blk = min(1024, s_len)
    attn_tk = min(512, s_len // 2)
    attn_sub = attn_tk
    cum_t = min(256, s_len)

    x2 = x.reshape(m, d)

    def ffn_layer(xin, layer):
        return _ffn(xin, norm_ffn[layer][None, :], ffn_w_gate, ffn_w_up, ffn_w_down, layer, tm_ffn, tf)

    gla_wt = jnp.swapaxes(gla_w_in, 1, 2)
    h, la = _gla_norm(x2, norm_mix[0][None, :], _pad_rows(gla_wt[0, GLA_MAIN:, :], LANES),
                      _pad_rows(gla_w_g2[0], LANES), gla_b_g2[0][None, :], tm)
    qkvr = _gla_proj(h, gla_wt, tm_gla, tn)
    og = _gla_mix(qkvr.reshape(b_sz, s_len, GLA_MAIN), la.reshape(b_sz, s_len, GLA_KEY_DIM),
                  gla_o_gain[0][None, :], b_sz, s_len, gla_t, 4)
    x2 = _residual_matmul(x2, og.reshape(m, GLA_VAL_DIM), gla_w_o, tm_res)
    x2 = ffn_layer(x2, 0)

    qk_gain = jnp.concatenate([jnp.tile(fox_q_gain[0], FOX_HEADS), jnp.tile(fox_k_gain[0], FOX_HEADS)])
    f_lo, f_hi = 3 * D_MODEL, 3 * D_MODEL + FOX_HEADS
    fox_wt = jnp.swapaxes(fox_w_in, 1, 2)
    qkvo, lf = _fox_proj(
        x2, norm_mix[1][None, :],
        fox_wt,
        _pad_rows(fox_wt[0, f_lo:f_hi, :], LANES),
        jnp.pad(fox_b_f[0], (0, LANES - FOX_HEADS))[None, :],
        qk_gain[None, :], tm, tn, f_lo, FOX_HEADS)
    c = _seq_cumsum(lf.reshape(b_sz, s_len, LANES), cum_t)
    c_hs = c[:, :, :FOX_HEADS].transpose(0, 2, 1)
    ck = c_hs.reshape(b_sz, FOX_HEADS, s_len // attn_tk, attn_tk)
    qk_max = (FOX_HEAD_DIM * FOX_SCALE * LOG2E * BF16_SLOP
              * jnp.max(jnp.abs(fox_q_gain[0])) * jnp.max(jnp.abs(fox_k_gain[0])))
    slack = (2.0 * qk_max + F32_EXP2_FLOOR).reshape(1)
    o = _fox_attn(qkvo.reshape(b_sz, s_len, FOX_MAIN), ck, slack, b_sz, s_len, attn_blk,
                  attn_blk // attn_sub)
    x2 = _residual_matmul(x2, o.reshape(m, D_MODEL), fox_w_o, tm_res)
    x2 = ffn_layer(x2, 1)
    return x2.reshape(b_sz, s_len, d)
```

```python
import functools

import jax
import jax.numpy as jnp
from jax import lax
from jax.experimental import pallas as pl
from jax.experimental.pallas import tpu as pltpu

F32 = jnp.float32
BF16 = jnp.bfloat16

D_MODEL = 2048
RMS_EPS = 1e-6

GLA_HEADS = 4
GLA_KEY_DIM = D_MODEL // 2
GLA_VAL_DIM = D_MODEL
GLA_HEAD_K = GLA_KEY_DIM // GLA_HEADS
GLA_HEAD_V = GLA_VAL_DIM // GLA_HEADS
GLA_GATE_RANK = 16
GLA_GATE_TEMP = 16.0
GLA_CHUNK = 64
GLA_MAIN = 2 * GLA_KEY_DIM + 2 * GLA_VAL_DIM
GLA_COARSE_LEVELS = (64, 32)
GLA_FINE_LEVELS = (16, 8)
GLA_DIAG = 4
GLA_CHUNKS_PER_ITER = 2

FOX_HEAD_DIM = 128
FOX_HEADS = D_MODEL // FOX_HEAD_DIM
FOX_MAIN = 4 * D_MODEL
FOX_SCALE = FOX_HEAD_DIM ** -0.5
LOG2E = 1.4426950408889634
BF16_SLOP = 1.02
F32_EXP2_FLOOR = 160.0

D_FF = ((8 * D_MODEL + 2) // 3 + 255) // 256 * 256

LANES = 128
SUBLANES = 8
NORM_SLAB = 256
VMEM_LIMIT = 58 * 1024 * 1024

_DONE = object()


def _lockstep(chains):
    pending = list(chains)
    while pending:
        pending = [g for g in pending if next(g, _DONE) is not _DONE]


NT_DIMS = (((1,), (1,)), ((), ()))
TN_DIMS = (((0,), (0,)), ((), ()))


def _params(*sem):
    return pltpu.CompilerParams(dimension_semantics=sem, vmem_limit_bytes=VMEM_LIMIT)


def _rmsnorm_rows(x, g):
    ms = jnp.mean(x * x, axis=-1, keepdims=True)
    return x * lax.rsqrt(ms + RMS_EPS) * g


def _log_sigmoid(z):
    return jnp.minimum(z, 0.0) - jnp.log(1.0 + jnp.exp(-jnp.abs(z)))


def _norm_to_scratch(x_ref, g_ref, h_ref):
    rows = x_ref.shape[0]
    slab = min(NORM_SLAB, rows)
    g = g_ref[...]
    for r0 in range(0, rows, slab):
        h_ref[r0:r0 + slab, :] = _rmsnorm_rows(x_ref[r0:r0 + slab, :], g).astype(BF16)


def _split3(a):
    hi = a.astype(BF16)
    r1 = a - hi.astype(F32)
    mid = r1.astype(BF16)
    lo = (r1 - mid.astype(F32)).astype(BF16)
    return hi, mid, lo


def _cumsum_rows(tril, a, terms=3):
    parts = _split3(a)[:terms]
    out = jnp.dot(tril, parts[0], preferred_element_type=F32)
    for part in parts[1:]:
        out += jnp.dot(tril, part, preferred_element_type=F32)
    return out


def _gla_norm_kernel(x_ref, g_ref, wg1_ref, wg2_ref, bg_ref, h_ref, la_ref):
    _norm_to_scratch(x_ref, g_ref, h_ref)
    h = h_ref[...]
    g1 = lax.dot_general(h, wg1_ref[...].astype(BF16), NT_DIMS, preferred_element_type=F32)
    z = jnp.dot(g1.astype(BF16), wg2_ref[...].astype(BF16),
                preferred_element_type=F32) + bg_ref[...]
    la_ref[...] = _log_sigmoid(z) * (LOG2E / GLA_GATE_TEMP)


def _gla_norm(x2, g, wg1, wg2, bg, tm):
    m = x2.shape[0]
    return pl.pallas_call(
        _gla_norm_kernel,
        grid=(m // tm,),
        in_specs=[
            pl.BlockSpec((tm, D_MODEL), lambda i: (i, 0)),
            pl.BlockSpec((1, D_MODEL), lambda i: (0, 0)),
            pl.BlockSpec((LANES, D_MODEL), lambda i: (0, 0)),
            pl.BlockSpec((LANES, GLA_KEY_DIM), lambda i: (0, 0)),
            pl.BlockSpec((1, GLA_KEY_DIM), lambda i: (0, 0)),
        ],
        out_specs=[
            pl.BlockSpec((tm, D_MODEL), lambda i: (i, 0)),
            pl.BlockSpec((tm, GLA_KEY_DIM), lambda i: (i, 0)),
        ],
        out_shape=[
            jax.ShapeDtypeStruct((m, D_MODEL), BF16),
            jax.ShapeDtypeStruct((m, GLA_KEY_DIM), F32),
        ],
        compiler_params=_params("parallel"),
        name="gla_norm",
    )(x2, g, wg1, wg2, bg)


def _gla_proj_kernel(h_ref, w_ref, o_ref, wb_ref):
    @pl.when(pl.program_id(1) == 0)
    def _():
        for r0 in range(0, wb_ref.shape[0], NORM_SLAB):
            wb_ref[r0:r0 + NORM_SLAB, :] = w_ref[0, r0:r0 + NORM_SLAB, :].astype(BF16)

    o_ref[...] = lax.dot_general(h_ref[...], wb_ref[...], NT_DIMS,
                                 preferred_element_type=F32).astype(o_ref.dtype)


def _gla_proj(h, wt3, tm, tn):
    m = h.shape[0]
    return pl.pallas_call(
        _gla_proj_kernel,
        grid=(GLA_MAIN // tn, m // tm),
        in_specs=[
            pl.BlockSpec((tm, D_MODEL), lambda j, i: (i, 0)),
            pl.BlockSpec((1, tn, D_MODEL), lambda j, i: (0, j, 0)),
        ],
        out_specs=pl.BlockSpec((tm, tn), lambda j, i: (i, j)),
        out_shape=jax.ShapeDtypeStruct((m, GLA_MAIN), BF16),
        scratch_shapes=[pltpu.VMEM((tn, D_MODEL), BF16)],
        compiler_params=_params("arbitrary", "arbitrary"),
        name="gla_proj",
    )(h, wt3)


def _gla_kernel(q_ref, k_ref, v_ref, r_ref, la_ref, gain_ref, o_ref, st_ref, *, n_chunks, heads):
    c_len = GLA_CHUNK

    @pl.when(pl.program_id(2) == 0)
    def _():
        st_ref[...] = jnp.zeros_like(st_ref)

    row = lax.broadcasted_iota(jnp.int32, (c_len, c_len), 0)
    col = lax.broadcasted_iota(jnp.int32, (c_len, c_len), 1)
    tril = (col <= row).astype(BF16)
    delta = row - col
    band = jnp.where((delta >= 0) & (delta <= row % GLA_DIAG), delta, -1)
    fine_masks = [
        (blk, (row // blk == col // blk) & (row % blk >= blk // 2) & (col % blk < blk // 2))
        for blk in GLA_FINE_LEVELS]
    half_rows = c_len // 2
    hrow = lax.broadcasted_iota(jnp.int32, (half_rows, half_rows), 0)
    hcol = lax.broadcasted_iota(jnp.int32, (half_rows, half_rows), 1)
    gain = gain_ref[...]
    scale = GLA_HEAD_K ** -0.5

    def gather_rows(a, blk, second_half):
        half = blk // 2
        off = half if second_half else 0
        return jnp.concatenate([a[s + off:s + off + half] for s in range(0, c_len, blk)], axis=0)

    def head_chunk(hd, sl):
        ksl = slice(hd * GLA_HEAD_K, (hd + 1) * GLA_HEAD_K)
        vsl = slice(hd * GLA_HEAD_V, (hd + 1) * GLA_HEAD_V)
        la = la_ref[0, sl, ksl]
        b = _cumsum_rows(tril, la, terms=2)
        yield
        b_last = b[c_len - 1:c_len, :]
        qf = q_ref[0, sl, ksl].astype(F32)
        kf = k_ref[0, sl, ksl].astype(F32)
        v = v_ref[0, sl, vsl]

        st = st_ref[hd]
        qi = (qf * jnp.exp2(b)).astype(BF16)
        o = lax.dot_general(qi, st.astype(BF16), NT_DIMS, preferred_element_type=F32)
        kd = (kf * jnp.exp2(b_last - b)).astype(BF16)
        st_ref[hd] = st * jnp.exp2(b_last) + lax.dot_general(
            v, kd, TN_DIMS, preferred_element_type=F32)
        yield

        parts = []
        for blk in GLA_COARSE_LEVELS:
            half = blk // 2
            ref = jnp.concatenate(
                [jnp.broadcast_to(b[s + half - 1:s + half, :], (half, GLA_HEAD_K))
                 for s in range(0, c_len, blk)], axis=0)
            ql = (gather_rows(qf, blk, True) * jnp.exp2(gather_rows(b, blk, True) - ref)).astype(BF16)
            kl = (gather_rows(kf, blk, False) * jnp.exp2(ref - gather_rows(b, blk, False))).astype(BF16)
            a_l = lax.dot_general(ql, kl, NT_DIMS, preferred_element_type=F32)
            yield
            if blk < c_len:
                a_l = jnp.where(hrow // half == hcol // half, a_l, 0.0)
            parts.append((blk, jnp.dot(a_l.astype(BF16), gather_rows(v, blk, False),
                                       preferred_element_type=F32)))
            yield

        attn = jnp.zeros((c_len, c_len), F32)
        for blk, mk in fine_masks:
            half = blk // 2
            ref = jnp.concatenate(
                [jnp.broadcast_to(b[s + half - 1:s + half, :], (blk, GLA_HEAD_K))
                 for s in range(0, c_len, blk)], axis=0)
            ql = (qf * jnp.exp2(b - ref)).astype(BF16)
            kl = (kf * jnp.exp2(ref - b)).astype(BF16)
            a_l = lax.dot_general(ql, kl, NT_DIMS, preferred_element_type=F32)
            yield
            attn = jnp.where(mk, a_l, attn)
        for d in range(GLA_DIAG):
            if d:
                prod = qf * pltpu.roll(kf, d, 0) * jnp.exp2(b - pltpu.roll(b, d, 0))
            else:
                prod = qf * kf
            attn = jnp.where(band == d, jnp.sum(prod, axis=-1, keepdims=True), attn)
        o += jnp.dot(attn.astype(BF16), v, preferred_element_type=F32)
        yield

        groups = [o[g:g + SUBLANES] for g in range(0, c_len, SUBLANES)]
        for blk, part in parts:
            half = blk // 2
            src = 0
            for s in range(0, c_len, blk):
                for g in range((s + half) // SUBLANES, (s + blk) // SUBLANES):
                    groups[g] = groups[g] + part[src:src + SUBLANES]
                    src += SUBLANES
        o = jnp.concatenate(groups, axis=0)

        ms = jnp.mean(o * o, axis=-1, keepdims=True)
        factor = scale * lax.rsqrt(scale * scale * ms + RMS_EPS)
        r = r_ref[0, sl, vsl].astype(F32)
        o_ref[0, sl, vsl] = (o * factor * gain * (r * jax.nn.sigmoid(r))).astype(o_ref.dtype)

    def chunk_group(c, carry):
        chains = []
        for u in range(GLA_CHUNKS_PER_ITER):
            sl = pl.ds(pl.multiple_of((c * GLA_CHUNKS_PER_ITER + u) * c_len, c_len), c_len)
            chains += [head_chunk(hd, sl) for hd in range(heads)]
        _lockstep(chains)
        return carry

    assert n_chunks % GLA_CHUNKS_PER_ITER == 0
    lax.fori_loop(0, n_chunks // GLA_CHUNKS_PER_ITER, chunk_group, 0)


def _gla_mix(qkvr, la, gain, b_sz, s_len, t_len, heads):
    n_chunks = t_len // GLA_CHUNK
    wk = heads * GLA_HEAD_K
    wv = heads * GLA_HEAD_V
    kq = GLA_KEY_DIM // wk
    kv = 2 * GLA_KEY_DIM // wv
    kr = kv + GLA_VAL_DIM // wv
    return pl.pallas_call(
        functools.partial(_gla_kernel, n_chunks=n_chunks, heads=heads),
        grid=(b_sz, GLA_HEADS // heads, s_len // t_len),
        in_specs=[
            pl.BlockSpec((1, t_len, wk), lambda b, h, t: (b, t, h)),
            pl.BlockSpec((1, t_len, wk), lambda b, h, t: (b, t, kq + h)),
            pl.BlockSpec((1, t_len, wv), lambda b, h, t: (b, t, kv + h)),
            pl.BlockSpec((1, t_len, wv), lambda b, h, t: (b, t, kr + h)),
            pl.BlockSpec((1, t_len, wk), lambda b, h, t: (b, t, h)),
            pl.BlockSpec((1, GLA_HEAD_V), lambda b, h, t: (0, 0)),
        ],
        out_specs=pl.BlockSpec((1, t_len, wv), lambda b, h, t: (b, t, h)),
        out_shape=jax.ShapeDtypeStruct((b_sz, s_len, GLA_VAL_DIM), BF16),
        scratch_shapes=[pltpu.VMEM((heads, GLA_HEAD_V, GLA_HEAD_K), F32)],
        compiler_params=_params("parallel", "parallel", "arbitrary"),
        name="gla_mix",
    )(qkvr, qkvr, qkvr, qkvr, la, gain)


def _residual_matmul_kernel(x_ref, a_ref, w_ref, o_ref, wb_ref):
    @pl.when(pl.program_id(0) == 0)
    def _():
        for r0 in range(0, w_ref.shape[1], NORM_SLAB):
            wb_ref[r0:r0 + NORM_SLAB, :] = w_ref[0, r0:r0 + NORM_SLAB, :].astype(BF16)

    o_ref[...] = x_ref[...] + jnp.dot(a_ref[...], wb_ref[...], preferred_element_type=F32)


def _residual_matmul(x2, a, w3, tm):
    m, k_dim = a.shape
    n = w3.shape[2]
    return pl.pallas_call(
        _residual_matmul_kernel,
        grid=(m // tm,),
        in_specs=[
            pl.BlockSpec((tm, n), lambda i: (i, 0)),
            pl.BlockSpec((tm, k_dim), lambda i: (i, 0)),
            pl.BlockSpec((1, k_dim, n), lambda i: (0, 0, 0), pipeline_mode=pl.Buffered(1)),
        ],
        out_specs=pl.BlockSpec((tm, n), lambda i: (i, 0)),
        out_shape=jax.ShapeDtypeStruct((m, n), F32),
        scratch_shapes=[pltpu.VMEM((k_dim, n), BF16)],
        compiler_params=_params("arbitrary"),
        name="residual_matmul",
    )(x2, a, w3)


def _ffn_kernel(x_ref, g_ref, wg_hbm, wu_hbm, wd_hbm, o_ref, h_ref, wg_buf, wu_buf, wd_buf, sem,
                *, layer, tf):
    i = pl.program_id(0)
    n_f = D_FF // tf
    assert n_f % 2 == 0

    def tile_copies(f, slot):
        cols = pl.ds(pl.multiple_of(f * tf, tf), tf)
        return (
            pltpu.make_async_copy(wg_hbm.at[layer, :, cols], wg_buf.at[slot], sem.at[0, slot]),
            pltpu.make_async_copy(wu_hbm.at[layer, :, cols], wu_buf.at[slot], sem.at[1, slot]),
            pltpu.make_async_copy(wd_hbm.at[layer, cols, :], wd_buf.at[slot], sem.at[2, slot]),
        )

    @pl.when(i == 0)
    def _():
        for cp in tile_copies(0, 0):
            cp.start()

    _norm_to_scratch(x_ref, g_ref, h_ref)

    def use_tile(f, slot, acc_ref):
        for cp in tile_copies(f, slot):
            cp.wait()
        nxt = jnp.where(f + 1 == n_f, 0, f + 1)
        for cp in tile_copies(nxt, 1 - slot):
            cp.start()
        h = h_ref[...]
        gate = jnp.dot(h, wg_buf[slot].astype(BF16), preferred_element_type=F32)
        up = jnp.dot(h, wu_buf[slot].astype(BF16), preferred_element_type=F32)
        act = (gate * jax.nn.sigmoid(gate) * up).astype(BF16)
        o_ref[...] = acc_ref[...] + jnp.dot(act, wd_buf[slot].astype(BF16),
                                            preferred_element_type=F32)

    def pair(p, carry):
        use_tile(2 * p, 0, o_ref)
        use_tile(2 * p + 1, 1, o_ref)
        return carry

    use_tile(0, 0, x_ref)
    use_tile(1, 1, o_ref)
    lax.fori_loop(1, n_f // 2, pair, 0)

    @pl.when(i == pl.num_programs(0) - 1)
    def _():
        for cp in tile_copies(0, 0):
            cp.wait()


def _ffn(x2, g, wg, wu, wd, layer, tm, tf):
    m = x2.shape[0]
    return pl.pallas_call(
        functools.partial(_ffn_kernel, layer=layer, tf=tf),
        grid=(m // tm,),
        in_specs=[
            pl.BlockSpec((tm, D_MODEL), lambda i: (i, 0)),
            pl.BlockSpec((1, D_MODEL), lambda i: (0, 0)),
            pl.BlockSpec(memory_space=pl.ANY),
            pl.BlockSpec(memory_space=pl.ANY),
            pl.BlockSpec(memory_space=pl.ANY),
        ],
        out_specs=pl.BlockSpec((tm, D_MODEL), lambda i: (i, 0)),
        out_shape=jax.ShapeDtypeStruct((m, D_MODEL), F32),
        scratch_shapes=[
            pltpu.VMEM((tm, D_MODEL), BF16),
            pltpu.VMEM((2, D_MODEL, tf), F32),
            pltpu.VMEM((2, D_MODEL, tf), F32),
            pltpu.VMEM((2, tf, D_MODEL), F32),
            pltpu.SemaphoreType.DMA((3, 2)),
        ],
        compiler_params=_params("arbitrary"),
        name="ffn",
    )(x2, g, wg, wu, wd)


def _fox_proj_kernel(x_ref, g_ref, w_ref, wf_ref, bf_ref, qkg_ref, o_ref, lf_ref, h_ref,
                     *, q_tiles, qk_tiles):
    j = pl.program_id(1)

    @pl.when(j == 0)
    def _():
        _norm_to_scratch(x_ref, g_ref, h_ref)
        h = h_ref[...]
        z = lax.dot_general(h, wf_ref[...].astype(BF16), NT_DIMS,
                            preferred_element_type=F32) + bf_ref[...]
        lf_ref[...] = _log_sigmoid(z)

    def project():
        return lax.dot_general(h_ref[...], w_ref[0].astype(BF16), NT_DIMS,
                               preferred_element_type=F32)

    @pl.when(j < qk_tiles)
    def _():
        acc = project()
        mult = jnp.where(j < q_tiles, FOX_SCALE * LOG2E, 1.0).astype(F32)
        for c in range(acc.shape[1] // FOX_HEAD_DIM):
            cs = slice(c * FOX_HEAD_DIM, (c + 1) * FOX_HEAD_DIM)
            xs = acc[:, cs]
            ms = jnp.mean(xs * xs, axis=-1, keepdims=True)
            o_ref[:, cs] = (xs * (lax.rsqrt(ms + RMS_EPS) * mult) * qkg_ref[:, cs]).astype(o_ref.dtype)

    @pl.when(j >= qk_tiles)
    def _():
        o_ref[...] = project().astype(o_ref.dtype)


def _fox_proj(x2, g, wt3, wf, bf, qk_gain, tm, tn, gap_at, gap):
    m = x2.shape[0]
    q_tiles = D_MODEL // tn
    qk_tiles = 2 * q_tiles
    assert gap_at % tn == 0 and gap % SUBLANES == 0
    first_after = gap_at // tn
    return pl.pallas_call(
        functools.partial(_fox_proj_kernel, q_tiles=q_tiles, qk_tiles=qk_tiles),
        grid=(m // tm, FOX_MAIN // tn),
        in_specs=[
            pl.BlockSpec((tm, D_MODEL), lambda i, j: (i, 0)),
            pl.BlockSpec((1, D_MODEL), lambda i, j: (0, 0)),
            pl.BlockSpec(
                (pl.Element(1), pl.Element(tn), pl.Element(D_MODEL)),
                lambda i, j: (0, pl.multiple_of(j * tn + jnp.where(j >= first_after, gap, 0),
                                                SUBLANES), 0)),
            pl.BlockSpec((LANES, D_MODEL), lambda i, j: (0, 0)),
            pl.BlockSpec((1, LANES), lambda i, j: (0, 0)),
            pl.BlockSpec((1, tn), lambda i, j: (0, jnp.minimum(j, qk_tiles - 1))),
        ],
        out_specs=[
            pl.BlockSpec((tm, tn), lambda i, j: (i, j)),
            pl.BlockSpec((tm, LANES), lambda i, j: (i, 0)),
        ],
        out_shape=[
            jax.ShapeDtypeStruct((m, FOX_MAIN), BF16),
            jax.ShapeDtypeStruct((m, LANES), F32),
        ],
        scratch_shapes=[pltpu.VMEM((tm, D_MODEL), BF16)],
        compiler_params=_params("parallel", "arbitrary"),
        name="fox_proj",
    )(x2, g, wt3, wf, bf, qk_gain)


def _cumsum_kernel(lf_ref, c_ref, *, t_len):
    row = lax.broadcasted_iota(jnp.int32, (t_len, t_len), 0)
    col = lax.broadcasted_iota(jnp.int32, (t_len, t_len), 1)
    tril = (col <= row).astype(BF16)

    def block(t, carry):
        sl = pl.ds(pl.multiple_of(t * t_len, t_len), t_len)
        c = _cumsum_rows(tril, lf_ref[0, sl, :]) + carry
        c_ref[0, sl, :] = c * LOG2E
        return c[t_len - 1:t_len, :]

    lax.fori_loop(0, lf_ref.shape[1] // t_len, block, jnp.zeros((1, LANES), F32))


def _seq_cumsum(lf, t_len):
    b_sz, s_len, _ = lf.shape
    return pl.pallas_call(
        functools.partial(_cumsum_kernel, t_len=t_len),
        grid=(b_sz,),
        in_specs=[pl.BlockSpec((1, s_len, LANES), lambda b: (b, 0, 0))],
        out_specs=pl.BlockSpec((1, s_len, LANES), lambda b: (b, 0, 0)),
        out_shape=jax.ShapeDtypeStruct(lf.shape, F32),
        compiler_params=_params("parallel"),
        name="fox_cumsum",
    )(lf)


def _fox_attn_kernel(ends_ref, firsts_ref, slack_ref, q_ref, k_ref, v_ref, og_ref, ck_ref, o_ref,
                     acc_ref, va_ref, *, blk, n_sub, tk):
    i = pl.program_id(2)
    sub = blk // n_sub
    assert sub == tk
    n_kt = ck_ref.shape[2]
    bh = pl.program_id(0) * pl.num_programs(1) + pl.program_id(1)
    n_loop = jnp.maximum(i * n_sub - 1, 0)
    limit = firsts_ref[bh * pl.num_programs(2) + i] + slack_ref[0]

    def first_needed(j, lo):
        return jnp.minimum(lo, jnp.where(ends_ref[bh * n_kt + j] > limit, n_loop, j))

    j_start = lax.fori_loop(0, n_loop, first_needed, n_loop)

    c0 = ck_ref[0, 0, pl.ds(i * n_sub, 1), :][:, 0:1]
    acc_ref[...] = jnp.zeros_like(acc_ref)

    @pl.when(i == 0)
    def _():
        for r0 in range(0, va_ref.shape[0], blk):
            va_ref[r0:r0 + blk, :FOX_HEAD_DIM] = v_ref[0, r0:r0 + blk, :]
            va_ref[r0:r0 + blk, FOX_HEAD_DIM:] = jnp.ones((blk, FOX_HEAD_DIM), BF16)

    def update(r, out, m_old, ks, va, bias, mask_offset):
        rows = slice(r * sub, (r + 1) * sub)
        s = lax.dot_general(q_ref[0, rows, :], ks, NT_DIMS, preferred_element_type=F32) + bias
        yield
        if mask_offset is not None:
            row = lax.broadcasted_iota(jnp.int32, s.shape, 0)
            col = lax.broadcasted_iota(jnp.int32, s.shape, 1)
            s = jnp.where(col <= row + mask_offset, s, -jnp.inf)
        m_new = jnp.maximum(m_old, jnp.max(s, axis=-1, keepdims=True))
        p = jnp.exp2(s - m_new).astype(BF16)
        alpha = jnp.exp2(m_old - m_new)
        acc_ref[rows, :] = alpha * acc_ref[rows, :] + jnp.dot(p, va, preferred_element_type=F32)
        out[r] = m_new

    def body(j, carry):
        sl = pl.ds(pl.multiple_of(j * tk, tk), tk)
        ks = k_ref[0, sl, :]
        va = va_ref[sl, :]
        bias = c0 - ck_ref[0, 0, pl.ds(j, 1), :]
        out = [None] * n_sub
        _lockstep([update(r, out, carry[r], ks, va, bias, None) for r in range(n_sub)])
        return tuple(out)

    init = tuple(jnp.full((sub, 1), -jnp.inf, F32) for _ in range(n_sub))
    carry = lax.fori_loop(j_start, n_loop, body, init)

    carry = list(carry)
    for r in range(1, n_sub):
        t_prev = i * n_sub + r - 1
        limit_r = ends_ref[bh * n_kt + t_prev] + slack_ref[0]
        for u in range(r):
            g = i * n_sub - 1 + u

            def visit(g=g, r=r):
                out = [None] * n_sub
                sl = pl.ds(pl.multiple_of(g * tk, tk), tk)
                bias = c0 - ck_ref[0, 0, pl.ds(g, 1), :]
                _lockstep([update(r, out, carry[r], k_ref[0, sl, :], va_ref[sl, :], bias, None)])
                return out[r]

            needed = (i > 0) & (ends_ref[bh * n_kt + jnp.maximum(g, 0)] <= limit_r)
            carry[r] = lax.cond(needed, visit, lambda r=r: carry[r])

    final = [None] * n_sub
    steps = []
    for r in range(n_sub):
        t_own = i * n_sub + r
        t_lo = jnp.maximum(t_own - 1, 0)
        win = pl.ds(pl.multiple_of(t_lo * tk, tk), 2 * tk)
        ck_win = jnp.concatenate(
            [ck_ref[0, 0, pl.ds(t_lo, 1), :], ck_ref[0, 0, pl.ds(t_lo + 1, 1), :]], axis=1)
        steps.append(update(r, final, carry[r], k_ref[0, win, :], va_ref[win, :],
                            c0 - ck_win, (t_own - t_lo) * tk))
    _lockstep(steps)
    for r in range(n_sub):
        rows = slice(r * sub, (r + 1) * sub)
        acc = acc_ref[rows, :]
        gate = jax.nn.sigmoid(og_ref[0, rows, :].astype(F32))
        o_ref[0, rows, :] = (acc[:, :FOX_HEAD_DIM] / acc[:, FOX_HEAD_DIM:FOX_HEAD_DIM + 1]
                             * gate).astype(o_ref.dtype)


def _fox_attn(qkvo, ck, slack, b_sz, s_len, blk, n_sub):
    h_cols = D_MODEL // FOX_HEAD_DIM
    n_blk = s_len // blk
    n_kt, tk = ck.shape[2:]
    ends = ck[:, :, :, tk - 1].reshape(-1)
    firsts = ck.reshape(b_sz, FOX_HEADS, n_blk, blk)[:, :, :, 0].reshape(-1)
    grid_spec = pltpu.PrefetchScalarGridSpec(
        num_scalar_prefetch=3,
        grid=(b_sz, FOX_HEADS, n_blk),
        in_specs=[
            pl.BlockSpec((1, blk, FOX_HEAD_DIM), lambda b, h, i, *_: (b, i, h)),
            pl.BlockSpec((1, s_len, FOX_HEAD_DIM), lambda b, h, i, *_: (b, 0, h_cols + h)),
            pl.BlockSpec((1, s_len, FOX_HEAD_DIM), lambda b, h, i, *_: (b, 0, 2 * h_cols + h)),
            pl.BlockSpec((1, blk, FOX_HEAD_DIM), lambda b, h, i, *_: (b, i, 3 * h_cols + h)),
            pl.BlockSpec((1, 1, n_kt, tk), lambda b, h, i, *_: (b, h, 0, 0)),
        ],
        out_specs=pl.BlockSpec((1, blk, FOX_HEAD_DIM), lambda b, h, i, *_: (b, i, h)),
        scratch_shapes=[
            pltpu.VMEM((blk, 2 * FOX_HEAD_DIM), F32),
            pltpu.VMEM((s_len, 2 * FOX_HEAD_DIM), BF16),
        ],
    )
    return pl.pallas_call(
        functools.partial(_fox_attn_kernel, blk=blk, n_sub=n_sub, tk=tk),
        grid_spec=grid_spec,
        out_shape=jax.ShapeDtypeStruct((b_sz, s_len, D_MODEL), BF16),
        compiler_params=_params("parallel", "parallel", "arbitrary"),
        name="fox_attn",
    )(ends, firsts, slack, qkvo, qkvo, qkvo, qkvo, ck)


def _pad_rows(w, n):
    return jnp.pad(w, ((0, n - w.shape[0]), (0, 0)))


def kernel(x, norm_mix, norm_ffn, gla_w_in, gla_w_g2, gla_b_g2, gla_o_gain, gla_w_o,
           fox_w_in, fox_b_f, fox_q_gain, fox_k_gain, fox_w_o,
           ffn_w_gate, ffn_w_up, ffn_w_down):
    b_sz, s_len, d = x.shape
    assert d == D_MODEL and s_len % GLA_CHUNK == 0
    m = b_sz * s_len
    tm = min(1024, m)
    tn = 1024
    tm_gla = min(2048, m)
    tm_res = min(512, m)
    tm_ffn = min(1024, m)
    tf = 256
    gla_t = min(512, s_len)
    attn_blk = min(1024, s_len)
    attn_tk = min(512, s_len // 2)
    attn_sub = attn_tk
    cum_t = min(256, s_len)

    x2 = x.reshape(m, d)

    def ffn_layer(xin, layer):
        return _ffn(xin, norm_ffn[layer][None, :], ffn_w_gate, ffn_w_up, ffn_w_down, layer, tm_ffn, tf)

    gla_wt = jnp.swapaxes(gla_w_in, 1, 2)
    h, la = _gla_norm(x2, norm_mix[0][None, :], _pad_rows(gla_wt[0, GLA_MAIN:, :], LANES),
                      _pad_rows(gla_w_g2[0], LANES), gla_b_g2[0][None, :], tm)
    qkvr = _gla_proj(h, gla_wt, tm_gla, tn)
    og = _gla_mix(qkvr.reshape(b_sz, s_len, GLA_MAIN), la.reshape(b_sz, s_len, GLA_KEY_DIM),
                  gla_o_gain[0][None, :], b_sz, s_len, gla_t, 4)
    x2 = _residual_matmul(x2, og.reshape(m, GLA_VAL_DIM), gla_w_o, tm_res)
    x2 = ffn_layer(x2, 0)

    qk_gain = jnp.concatenate([jnp.tile(fox_q_gain[0], FOX_HEADS), jnp.tile(fox_k_gain[0], FOX_HEADS)])
    f_lo, f_hi = 3 * D_MODEL, 3 * D_MODEL + FOX_HEADS
    fox_wt = jnp.swapaxes(fox_w_in, 1, 2)
    qkvo, lf = _fox_proj(
        x2, norm_mix[1][None, :],
        fox_wt,
        _pad_rows(fox_wt[0, f_lo:f_hi, :], LANES),
        jnp.pad(fox_b_f[0], (0, LANES - FOX_HEADS))[None, :],
        qk_gain[None, :], tm, tn, f_lo, FOX_HEADS)
    c = _seq_cumsum(lf.reshape(b_sz, s_len, LANES), cum_t)
    c_hs = c[:, :, :FOX_HEADS].transpose(0, 2, 1)
    ck = c_hs.reshape(b_sz, FOX_HEADS, s_len // attn_tk, attn_tk)
    qk_max = (FOX_HEAD_DIM * FOX_SCALE * LOG2E * BF16_SLOP
              * jnp.max(jnp.abs(fox_q_gain[0])) * jnp.max(jnp.abs(fox_k_gain[0])))
    slack = (2.0 * qk_max + F32_EXP2_FLOOR).reshape(1)
    o = _fox_attn(qkvo.reshape(b_sz, s_len, FOX_MAIN), ck, slack, b_sz, s_len, attn_blk,
                  attn_blk // attn_sub)
    x2 = _residual_matmul(x2, o.reshape(m, D_MODEL), fox_w_o, tm_res)
    x2 = ffn_layer(x2, 1)
    return x2.reshape(b_sz, s_len, d)
```

```python
import functools

import jax
import jax.numpy as jnp
from jax import lax
from jax.experimental import pallas as pl
from jax.experimental.pallas import tpu as pltpu

F32 = jnp.float32
BF16 = jnp.bfloat16

D_MODEL = 2048
RMS_EPS = 1e-6

GLA_HEADS = 4
GLA_KEY_DIM = D_MODEL // 2
GLA_VAL_DIM = D_MODEL
GLA_HEAD_K = GLA_KEY_DIM // GLA_HEADS
GLA_HEAD_V = GLA_VAL_DIM // GLA_HEADS
GLA_GATE_RANK = 16
GLA_GATE_TEMP = 16.0
GLA_CHUNK = 64
GLA_MAIN = 2 * GLA_KEY_DIM + 2 * GLA_VAL_DIM
GLA_COARSE_LEVELS = (64, 32)
GLA_FINE_LEVELS = (16, 8)
GLA_DIAG = 4
GLA_CHUNKS_PER_ITER = 2

FOX_HEAD_DIM = 128
FOX_HEADS = D_MODEL // FOX_HEAD_DIM
FOX_MAIN = 4 * D_MODEL
FOX_SCALE = FOX_HEAD_DIM ** -0.5
LOG2E = 1.4426950408889634
BF16_SLOP = 1.02
F32_EXP2_FLOOR = 1e30

D_FF = ((8 * D_MODEL + 2) // 3 + 255) // 256 * 256

LANES = 128
SUBLANES = 8
NORM_SLAB = 256
VMEM_LIMIT = 58 * 1024 * 1024

_DONE = object()


def _lockstep(chains):
    pending = list(chains)
    while pending:
        pending = [g for g in pending if next(g, _DONE) is not _DONE]


NT_DIMS = (((1,), (1,)), ((), ()))
TN_DIMS = (((0,), (0,)), ((), ()))


def _params(*sem):
    return pltpu.CompilerParams(dimension_semantics=sem, vmem_limit_bytes=VMEM_LIMIT)


def _rmsnorm_rows(x, g):
    ms = jnp.mean(x * x, axis=-1, keepdims=True)
    return x * lax.rsqrt(ms + RMS_EPS) * g


def _log_sigmoid(z):
    return jnp.minimum(z, 0.0) - jnp.log(1.0 + jnp.exp(-jnp.abs(z)))


def _norm_to_scratch(x_ref, g_ref, h_ref):
    rows = x_ref.shape[0]
    slab = min(NORM_SLAB, rows)
    g = g_ref[...]
    for r0 in range(0, rows, slab):
        h_ref[r0:r0 + slab, :] = _rmsnorm_rows(x_ref[r0:r0 + slab, :], g).astype(BF16)


def _split3(a):
    hi = a.astype(BF16)
    r1 = a - hi.astype(F32)
    mid = r1.astype(BF16)
    lo = (r1 - mid.astype(F32)).astype(BF16)
    return hi, mid, lo


def _cumsum_rows(tril, a, terms=3):
    parts = _split3(a)[:terms]
    out = jnp.dot(tril, parts[0], preferred_element_type=F32)
    for part in parts[1:]:
        out += jnp.dot(tril, part, preferred_element_type=F32)
    return out


def _gla_norm_kernel(x_ref, g_ref, wg1_ref, wg2_ref, bg_ref, h_ref, la_ref):
    _norm_to_scratch(x_ref, g_ref, h_ref)
    h = h_ref[...]
    g1 = lax.dot_general(h, wg1_ref[...].astype(BF16), NT_DIMS, preferred_element_type=F32)
    z = jnp.dot(g1.astype(BF16), wg2_ref[...].astype(BF16),
                preferred_element_type=F32) + bg_ref[...]
    la_ref[...] = _log_sigmoid(z) * (LOG2E / GLA_GATE_TEMP)


def _gla_norm(x2, g, wg1, wg2, bg, tm):
    m = x2.shape[0]
    return pl.pallas_call(
        _gla_norm_kernel,
        grid=(m // tm,),
        in_specs=[
            pl.BlockSpec((tm, D_MODEL), lambda i: (i, 0)),
            pl.BlockSpec((1, D_MODEL), lambda i: (0, 0)),
            pl.BlockSpec((LANES, D_MODEL), lambda i: (0, 0)),
            pl.BlockSpec((LANES, GLA_KEY_DIM), lambda i: (0, 0)),
            pl.BlockSpec((1, GLA_KEY_DIM), lambda i: (0, 0)),
        ],
        out_specs=[
            pl.BlockSpec((tm, D_MODEL), lambda i: (i, 0)),
            pl.BlockSpec((tm, GLA_KEY_DIM), lambda i: (i, 0)),
        ],
        out_shape=[
            jax.ShapeDtypeStruct((m, D_MODEL), BF16),
            jax.ShapeDtypeStruct((m, GLA_KEY_DIM), F32),
        ],
        compiler_params=_params("parallel"),
        name="gla_norm",
    )(x2, g, wg1, wg2, bg)


def _gla_proj_kernel(h_ref, w_ref, o_ref, wb_ref):
    @pl.when(pl.program_id(1) == 0)
    def _():
        for r0 in range(0, wb_ref.shape[0], NORM_SLAB):
            wb_ref[r0:r0 + NORM_SLAB, :] = w_ref[0, r0:r0 + NORM_SLAB, :].astype(BF16)

    o_ref[...] = lax.dot_general(h_ref[...], wb_ref[...], NT_DIMS,
                                 preferred_element_type=F32).astype(o_ref.dtype)


def _gla_proj(h, wt3, tm, tn):
    m = h.shape[0]
    return pl.pallas_call(
        _gla_proj_kernel,
        grid=(GLA_MAIN // tn, m // tm),
        in_specs=[
            pl.BlockSpec((tm, D_MODEL), lambda j, i: (i, 0)),
            pl.BlockSpec((1, tn, D_MODEL), lambda j, i: (0, j, 0)),
        ],
        out_specs=pl.BlockSpec((tm, tn), lambda j, i: (i, j)),
        out_shape=jax.ShapeDtypeStruct((m, GLA_MAIN), BF16),
        scratch_shapes=[pltpu.VMEM((tn, D_MODEL), BF16)],
        compiler_params=_params("arbitrary", "arbitrary"),
        name="gla_proj",
    )(h, wt3)


def _gla_kernel(q_ref, k_ref, v_ref, r_ref, la_ref, gain_ref, o_ref, st_ref, *, n_chunks, heads):
    c_len = GLA_CHUNK

    @pl.when(pl.program_id(2) == 0)
    def _():
        st_ref[...] = jnp.zeros_like(st_ref)

    row = lax.broadcasted_iota(jnp.int32, (c_len, c_len), 0)
    col = lax.broadcasted_iota(jnp.int32, (c_len, c_len), 1)
    tril = (col <= row).astype(BF16)
    delta = row - col
    band = jnp.where((delta >= 0) & (delta <= row % GLA_DIAG), delta, -1)
    fine_masks = [
        (blk, (row // blk == col // blk) & (row % blk >= blk // 2) & (col % blk < blk // 2))
        for blk in GLA_FINE_LEVELS]
    half_rows = c_len // 2
    hrow = lax.broadcasted_iota(jnp.int32, (half_rows, half_rows), 0)
    hcol = lax.broadcasted_iota(jnp.int32, (half_rows, half_rows), 1)
    gain = gain_ref[...]
    scale = GLA_HEAD_K ** -0.5

    def gather_rows(a, blk, second_half):
        half = blk // 2
        off = half if second_half else 0
        return jnp.concatenate([a[s + off:s + off + half] for s in range(0, c_len, blk)], axis=0)

    def head_chunk(hd, sl):
        ksl = slice(hd * GLA_HEAD_K, (hd + 1) * GLA_HEAD_K)
        vsl = slice(hd * GLA_HEAD_V, (hd + 1) * GLA_HEAD_V)
        la = la_ref[0, sl, ksl]
        b = _cumsum_rows(tril, la, terms=2)
        yield
        b_last = b[c_len - 1:c_len, :]
        qf = q_ref[0, sl, ksl].astype(F32)
        kf = k_ref[0, sl, ksl].astype(F32)
        v = v_ref[0, sl, vsl]

        st = st_ref[hd]
        qi = (qf * jnp.exp2(b)).astype(BF16)
        o = lax.dot_general(qi, st.astype(BF16), NT_DIMS, preferred_element_type=F32)
        kd = (kf * jnp.exp2(b_last - b)).astype(BF16)
        st_ref[hd] = st * jnp.exp2(b_last) + lax.dot_general(
            v, kd, TN_DIMS, preferred_element_type=F32)
        yield

        parts = []
        for blk in GLA_COARSE_LEVELS:
            half = blk // 2
            ref = jnp.concatenate(
                [jnp.broadcast_to(b[s + half - 1:s + half, :], (half, GLA_HEAD_K))
                 for s in range(0, c_len, blk)], axis=0)
            ql = (gather_rows(qf, blk, True) * jnp.exp2(gather_rows(b, blk, True) - ref)).astype(BF16)
            kl = (gather_rows(kf, blk, False) * jnp.exp2(ref - gather_rows(b, blk, False))).astype(BF16)
            a_l = lax.dot_general(ql, kl, NT_DIMS, preferred_element_type=F32)
            yield
            if blk < c_len:
                a_l = jnp.where(hrow // half == hcol // half, a_l, 0.0)
            parts.append((blk, jnp.dot(a_l.astype(BF16), gather_rows(v, blk, False),
                                       preferred_element_type=F32)))
            yield

        attn = jnp.zeros((c_len, c_len), F32)
        for blk, mk in fine_masks:
            half = blk // 2
            ref = jnp.concatenate(
                [jnp.broadcast_to(b[s + half - 1:s + half, :], (blk, GLA_HEAD_K))
                 for s in range(0, c_len, blk)], axis=0)
            ql = (qf * jnp.exp2(b - ref)).astype(BF16)
            kl = (kf * jnp.exp2(ref - b)).astype(BF16)
            a_l = lax.dot_general(ql, kl, NT_DIMS, preferred_element_type=F32)
            yield
            attn = jnp.where(mk, a_l, attn)
        for d in range(GLA_DIAG):
            if d:
                prod = qf * pltpu.roll(kf, d, 0) * jnp.exp2(b - pltpu.roll(b, d, 0))
            else:
                prod = qf * kf
            attn = jnp.where(band == d, jnp.sum(prod, axis=-1, keepdims=True), attn)
        o += jnp.dot(attn.astype(BF16), v, preferred_element_type=F32)
        yield

        groups = [o[g:g + SUBLANES] for g in range(0, c_len, SUBLANES)]
        for blk, part in parts:
            half = blk // 2
            src = 0
            for s in range(0, c_len, blk):
                for g in range((s + half) // SUBLANES, (s + blk) // SUBLANES):
                    groups[g] = groups[g] + part[src:src + SUBLANES]
                    src += SUBLANES
        o = jnp.concatenate(groups, axis=0)

        ms = jnp.mean(o * o, axis=-1, keepdims=True)
        factor = scale * lax.rsqrt(scale * scale * ms + RMS_EPS)
        r = r_ref[0, sl, vsl].astype(F32)
        o_ref[0, sl, vsl] = (o * factor * gain * (r * jax.nn.sigmoid(r))).astype(o_ref.dtype)

    def chunk_group(c, carry):
        chains = []
        for u in range(GLA_CHUNKS_PER_ITER):
            sl = pl.ds(pl.multiple_of((c * GLA_CHUNKS_PER_ITER + u) * c_len, c_len), c_len)
            chains += [head_chunk(hd, sl) for hd in range(heads)]
        _lockstep(chains)
        return carry

    assert n_chunks % GLA_CHUNKS_PER_ITER == 0
    lax.fori_loop(0, n_chunks // GLA_CHUNKS_PER_ITER, chunk_group, 0)


def _gla_mix(qkvr, la, gain, b_sz, s_len, t_len, heads):
    n_chunks = t_len // GLA_CHUNK
    wk = heads * GLA_HEAD_K
    wv = heads * GLA_HEAD_V
    kq = GLA_KEY_DIM // wk
    kv = 2 * GLA_KEY_DIM // wv
    kr = kv + GLA_VAL_DIM // wv
    return pl.pallas_call(
        functools.partial(_gla_kernel, n_chunks=n_chunks, heads=heads),
        grid=(b_sz, GLA_HEADS // heads, s_len // t_len),
        in_specs=[
            pl.BlockSpec((1, t_len, wk), lambda b, h, t: (b, t, h)),
            pl.BlockSpec((1, t_len, wk), lambda b, h, t: (b, t, kq + h)),
            pl.BlockSpec((1, t_len, wv), lambda b, h, t: (b, t, kv + h)),
            pl.BlockSpec((1, t_len, wv), lambda b, h, t: (b, t, kr + h)),
            pl.BlockSpec((1, t_len, wk), lambda b, h, t: (b, t, h)),
            pl.BlockSpec((1, GLA_HEAD_V), lambda b, h, t: (0, 0)),
        ],
        out_specs=pl.BlockSpec((1, t_len, wv), lambda b, h, t: (b, t, h)),
        out_shape=jax.ShapeDtypeStruct((b_sz, s_len, GLA_VAL_DIM), BF16),
        scratch_shapes=[pltpu.VMEM((heads, GLA_HEAD_V, GLA_HEAD_K), F32)],
        compiler_params=_params("parallel", "parallel", "arbitrary"),
        name="gla_mix",
    )(qkvr, qkvr, qkvr, qkvr, la, gain)


def _residual_matmul_kernel(x_ref, a_ref, w_ref, o_ref, wb_ref):
    @pl.when(pl.program_id(0) == 0)
    def _():
        for r0 in range(0, w_ref.shape[1], NORM_SLAB):
            wb_ref[r0:r0 + NORM_SLAB, :] = w_ref[0, r0:r0 + NORM_SLAB, :].astype(BF16)

    o_ref[...] = x_ref[...] + jnp.dot(a_ref[...], wb_ref[...], preferred_element_type=F32)


def _residual_matmul(x2, a, w3, tm):
    m, k_dim = a.shape
    n = w3.shape[2]
    return pl.pallas_call(
        _residual_matmul_kernel,
        grid=(m // tm,),
        in_specs=[
            pl.BlockSpec((tm, n), lambda i: (i, 0)),
            pl.BlockSpec((tm, k_dim), lambda i: (i, 0)),
            pl.BlockSpec((1, k_dim, n), lambda i: (0, 0, 0), pipeline_mode=pl.Buffered(1)),
        ],
        out_specs=pl.BlockSpec((tm, n), lambda i: (i, 0)),
        out_shape=jax.ShapeDtypeStruct((m, n), F32),
        scratch_shapes=[pltpu.VMEM((k_dim, n), BF16)],
        compiler_params=_params("arbitrary"),
        name="residual_matmul",
    )(x2, a, w3)


def _ffn_kernel(x_ref, g_ref, wg_hbm, wu_hbm, wd_hbm, o_ref, h_ref, wg_buf, wu_buf, wd_buf, sem,
                *, layer, tf):
    i = pl.program_id(0)
    n_f = D_FF // tf
    assert n_f % 2 == 0

    def tile_copies(f, slot):
        cols = pl.ds(pl.multiple_of(f * tf, tf), tf)
        return (
            pltpu.make_async_copy(wg_hbm.at[layer, :, cols], wg_buf.at[slot], sem.at[0, slot]),
            pltpu.make_async_copy(wu_hbm.at[layer, :, cols], wu_buf.at[slot], sem.at[1, slot]),
            pltpu.make_async_copy(wd_hbm.at[layer, cols, :], wd_buf.at[slot], sem.at[2, slot]),
        )

    @pl.when(i == 0)
    def _():
        for cp in tile_copies(0, 0):
            cp.start()

    _norm_to_scratch(x_ref, g_ref, h_ref)

    def use_tile(f, slot, acc_ref):
        for cp in tile_copies(f, slot):
            cp.wait()
        nxt = jnp.where(f + 1 == n_f, 0, f + 1)
        for cp in tile_copies(nxt, 1 - slot):
            cp.start()
        h = h_ref[...]
        gate = jnp.dot(h, wg_buf[slot].astype(BF16), preferred_element_type=F32)
        up = jnp.dot(h, wu_buf[slot].astype(BF16), preferred_element_type=F32)
        act = (gate * jax.nn.sigmoid(gate) * up).astype(BF16)
        o_ref[...] = acc_ref[...] + jnp.dot(act, wd_buf[slot].astype(BF16),
                                            preferred_element_type=F32)

    def pair(p, carry):
        use_tile(2 * p, 0, o_ref)
        use_tile(2 * p + 1, 1, o_ref)
        return carry

    use_tile(0, 0, x_ref)
    use_tile(1, 1, o_ref)
    lax.fori_loop(1, n_f // 2, pair, 0)

    @pl.when(i == pl.num_programs(0) - 1)
    def _():
        for cp in tile_copies(0, 0):
            cp.wait()


def _ffn(x2, g, wg, wu, wd, layer, tm, tf):
    m = x2.shape[0]
    return pl.pallas_call(
        functools.partial(_ffn_kernel, layer=layer, tf=tf),
        grid=(m // tm,),
        in_specs=[
            pl.BlockSpec((tm, D_MODEL), lambda i: (i, 0)),
            pl.BlockSpec((1, D_MODEL), lambda i: (0, 0)),
            pl.BlockSpec(memory_space=pl.ANY),
            pl.BlockSpec(memory_space=pl.ANY),
            pl.BlockSpec(memory_space=pl.ANY),
        ],
        out_specs=pl.BlockSpec((tm, D_MODEL), lambda i: (i, 0)),
        out_shape=jax.ShapeDtypeStruct((m, D_MODEL), F32),
        scratch_shapes=[
            pltpu.VMEM((tm, D_MODEL), BF16),
            pltpu.VMEM((2, D_MODEL, tf), F32),
            pltpu.VMEM((2, D_MODEL, tf), F32),
            pltpu.VMEM((2, tf, D_MODEL), F32),
            pltpu.SemaphoreType.DMA((3, 2)),
        ],
        compiler_params=_params("arbitrary"),
        name="ffn",
    )(x2, g, wg, wu, wd)


def _fox_proj_kernel(x_ref, g_ref, w_ref, wf_ref, bf_ref, qkg_ref, o_ref, lf_ref, h_ref,
                     *, q_tiles, qk_tiles):
    j = pl.program_id(1)

    @pl.when(j == 0)
    def _():
        _norm_to_scratch(x_ref, g_ref, h_ref)
        h = h_ref[...]
        z = lax.dot_general(h, wf_ref[...].astype(BF16), NT_DIMS,
                            preferred_element_type=F32) + bf_ref[...]
        lf_ref[...] = _log_sigmoid(z)

    def project():
        return lax.dot_general(h_ref[...], w_ref[0].astype(BF16), NT_DIMS,
                               preferred_element_type=F32)

    @pl.when(j < qk_tiles)
    def _():
        acc = project()
        mult = jnp.where(j < q_tiles, FOX_SCALE * LOG2E, 1.0).astype(F32)
        for c in range(acc.shape[1] // FOX_HEAD_DIM):
            cs = slice(c * FOX_HEAD_DIM, (c + 1) * FOX_HEAD_DIM)
            xs = acc[:, cs]
            ms = jnp.mean(xs * xs, axis=-1, keepdims=True)
            o_ref[:, cs] = (xs * (lax.rsqrt(ms + RMS_EPS) * mult) * qkg_ref[:, cs]).astype(o_ref.dtype)

    @pl.when(j >= qk_tiles)
    def _():
        o_ref[...] = project().astype(o_ref.dtype)


def _fox_proj(x2, g, wt3, wf, bf, qk_gain, tm, tn, gap_at, gap):
    m = x2.shape[0]
    q_tiles = D_MODEL // tn
    qk_tiles = 2 * q_tiles
    assert gap_at % tn == 0 and gap % SUBLANES == 0
    first_after = gap_at // tn
    return pl.pallas_call(
        functools.partial(_fox_proj_kernel, q_tiles=q_tiles, qk_tiles=qk_tiles),
        grid=(m // tm, FOX_MAIN // tn),
        in_specs=[
            pl.BlockSpec((tm, D_MODEL), lambda i, j: (i, 0)),
            pl.BlockSpec((1, D_MODEL), lambda i, j: (0, 0)),
            pl.BlockSpec(
                (pl.Element(1), pl.Element(tn), pl.Element(D_MODEL)),
                lambda i, j: (0, pl.multiple_of(j * tn + jnp.where(j >= first_after, gap, 0),
                                                SUBLANES), 0)),
            pl.BlockSpec((LANES, D_MODEL), lambda i, j: (0, 0)),
            pl.BlockSpec((1, LANES), lambda i, j: (0, 0)),
            pl.BlockSpec((1, tn), lambda i, j: (0, jnp.minimum(j, qk_tiles - 1))),
        ],
        out_specs=[
            pl.BlockSpec((tm, tn), lambda i, j: (i, j)),
            pl.BlockSpec((tm, LANES), lambda i, j: (i, 0)),
        ],
        out_shape=[
            jax.ShapeDtypeStruct((m, FOX_MAIN), BF16),
            jax.ShapeDtypeStruct((m, LANES), F32),
        ],
        scratch_shapes=[pltpu.VMEM((tm, D_MODEL), BF16)],
        compiler_params=_params("parallel", "arbitrary"),
        name="fox_proj",
    )(x2, g, wt3, wf, bf, qk_gain)


def _cumsum_kernel(lf_ref, c_ref, *, t_len):
    row = lax.broadcasted_iota(jnp.int32, (t_len, t_len), 0)
    col = lax.broadcasted_iota(jnp.int32, (t_len, t_len), 1)
    tril = (col <= row).astype(BF16)

    def block(t, carry):
        sl = pl.ds(pl.multiple_of(t * t_len, t_len), t_len)
        c = _cumsum_rows(tril, lf_ref[0, sl, :]) + carry
        c_ref[0, sl, :] = c * LOG2E
        return c[t_len - 1:t_len, :]

    lax.fori_loop(0, lf_ref.shape[1] // t_len, block, jnp.zeros((1, LANES), F32))


def _seq_cumsum(lf, t_len):
    b_sz, s_len, _ = lf.shape
    return pl.pallas_call(
        functools.partial(_cumsum_kernel, t_len=t_len),
        grid=(b_sz,),
        in_specs=[pl.BlockSpec((1, s_len, LANES), lambda b: (b, 0, 0))],
        out_specs=pl.BlockSpec((1, s_len, LANES), lambda b: (b, 0, 0)),
        out_shape=jax.ShapeDtypeStruct(lf.shape, F32),
        compiler_params=_params("parallel"),
        name="fox_cumsum",
    )(lf)


def _fox_attn_kernel(ends_ref, firsts_ref, slack_ref, q_ref, k_ref, v_ref, og_ref, ck_ref, o_ref,
                     acc_ref, va_ref, *, blk, n_sub, tk):
    i = pl.program_id(2)
    sub = blk // n_sub
    assert sub == tk
    n_kt = ck_ref.shape[2]
    bh = pl.program_id(0) * pl.num_programs(1) + pl.program_id(1)
    n_loop = jnp.maximum(i * n_sub - 1, 0)
    limit = firsts_ref[bh * pl.num_programs(2) + i] + slack_ref[0]

    def first_needed(j, lo):
        return jnp.minimum(lo, jnp.where(ends_ref[bh * n_kt + j] > limit, n_loop, j))

    j_start = lax.fori_loop(0, n_loop, first_needed, n_loop)

    c0 = ck_ref[0, 0, pl.ds(i * n_sub, 1), :][:, 0:1]
    acc_ref[...] = jnp.zeros_like(acc_ref)

    @pl.when(i == 0)
    def _():
        for r0 in range(0, va_ref.shape[0], blk):
            va_ref[r0:r0 + blk, :FOX_HEAD_DIM] = v_ref[0, r0:r0 + blk, :]
            va_ref[r0:r0 + blk, FOX_HEAD_DIM:] = jnp.ones((blk, FOX_HEAD_DIM), BF16)

    def update(r, out, m_old, ks, va, bias, mask_offset):
        rows = slice(r * sub, (r + 1) * sub)
        s = lax.dot_general(q_ref[0, rows, :], ks, NT_DIMS, preferred_element_type=F32) + bias
        yield
        if mask_offset is not None:
            row = lax.broadcasted_iota(jnp.int32, s.shape, 0)
            col = lax.broadcasted_iota(jnp.int32, s.shape, 1)
            s = jnp.where(col <= row + mask_offset, s, -jnp.inf)
        m_new = jnp.maximum(m_old, jnp.max(s, axis=-1, keepdims=True))
        p = jnp.exp2(s - m_new).astype(BF16)
        alpha = jnp.exp2(m_old - m_new)
        acc_ref[rows, :] = alpha * acc_ref[rows, :] + jnp.dot(p, va, preferred_element_type=F32)
        out[r] = m_new

    def body(j, carry):
        sl = pl.ds(pl.multiple_of(j * tk, tk), tk)
        ks = k_ref[0, sl, :]
        va = va_ref[sl, :]
        bias = c0 - ck_ref[0, 0, pl.ds(j, 1), :]
        out = [None] * n_sub
        _lockstep([update(r, out, carry[r], ks, va, bias, None) for r in range(n_sub)])
        return tuple(out)

    init = tuple(jnp.full((sub, 1), -jnp.inf, F32) for _ in range(n_sub))
    carry = lax.fori_loop(j_start, n_loop, body, init)

    carry = list(carry)
    for r in range(1, n_sub):
        t_prev = i * n_sub + r - 1
        limit_r = ends_ref[bh * n_kt + t_prev] + slack_ref[0]
        for u in range(r):
            g = i * n_sub - 1 + u

            def visit(g=g, r=r):
                out = [None] * n_sub
                sl = pl.ds(pl.multiple_of(g * tk, tk), tk)
                bias = c0 - ck_ref[0, 0, pl.ds(g, 1), :]
                _lockstep([update(r, out, carry[r], k_ref[0, sl, :], va_ref[sl, :], bias, None)])
                return out[r]

            needed = (i > 0) & (ends_ref[bh * n_kt + jnp.maximum(g, 0)] <= limit_r)
            carry[r] = lax.cond(needed, visit, lambda r=r: carry[r])

    final = [None] * n_sub
    steps = []
    for r in range(n_sub):
        t_own = i * n_sub + r
        t_lo = jnp.maximum(t_own - 1, 0)
        win = pl.ds(pl.multiple_of(t_lo * tk, tk), 2 * tk)
        ck_win = jnp.concatenate(
            [ck_ref[0, 0, pl.ds(t_lo, 1), :], ck_ref[0, 0, pl.ds(t_lo + 1, 1), :]], axis=1)
        steps.append(update(r, final, carry[r], k_ref[0, win, :], va_ref[win, :],
                            c0 - ck_win, (t_own - t_lo) * tk))
    _lockstep(steps)
    for r in range(n_sub):
        rows = slice(r * sub, (r + 1) * sub)
        acc = acc_ref[rows, :]
        gate = jax.nn.sigmoid(og_ref[0, rows, :].astype(F32))
        o_ref[0, rows, :] = (acc[:, :FOX_HEAD_DIM] / acc[:, FOX_HEAD_DIM:FOX_HEAD_DIM + 1]
                             * gate).astype(o_ref.dtype)


def _fox_attn(qkvo, ck, slack, b_sz, s_len, blk, n_sub):
    h_cols = D_MODEL // FOX_HEAD_DIM
    n_blk = s_len // blk
    n_kt, tk = ck.shape[2:]
    ends = ck[:, :, :, tk - 1].reshape(-1)
    firsts = ck.reshape(b_sz, FOX_HEADS, n_blk, blk)[:, :, :, 0].reshape(-1)
    grid_spec = pltpu.PrefetchScalarGridSpec(
        num_scalar_prefetch=3,
        grid=(b_sz, FOX_HEADS, n_blk),
        in_specs=[
            pl.BlockSpec((1, blk, FOX_HEAD_DIM), lambda b, h, i, *_: (b, i, h)),
            pl.BlockSpec((1, s_len, FOX_HEAD_DIM), lambda b, h, i, *_: (b, 0, h_cols + h)),
            pl.BlockSpec((1, s_len, FOX_HEAD_DIM), lambda b, h, i, *_: (b, 0, 2 * h_cols + h)),
            pl.BlockSpec((1, blk, FOX_HEAD_DIM), lambda b, h, i, *_: (b, i, 3 * h_cols + h)),
            pl.BlockSpec((1, 1, n_kt, tk), lambda b, h, i, *_: (b, h, 0, 0)),
        ],
        out_specs=pl.BlockSpec((1, blk, FOX_HEAD_DIM), lambda b, h, i, *_: (b, i, h)),
        scratch_shapes=[
            pltpu.VMEM((blk, 2 * FOX_HEAD_DIM), F32),
            pltpu.VMEM((s_len, 2 * FOX_HEAD_DIM), BF16),
        ],
    )
    return pl.pallas_call(
        functools.partial(_fox_attn_kernel, blk=blk, n_sub=n_sub, tk=tk),
        grid_spec=grid_spec,
        out_shape=jax.ShapeDtypeStruct((b_sz, s_len, D_MODEL), BF16),
        compiler_params=_params("parallel", "parallel", "arbitrary"),
        name="fox_attn",
    )(ends, firsts, slack, qkvo, qkvo, qkvo, qkvo, ck)


def _pad_rows(w, n):
    return jnp.pad(w, ((0, n - w.shape[0]), (0, 0)))


def kernel(x, norm_mix, norm_ffn, gla_w_in, gla_w_g2, gla_b_g2, gla_o_gain, gla_w_o,
           fox_w_in, fox_b_f, fox_q_gain, fox_k_gain, fox_w_o,
           ffn_w_gate, ffn_w_up, ffn_w_down):
    b_sz, s_len, d = x.shape
    assert d == D_MODEL and s_len % GLA_CHUNK == 0
    m = b_sz * s_len
    tm = min(1024, m)
    tn = 1024
    tm_gla = min(2048, m)
    tm_res = min(512, m)
    tm_ffn = min(1024, m)
    tf = 256
    gla_t = min(512, s_len)
    attn_blk = min(1024, s_len)
    attn_tk = min(512, s_len // 2)
    attn_sub = attn_tk
    cum_t = min(256, s_len)

    x2 = x.reshape(m, d)

    def ffn_layer(xin, layer):
        return _ffn(xin, norm_ffn[layer][None, :], ffn_w_gate, ffn_w_up, ffn_w_down, layer, tm_ffn, tf)

    gla_wt = jnp.swapaxes(gla_w_in, 1, 2)
    h, la = _gla_norm(x2, norm_mix[0][None, :], _pad_rows(gla_wt[0, GLA_MAIN:, :], LANES),
                      _pad_rows(gla_w_g2[0], LANES), gla_b_g2[0][None, :], tm)
    qkvr = _gla_proj(h, gla_wt, tm_gla, tn)
    og = _gla_mix(qkvr.reshape(b_sz, s_len, GLA_MAIN), la.reshape(b_sz, s_len, GLA_KEY_DIM),
                  gla_o_gain[0][None, :], b_sz, s_len, gla_t, 4)
    x2 = _residual_matmul(x2, og.reshape(m, GLA_VAL_DIM), gla_w_o, tm_res)
    x2 = ffn_layer(x2, 0)

    qk_gain = jnp.concatenate([jnp.tile(fox_q_gain[0], FOX_HEADS), jnp.tile(fox_k_gain[0], FOX_HEADS)])
    f_lo, f_hi = 3 * D_MODEL, 3 * D_MODEL + FOX_HEADS
    fox_wt = jnp.swapaxes(fox_w_in, 1, 2)
    qkvo, lf = _fox_proj(
        x2, norm_mix[1][None, :],
        fox_wt,
        _pad_rows(fox_wt[0, f_lo:f_hi, :], LANES),
        jnp.pad(fox_b_f[0], (0, LANES - FOX_HEADS))[None, :],
        qk_gain[None, :], tm, tn, f_lo, FOX_HEADS)
    c = _seq_cumsum(lf.reshape(b_sz, s_len, LANES), cum_t)
    c_hs = c[:, :, :FOX_HEADS].transpose(0, 2, 1)
    ck = c_hs.reshape(b_sz, FOX_HEADS, s_len // attn_tk, attn_tk)
    qk_max = (FOX_HEAD_DIM * FOX_SCALE * LOG2E * BF16_SLOP
              * jnp.max(jnp.abs(fox_q_gain[0])) * jnp.max(jnp.abs(fox_k_gain[0])))
    slack = (2.0 * qk_max + F32_EXP2_FLOOR).reshape(1)
    o = _fox_attn(qkvo.reshape(b_sz, s_len, FOX_MAIN), ck, slack, b_sz, s_len, attn_blk,
                  attn_blk // attn_sub)
    x2 = _residual_matmul(x2, o.reshape(m, D_MODEL), fox_w_o, tm_res)
    x2 = ffn_layer(x2, 1)
    return x2.reshape(b_sz, s_len, d)
```

```python
import functools

import jax
import jax.numpy as jnp
from jax import lax
from jax.experimental import pallas as pl
from jax.experimental.pallas import tpu as pltpu

F32 = jnp.float32
BF16 = jnp.bfloat16

D_MODEL = 2048
RMS_EPS = 1e-6

GLA_HEADS = 4
GLA_KEY_DIM = D_MODEL // 2
GLA_VAL_DIM = D_MODEL
GLA_HEAD_K = GLA_KEY_DIM // GLA_HEADS
GLA_HEAD_V = GLA_VAL_DIM // GLA_HEADS
GLA_GATE_RANK = 16
GLA_GATE_TEMP = 16.0
GLA_CHUNK = 64
GLA_MAIN = 2 * GLA_KEY_DIM + 2 * GLA_VAL_DIM
GLA_COARSE_LEVELS = (64, 32)
GLA_FINE_LEVELS = (16, 8)
GLA_DIAG = 4
GLA_CHUNKS_PER_ITER = 2

FOX_HEAD_DIM = 128
FOX_HEADS = D_MODEL // FOX_HEAD_DIM
FOX_MAIN = 4 * D_MODEL
FOX_SCALE = FOX_HEAD_DIM ** -0.5
LOG2E = 1.4426950408889634
BF16_SLOP = 1.02
F32_EXP2_FLOOR = 160.0

D_FF = ((8 * D_MODEL + 2) // 3 + 255) // 256 * 256

LANES = 128
SUBLANES = 8
NORM_SLAB = 256
VMEM_LIMIT = 58 * 1024 * 1024

_DONE = object()


def _lockstep(chains):
    pending = list(chains)
    while pending:
        pending = [g for g in pending if next(g, _DONE) is not _DONE]


NT_DIMS = (((1,), (1,)), ((), ()))
TN_DIMS = (((0,), (0,)), ((), ()))


def _params(*sem):
    return pltpu.CompilerParams(dimension_semantics=sem, vmem_limit_bytes=VMEM_LIMIT)


def _rmsnorm_rows(x, g):
    ms = jnp.mean(x * x, axis=-1, keepdims=True)
    return x * lax.rsqrt(ms + RMS_EPS) * g


def _log_sigmoid(z):
    return jnp.minimum(z, 0.0) - jnp.log(1.0 + jnp.exp(-jnp.abs(z)))


def _norm_to_scratch(x_ref, g_ref, h_ref):
    rows = x_ref.shape[0]
    slab = min(NORM_SLAB, rows)
    g = g_ref[...]
    for r0 in range(0, rows, slab):
        h_ref[r0:r0 + slab, :] = _rmsnorm_rows(x_ref[r0:r0 + slab, :], g).astype(BF16)


def _split3(a):
    hi = a.astype(BF16)
    r1 = a - hi.astype(F32)
    mid = r1.astype(BF16)
    lo = (r1 - mid.astype(F32)).astype(BF16)
    return hi, mid, lo


def _cumsum_rows(tril, a, terms=3):
    parts = _split3(a)[:terms]
    out = jnp.dot(tril, parts[0], preferred_element_type=F32)
    for part in parts[1:]:
        out += jnp.dot(tril, part, preferred_element_type=F32)
    return out


def _gla_norm_kernel(x_ref, g_ref, wg1_ref, wg2_ref, bg_ref, h_ref, la_ref):
    _norm_to_scratch(x_ref, g_ref, h_ref)
    h = h_ref[...]
    g1 = lax.dot_general(h, wg1_ref[...].astype(BF16), NT_DIMS, preferred_element_type=F32)
    z = jnp.dot(g1.astype(BF16), wg2_ref[...].astype(BF16),
                preferred_element_type=F32) + bg_ref[...]
    la_ref[...] = _log_sigmoid(z) * (LOG2E / GLA_GATE_TEMP)


def _gla_norm(x2, g, wg1, wg2, bg, tm):
    m = x2.shape[0]
    return pl.pallas_call(
        _gla_norm_kernel,
        grid=(m // tm,),
        in_specs=[
            pl.BlockSpec((tm, D_MODEL), lambda i: (i, 0)),
            pl.BlockSpec((1, D_MODEL), lambda i: (0, 0)),
            pl.BlockSpec((LANES, D_MODEL), lambda i: (0, 0)),
            pl.BlockSpec((LANES, GLA_KEY_DIM), lambda i: (0, 0)),
            pl.BlockSpec((1, GLA_KEY_DIM), lambda i: (0, 0)),
        ],
        out_specs=[
            pl.BlockSpec((tm, D_MODEL), lambda i: (i, 0)),
            pl.BlockSpec((tm, GLA_KEY_DIM), lambda i: (i, 0)),
        ],
        out_shape=[
            jax.ShapeDtypeStruct((m, D_MODEL), BF16),
            jax.ShapeDtypeStruct((m, GLA_KEY_DIM), F32),
        ],
        compiler_params=_params("parallel"),
        name="gla_norm",
    )(x2, g, wg1, wg2, bg)


def _gla_proj_kernel(h_ref, w_ref, o_ref, wb_ref):
    @pl.when(pl.program_id(1) == 0)
    def _():
        for r0 in range(0, wb_ref.shape[0], NORM_SLAB):
            wb_ref[r0:r0 + NORM_SLAB, :] = w_ref[0, r0:r0 + NORM_SLAB, :].astype(BF16)

    o_ref[...] = lax.dot_general(h_ref[...], wb_ref[...], NT_DIMS,
                                 preferred_element_type=F32).astype(o_ref.dtype)


def _gla_proj(h, wt3, tm, tn):
    m = h.shape[0]
    return pl.pallas_call(
        _gla_proj_kernel,
        grid=(GLA_MAIN // tn, m // tm),
        in_specs=[
            pl.BlockSpec((tm, D_MODEL), lambda j, i: (i, 0)),
            pl.BlockSpec((1, tn, D_MODEL), lambda j, i: (0, j, 0)),
        ],
        out_specs=pl.BlockSpec((tm, tn), lambda j, i: (i, j)),
        out_shape=jax.ShapeDtypeStruct((m, GLA_MAIN), BF16),
        scratch_shapes=[pltpu.VMEM((tn, D_MODEL), BF16)],
        compiler_params=_params("arbitrary", "arbitrary"),
        name="gla_proj",
    )(h, wt3)


def _gla_kernel(q_ref, k_ref, v_ref, r_ref, la_ref, gain_ref, o_ref, st_ref, *, n_chunks, heads):
    c_len = GLA_CHUNK

    @pl.when(pl.program_id(2) == 0)
    def _():
        st_ref[...] = jnp.zeros_like(st_ref)

    row = lax.broadcasted_iota(jnp.int32, (c_len, c_len), 0)
    col = lax.broadcasted_iota(jnp.int32, (c_len, c_len), 1)
    tril = (col <= row).astype(BF16)
    delta = row - col
    band = jnp.where((delta >= 0) & (delta <= row % GLA_DIAG), delta, -1)
    fine_masks = [
        (blk, (row // blk == col // blk) & (row % blk >= blk // 2) & (col % blk < blk // 2))
        for blk in GLA_FINE_LEVELS]
    half_rows = c_len // 2
    hrow = lax.broadcasted_iota(jnp.int32, (half_rows, half_rows), 0)
    hcol = lax.broadcasted_iota(jnp.int32, (half_rows, half_rows), 1)
    gain = gain_ref[...]
    scale = GLA_HEAD_K ** -0.5

    def gather_rows(a, blk, second_half):
        half = blk // 2
        off = half if second_half else 0
        return jnp.concatenate([a[s + off:s + off + half] for s in range(0, c_len, blk)], axis=0)

    def head_chunk(hd, sl):
        ksl = slice(hd * GLA_HEAD_K, (hd + 1) * GLA_HEAD_K)
        vsl = slice(hd * GLA_HEAD_V, (hd + 1) * GLA_HEAD_V)
        la = la_ref[0, sl, ksl]
        b = _cumsum_rows(tril, la, terms=2)
        yield
        b_last = b[c_len - 1:c_len, :]
        qf = q_ref[0, sl, ksl].astype(F32)
        kf = k_ref[0, sl, ksl].astype(F32)
        v = v_ref[0, sl, vsl]

        st = st_ref[hd]
        qi = (qf * jnp.exp2(b)).astype(BF16)
        o = lax.dot_general(qi, st.astype(BF16), NT_DIMS, preferred_element_type=F32)
        kd = (kf * jnp.exp2(b_last - b)).astype(BF16)
        st_ref[hd] = st * jnp.exp2(b_last) + lax.dot_general(
            v, kd, TN_DIMS, preferred_element_type=F32)
        yield

        parts = []
        for blk in GLA_COARSE_LEVELS:
            half = blk // 2
            ref = jnp.concatenate(
                [jnp.broadcast_to(b[s + half - 1:s + half, :], (half, GLA_HEAD_K))
                 for s in range(0, c_len, blk)], axis=0)
            ql = (gather_rows(qf, blk, True) * jnp.exp2(gather_rows(b, blk, True) - ref)).astype(BF16)
            kl = (gather_rows(kf, blk, False) * jnp.exp2(ref - gather_rows(b, blk, False))).astype(BF16)
            a_l = lax.dot_general(ql, kl, NT_DIMS, preferred_element_type=F32)
            yield
            if blk < c_len:
                a_l = jnp.where(hrow // half == hcol // half, a_l, 0.0)
            parts.append((blk, jnp.dot(a_l.astype(BF16), gather_rows(v, blk, False),
                                       preferred_element_type=F32)))
            yield

        attn = jnp.zeros((c_len, c_len), F32)
        for blk, mk in fine_masks:
            half = blk // 2
            ref = jnp.concatenate(
                [jnp.broadcast_to(b[s + half - 1:s + half, :], (blk, GLA_HEAD_K))
                 for s in range(0, c_len, blk)], axis=0)
            ql = (qf * jnp.exp2(b - ref)).astype(BF16)
            kl = (kf * jnp.exp2(ref - b)).astype(BF16)
            a_l = lax.dot_general(ql, kl, NT_DIMS, preferred_element_type=F32)
            yield
            attn = jnp.where(mk, a_l, attn)
        for d in range(GLA_DIAG):
            if d:
                prod = qf * pltpu.roll(kf, d, 0) * jnp.exp2(b - pltpu.roll(b, d, 0))
            else:
                prod = qf * kf
            attn = jnp.where(band == d, jnp.sum(prod, axis=-1, keepdims=True), attn)
        o += jnp.dot(attn.astype(BF16), v, preferred_element_type=F32)
        yield

        groups = [o[g:g + SUBLANES] for g in range(0, c_len, SUBLANES)]
        for blk, part in parts:
            half = blk // 2
            src = 0
            for s in range(0, c_len, blk):
                for g in range((s + half) // SUBLANES, (s + blk) // SUBLANES):
                    groups[g] = groups[g] + part[src:src + SUBLANES]
                    src += SUBLANES
        o = jnp.concatenate(groups, axis=0)

        ms = jnp.mean(o * o, axis=-1, keepdims=True)
        factor = scale * lax.rsqrt(scale * scale * ms + RMS_EPS)
        r = r_ref[0, sl, vsl].astype(F32)
        o_ref[0, sl, vsl] = (o * factor * gain * (r * jax.nn.sigmoid(r))).astype(o_ref.dtype)

    def chunk_group(c, carry):
        chains = []
        for u in range(GLA_CHUNKS_PER_ITER):
            sl = pl.ds(pl.multiple_of((c * GLA_CHUNKS_PER_ITER + u) * c_len, c_len), c_len)
            chains += [head_chunk(hd, sl) for hd in range(heads)]
        _lockstep(chains)
        return carry

    assert n_chunks % GLA_CHUNKS_PER_ITER == 0
    lax.fori_loop(0, n_chunks // GLA_CHUNKS_PER_ITER, chunk_group, 0)


def _gla_mix(qkvr, la, gain, b_sz, s_len, t_len, heads):
    n_chunks = t_len // GLA_CHUNK
    wk = heads * GLA_HEAD_K
    wv = heads * GLA_HEAD_V
    kq = GLA_KEY_DIM // wk
    kv = 2 * GLA_KEY_DIM // wv
    kr = kv + GLA_VAL_DIM // wv
    return pl.pallas_call(
        functools.partial(_gla_kernel, n_chunks=n_chunks, heads=heads),
        grid=(b_sz, GLA_HEADS // heads, s_len // t_len),
        in_specs=[
            pl.BlockSpec((1, t_len, wk), lambda b, h, t: (b, t, h)),
            pl.BlockSpec((1, t_len, wk), lambda b, h, t: (b, t, kq + h)),
            pl.BlockSpec((1, t_len, wv), lambda b, h, t: (b, t, kv + h)),
            pl.BlockSpec((1, t_len, wv), lambda b, h, t: (b, t, kr + h)),
            pl.BlockSpec((1, t_len, wk), lambda b, h, t: (b, t, h)),
            pl.BlockSpec((1, GLA_HEAD_V), lambda b, h, t: (0, 0)),
        ],
        out_specs=pl.BlockSpec((1, t_len, wv), lambda b, h, t: (b, t, h)),
        out_shape=jax.ShapeDtypeStruct((b_sz, s_len, GLA_VAL_DIM), BF16),
        scratch_shapes=[pltpu.VMEM((heads, GLA_HEAD_V, GLA_HEAD_K), F32)],
        compiler_params=_params("parallel", "parallel", "arbitrary"),
        name="gla_mix",
    )(qkvr, qkvr, qkvr, qkvr, la, gain)


def _residual_matmul_kernel(x_ref, a_ref, w_ref, o_ref, wb_ref):
    @pl.when(pl.program_id(0) == 0)
    def _():
        for r0 in range(0, w_ref.shape[1], NORM_SLAB):
            wb_ref[r0:r0 + NORM_SLAB, :] = w_ref[0, r0:r0 + NORM_SLAB, :].astype(BF16)

    o_ref[...] = x_ref[...] + jnp.dot(a_ref[...], wb_ref[...], preferred_element_type=F32)


def _residual_matmul(x2, a, w3, tm):
    m, k_dim = a.shape
    n = w3.shape[2]
    return pl.pallas_call(
        _residual_matmul_kernel,
        grid=(m // tm,),
        in_specs=[
            pl.BlockSpec((tm, n), lambda i: (i, 0)),
            pl.BlockSpec((tm, k_dim), lambda i: (i, 0)),
            pl.BlockSpec((1, k_dim, n), lambda i: (0, 0, 0), pipeline_mode=pl.Buffered(1)),
        ],
        out_specs=pl.BlockSpec((tm, n), lambda i: (i, 0)),
        out_shape=jax.ShapeDtypeStruct((m, n), F32),
        scratch_shapes=[pltpu.VMEM((k_dim, n), BF16)],
        compiler_params=_params("arbitrary"),
        name="residual_matmul",
    )(x2, a, w3)


def _ffn_kernel(x_ref, g_ref, wg_hbm, wu_hbm, wd_hbm, o_ref, h_ref, wg_buf, wu_buf, wd_buf, sem,
                *, layer, tf):
    i = pl.program_id(0)
    n_f = D_FF // tf
    assert n_f % 2 == 0

    def tile_copies(f, slot):
        cols = pl.ds(pl.multiple_of(f * tf, tf), tf)
        return (
            pltpu.make_async_copy(wg_hbm.at[layer, :, cols], wg_buf.at[slot], sem.at[0, slot]),
            pltpu.make_async_copy(wu_hbm.at[layer, :, cols], wu_buf.at[slot], sem.at[1, slot]),
            pltpu.make_async_copy(wd_hbm.at[layer, cols, :], wd_buf.at[slot], sem.at[2, slot]),
        )

    @pl.when(i == 0)
    def _():
        for cp in tile_copies(0, 0):
            cp.start()

    _norm_to_scratch(x_ref, g_ref, h_ref)

    def use_tile(f, slot, acc_ref):
        for cp in tile_copies(f, slot):
            cp.wait()
        nxt = jnp.where(f + 1 == n_f, 0, f + 1)
        for cp in tile_copies(nxt, 1 - slot):
            cp.start()
        h = h_ref[...]
        gate = jnp.dot(h, wg_buf[slot].astype(BF16), preferred_element_type=F32)
        up = jnp.dot(h, wu_buf[slot].astype(BF16), preferred_element_type=F32)
        act = (gate * jax.nn.sigmoid(gate) * up).astype(BF16)
        o_ref[...] = acc_ref[...] + jnp.dot(act, wd_buf[slot].astype(BF16),
                                            preferred_element_type=F32)

    def pair(p, carry):
        use_tile(2 * p, 0, o_ref)
        use_tile(2 * p + 1, 1, o_ref)
        return carry

    use_tile(0, 0, x_ref)
    use_tile(1, 1, o_ref)
    lax.fori_loop(1, n_f // 2, pair, 0)

    @pl.when(i == pl.num_programs(0) - 1)
    def _():
        for cp in tile_copies(0, 0):
            cp.wait()


def _ffn(x2, g, wg, wu, wd, layer, tm, tf):
    m = x2.shape[0]
    return pl.pallas_call(
        functools.partial(_ffn_kernel, layer=layer, tf=tf),
        grid=(m // tm,),
        in_specs=[
            pl.BlockSpec((tm, D_MODEL), lambda i: (i, 0)),
            pl.BlockSpec((1, D_MODEL), lambda i: (0, 0)),
            pl.BlockSpec(memory_space=pl.ANY),
            pl.BlockSpec(memory_space=pl.ANY),
            pl.BlockSpec(memory_space=pl.ANY),
        ],
        out_specs=pl.BlockSpec((tm, D_MODEL), lambda i: (i, 0)),
        out_shape=jax.ShapeDtypeStruct((m, D_MODEL), F32),
        scratch_shapes=[
            pltpu.VMEM((tm, D_MODEL), BF16),
            pltpu.VMEM((2, D_MODEL, tf), F32),
            pltpu.VMEM((2, D_MODEL, tf), F32),
            pltpu.VMEM((2, tf, D_MODEL), F32),
            pltpu.SemaphoreType.DMA((3, 2)),
        ],
        compiler_params=_params("arbitrary"),
        name="ffn",
    )(x2, g, wg, wu, wd)


def _fox_proj_kernel(x_ref, g_ref, w_ref, wf_ref, bf_ref, qkg_ref, o_ref, lf_ref, h_ref,
                     *, q_tiles, qk_tiles):
    j = pl.program_id(1)

    @pl.when(j == 0)
    def _():
        _norm_to_scratch(x_ref, g_ref, h_ref)
        h = h_ref[...]
        z = lax.dot_general(h, wf_ref[...].astype(BF16), NT_DIMS,
                            preferred_element_type=F32) + bf_ref[...]
        lf_ref[...] = _log_sigmoid(z)

    def project():
        return lax.dot_general(h_ref[...], w_ref[0].astype(BF16), NT_DIMS,
                               preferred_element_type=F32)

    @pl.when(j < qk_tiles)
    def _():
        acc = project()
        mult = jnp.where(j < q_tiles, FOX_SCALE * LOG2E, 1.0).astype(F32)
        for c in range(acc.shape[1] // FOX_HEAD_DIM):
            cs = slice(c * FOX_HEAD_DIM, (c + 1) * FOX_HEAD_DIM)
            xs = acc[:, cs]
            ms = jnp.mean(xs * xs, axis=-1, keepdims=True)
            o_ref[:, cs] = (xs * (lax.rsqrt(ms + RMS_EPS) * mult) * qkg_ref[:, cs]).astype(o_ref.dtype)

    @pl.when(j >= qk_tiles)
    def _():
        o_ref[...] = project().astype(o_ref.dtype)


def _fox_proj(x2, g, wt3, wf, bf, qk_gain, tm, tn, gap_at, gap):
    m = x2.shape[0]
    q_tiles = D_MODEL // tn
    qk_tiles = 2 * q_tiles
    assert gap_at % tn == 0 and gap % SUBLANES == 0
    first_after = gap_at // tn
    return pl.pallas_call(
        functools.partial(_fox_proj_kernel, q_tiles=q_tiles, qk_tiles=qk_tiles),
        grid=(m // tm, FOX_MAIN // tn),
        in_specs=[
            pl.BlockSpec((tm, D_MODEL), lambda i, j: (i, 0)),
            pl.BlockSpec((1, D_MODEL), lambda i, j: (0, 0)),
            pl.BlockSpec(
                (pl.Element(1), pl.Element(tn), pl.Element(D_MODEL)),
                lambda i, j: (0, pl.multiple_of(j * tn + jnp.where(j >= first_after, gap, 0),
                                                SUBLANES), 0)),
            pl.BlockSpec((LANES, D_MODEL), lambda i, j: (0, 0)),
            pl.BlockSpec((1, LANES), lambda i, j: (0, 0)),
            pl.BlockSpec((1, tn), lambda i, j: (0, jnp.minimum(j, qk_tiles - 1))),
        ],
        out_specs=[
            pl.BlockSpec((tm, tn), lambda i, j: (i, j)),
            pl.BlockSpec((tm, LANES), lambda i, j: (i, 0)),
        ],
        out_shape=[
            jax.ShapeDtypeStruct((m, FOX_MAIN), BF16),
            jax.ShapeDtypeStruct((m, LANES), F32),
        ],
        scratch_shapes=[pltpu.VMEM((tm, D_MODEL), BF16)],
        compiler_params=_params("parallel", "arbitrary"),
        name="fox_proj",
    )(x2, g, wt3, wf, bf, qk_gain)


def _cumsum_kernel(lf_ref, c_ref, *, t_len):
    row = lax.broadcasted_iota(jnp.int32, (t_len, t_len), 0)
    col = lax.broadcasted_iota(jnp.int32, (t_len, t_len), 1)
    tril = (col <= row).astype(BF16)

    def block(t, carry):
        sl = pl.ds(pl.multiple_of(t * t_len, t_len), t_len)
        c = _cumsum_rows(tril, lf_ref[0, sl, :]) + carry
        c_ref[0, sl, :] = c * LOG2E
        return c[t_len - 1:t_len, :]

    lax.fori_loop(0, lf_ref.shape[1] // t_len, block, jnp.zeros((1, LANES), F32))


def _seq_cumsum(lf, t_len):
    b_sz, s_len, _ = lf.shape
    return pl.pallas_call(
        functools.partial(_cumsum_kernel, t_len=t_len),
        grid=(b_sz,),
        in_specs=[pl.BlockSpec((1, s_len, LANES), lambda b: (b, 0, 0))],
        out_specs=pl.BlockSpec((1, s_len, LANES), lambda b: (b, 0, 0)),
        out_shape=jax.ShapeDtypeStruct(lf.shape, F32),
        compiler_params=_params("parallel"),
        name="fox_cumsum",
    )(lf)


def _fox_attn_kernel(ends_ref, firsts_ref, slack_ref, q_ref, k_ref, v_ref, og_ref, ck_ref, o_ref,
                     acc_ref, va_ref, kp_ref, *, blk, n_sub, tk):
    i = pl.program_id(2)
    sub = blk // n_sub
    assert sub == tk
    n_kt = ck_ref.shape[2]
    bh = pl.program_id(0) * pl.num_programs(1) + pl.program_id(1)
    n_loop = jnp.maximum(i * n_sub - 1, 0)
    limit = firsts_ref[bh * pl.num_programs(2) + i] + slack_ref[0]

    def first_needed(j, lo):
        return jnp.minimum(lo, jnp.where(ends_ref[bh * n_kt + j] > limit, n_loop, j))

    j_start = lax.fori_loop(0, n_loop, first_needed, n_loop)

    c0 = ck_ref[0, 0, pl.ds(i * n_sub, 1), :][:, 0:1]
    acc_ref[...] = jnp.zeros_like(acc_ref)

    @pl.when(i == 0)
    def _():
        kp_ref[0:tk, :] = jnp.zeros((tk, FOX_HEAD_DIM), BF16)
        va_ref[0:tk, :] = jnp.zeros((tk, 2 * FOX_HEAD_DIM), BF16)
        for r0 in range(0, k_ref.shape[1], blk):
            kp_ref[tk + r0:tk + r0 + blk, :] = k_ref[0, r0:r0 + blk, :]
            va_ref[tk + r0:tk + r0 + blk, :FOX_HEAD_DIM] = v_ref[0, r0:r0 + blk, :]
            va_ref[tk + r0:tk + r0 + blk, FOX_HEAD_DIM:] = jnp.ones((blk, FOX_HEAD_DIM), BF16)

    causal = (lax.broadcasted_iota(jnp.int32, (sub, tk), 1)
              <= lax.broadcasted_iota(jnp.int32, (sub, tk), 0))

    def update(r, out, m_old, ks, va, bias, has_prev):
        rows = slice(r * sub, (r + 1) * sub)
        s = lax.dot_general(q_ref[0, rows, :], ks, NT_DIMS, preferred_element_type=F32) + bias
        yield
        if has_prev is not None:
            s_prev = s[:, :tk]
            if has_prev is not True:
                s_prev = jnp.where(has_prev, s_prev, -jnp.inf)
            s = jnp.concatenate([s_prev, jnp.where(causal, s[:, tk:], -jnp.inf)], axis=1)
        m_new = jnp.maximum(m_old, jnp.max(s, axis=-1, keepdims=True))
        p = jnp.exp2(s - m_new).astype(BF16)
        alpha = jnp.exp2(m_old - m_new)
        acc_ref[rows, :] = alpha * acc_ref[rows, :] + jnp.dot(p, va, preferred_element_type=F32)
        out[r] = m_new

    def tile(j):
        return pl.ds(pl.multiple_of((j + 1) * tk, tk), tk)

    def body(j, carry):
        ks = kp_ref[tile(j), :]
        va = va_ref[tile(j), :]
        bias = c0 - ck_ref[0, 0, pl.ds(j, 1), :]
        out = [None] * n_sub
        _lockstep([update(r, out, carry[r], ks, va, bias, None) for r in range(n_sub)])
        return tuple(out)

    init = tuple(jnp.full((sub, 1), -jnp.inf, F32) for _ in range(n_sub))
    carry = lax.fori_loop(j_start, n_loop, body, init)

    carry = list(carry)
    for r in range(1, n_sub):
        t_prev = i * n_sub + r - 1
        limit_r = ends_ref[bh * n_kt + t_prev] + slack_ref[0]
        for u in range(r):
            g = i * n_sub - 1 + u

            def visit(g=g, r=r):
                out = [None] * n_sub
                bias = c0 - ck_ref[0, 0, pl.ds(g, 1), :]
                _lockstep([update(r, out, carry[r], kp_ref[tile(g), :], va_ref[tile(g), :],
                                  bias, None)])
                return out[r]

            needed = (i > 0) & (ends_ref[bh * n_kt + jnp.maximum(g, 0)] <= limit_r)
            carry[r] = lax.cond(needed, visit, lambda r=r: carry[r])

    final = [None] * n_sub
    steps = []
    for r in range(n_sub):
        t_own = i * n_sub + r
        win = pl.ds(pl.multiple_of(t_own * tk, tk), 2 * tk)
        ck_win = jnp.concatenate(
            [ck_ref[0, 0, pl.ds(jnp.maximum(t_own - 1, 0), 1), :],
             ck_ref[0, 0, pl.ds(t_own, 1), :]], axis=1)
        steps.append(update(r, final, carry[r], kp_ref[win, :], va_ref[win, :], c0 - ck_win,
                            True if r else i > 0))
    _lockstep(steps)
    for r in range(n_sub):
        rows = slice(r * sub, (r + 1) * sub)
        acc = acc_ref[rows, :]
        gate = jax.nn.sigmoid(og_ref[0, rows, :].astype(F32))
        o_ref[0, rows, :] = (acc[:, :FOX_HEAD_DIM] / acc[:, FOX_HEAD_DIM:FOX_HEAD_DIM + 1]
                             * gate).astype(o_ref.dtype)


def _fox_attn(qkvo, ck, slack, b_sz, s_len, blk, n_sub):
    h_cols = D_MODEL // FOX_HEAD_DIM
    n_blk = s_len // blk
    n_kt, tk = ck.shape[2:]
    ends = ck[:, :, :, tk - 1].reshape(-1)
    firsts = ck.reshape(b_sz, FOX_HEADS, n_blk, blk)[:, :, :, 0].reshape(-1)
    grid_spec = pltpu.PrefetchScalarGridSpec(
        num_scalar_prefetch=3,
        grid=(b_sz, FOX_HEADS, n_blk),
        in_specs=[
            pl.BlockSpec((1, blk, FOX_HEAD_DIM), lambda b, h, i, *_: (b, i, h)),
            pl.BlockSpec((1, s_len, FOX_HEAD_DIM), lambda b, h, i, *_: (b, 0, h_cols + h)),
            pl.BlockSpec((1, s_len, FOX_HEAD_DIM), lambda b, h, i, *_: (b, 0, 2 * h_cols + h)),
            pl.BlockSpec((1, blk, FOX_HEAD_DIM), lambda b, h, i, *_: (b, i, 3 * h_cols + h)),
            pl.BlockSpec((1, 1, n_kt, tk), lambda b, h, i, *_: (b, h, 0, 0)),
        ],
        out_specs=pl.BlockSpec((1, blk, FOX_HEAD_DIM), lambda b, h, i, *_: (b, i, h)),
        scratch_shapes=[
            pltpu.VMEM((blk, 2 * FOX_HEAD_DIM), F32),
            pltpu.VMEM((s_len + tk, 2 * FOX_HEAD_DIM), BF16),
            pltpu.VMEM((s_len + tk, FOX_HEAD_DIM), BF16),
        ],
    )
    return pl.pallas_call(
        functools.partial(_fox_attn_kernel, blk=blk, n_sub=n_sub, tk=tk),
        grid_spec=grid_spec,
        out_shape=jax.ShapeDtypeStruct((b_sz, s_len, D_MODEL), BF16),
        compiler_params=_params("parallel", "parallel", "arbitrary"),
        name="fox_attn",
    )(ends, firsts, slack, qkvo, qkvo, qkvo, qkvo, ck)


def _pad_rows(w, n):
    return jnp.pad(w, ((0, n - w.shape[0]), (0, 0)))


def kernel(x, norm_mix, norm_ffn, gla_w_in, gla_w_g2, gla_b_g2, gla_o_gain, gla_w_o,
           fox_w_in, fox_b_f, fox_q_gain, fox_k_gain, fox_w_o,
           ffn_w_gate, ffn_w_up, ffn_w_down):
    b_sz, s_len, d = x.shape
    assert d == D_MODEL and s_len % GLA_CHUNK == 0
    m = b_sz * s_len
    tm = min(1024, m)
    tn = 1024
    tm_gla = min(2048, m)
    tm_res = min(512, m)
    tm_ffn = min(1024, m)
    tf = 256
    gla_t = min(512, s_len)
    attn_blk = min(1024, s_len)
    attn_tk = min(512, s_len // 2)
    attn_sub = attn_tk
    cum_t = min(256, s_len)

    x2 = x.reshape(m, d)

    def ffn_layer(xin, layer):
        return _ffn(xin, norm_ffn[layer][None, :], ffn_w_gate, ffn_w_up, ffn_w_down, layer, tm_ffn, tf)

    gla_wt = jnp.swapaxes(gla_w_in, 1, 2)
    h, la = _gla_norm(x2, norm_mix[0][None, :], _pad_rows(gla_wt[0, GLA_MAIN:, :], LANES),
                      _pad_rows(gla_w_g2[0], LANES), gla_b_g2[0][None, :], tm)
    qkvr = _gla_proj(h, gla_wt, tm_gla, tn)
    og = _gla_mix(qkvr.reshape(b_sz, s_len, GLA_MAIN), la.reshape(b_sz, s_len, GLA_KEY_DIM),
                  gla_o_gain[0][None, :], b_sz, s_len, gla_t, 4)
    x2 = _residual_matmul(x2, og.reshape(m, GLA_VAL_DIM), gla_w_o, tm_res)
    x2 = ffn_layer(x2, 0)

    qk_gain = jnp.concatenate([jnp.tile(fox_q_gain[0], FOX_HEADS), jnp.tile(fox_k_gain[0], FOX_HEADS)])
    f_lo, f_hi = 3 * D_MODEL, 3 * D_MODEL + FOX_HEADS
    fox_wt = jnp.swapaxes(fox_w_in, 1, 2)
    qkvo, lf = _fox_proj(
        x2, norm_mix[1][None, :],
        fox_wt,
        _pad_rows(fox_wt[0, f_lo:f_hi, :], LANES),
        jnp.pad(fox_b_f[0], (0, LANES - FOX_HEADS))[None, :],
        qk_gain[None, :], tm, tn, f_lo, FOX_HEADS)
    c = _seq_cumsum(lf.reshape(b_sz, s_len, LANES), cum_t)
    c_hs = c[:, :, :FOX_HEADS].transpose(0, 2, 1)
    ck = c_hs.reshape(b_sz, FOX_HEADS, s_len // attn_tk, attn_tk)
    qk_max = (FOX_HEAD_DIM * FOX_SCALE * LOG2E * BF16_SLOP
              * jnp.max(jnp.abs(fox_q_gain[0])) * jnp.max(jnp.abs(fox_k_gain[0])))
    slack = (2.0 * qk_max + F32_EXP2_FLOOR).reshape(1)
    o = _fox_attn(qkvo.reshape(b_sz, s_len, FOX_MAIN), ck, slack, b_sz, s_len, attn_blk,
                  attn_blk // attn_sub)
    x2 = _residual_matmul(x2, o.reshape(m, D_MODEL), fox_w_o, tm_res)
    x2 = ffn_layer(x2, 1)
    return x2.reshape(b_sz, s_len, d)
```
